```python
import jax, jax.numpy as jnp
from jax import lax
import numpy as np

D_MODEL = 1024
BATCH = 32
SEQ = 2048
DEPTH = 2

GRID_W = 64
CTX_LEN = 256
EPS = 1e-6
ROPE_THETA = 10000.0
Q_BLOCK = 128
CHUNK = 128
SGU_GROUPS = 4
SGU_GROUP_CH = 128
SGU_WIDTH = SGU_GROUPS * SGU_GROUP_CH
GQA_HEADS = 4
GQA_KV_HEADS = 2
GQA_GROUP = GQA_HEADS // GQA_KV_HEADS
HEAD_DIM = 128
GQA_Q_W = GQA_HEADS * HEAD_DIM
GQA_KV_W = GQA_KV_HEADS * HEAD_DIM
A_END = 2 * SGU_WIDTH
K_OFF = A_END + GQA_Q_W
V_OFF = K_OFF + GQA_KV_W
AB_IN_W = V_OFF + GQA_KV_W
AB_MIX_W = SGU_WIDTH + GQA_Q_W
MLA_HEADS = 8
MLA_Q_LORA = 512
MLA_KV_LORA = 256
MLA_NOPE = 128
MLA_ROPE = 64
MLA_V = 128
MLA_QK = MLA_NOPE + MLA_ROPE
MLA_IN_W = MLA_Q_LORA + MLA_KV_LORA + MLA_ROPE
N_EXPERTS = 32
TOP_K = 4
D_EXPERT = 1024
SWIGLU_LIMIT = 7.0
SWIGLU_ALPHA = 1.702
MOE_BLOCK = 128

kernel_name = "hybrid_gmlp_gqa_mla_moe_dit"


def rms_norm(x, g):
    xf = x.astype(jnp.float32)
    y = xf * lax.rsqrt(jnp.mean(xf * xf, axis=-1, keepdims=True) + EPS)
    return (y * g).astype(x.dtype)


def grid_angles(rows, rot_dim):
    t = jnp.arange(rows * GRID_W, dtype=jnp.int32)
    row = (t // GRID_W).astype(jnp.float32)
    col = (t % GRID_W).astype(jnp.float32)
    axis_dim = rot_dim // 2
    inv_freq = ROPE_THETA ** (-jnp.arange(0, axis_dim, 2, dtype=jnp.float32) / axis_dim)
    return row[:, None] * inv_freq, col[:, None] * inv_freq


def rope_1d(x, ang):
    x1, x2 = jnp.split(x, 2, axis=-1)
    cos = jnp.cos(ang)[:, None, :]
    sin = jnp.sin(ang)[:, None, :]
    return jnp.concatenate([x1 * cos - x2 * sin, x2 * cos + x1 * sin], axis=-1)


def rope_2d(x, ang_r, ang_c):
    xf = x.astype(jnp.float32)
    xr, xc = jnp.split(xf, 2, axis=-1)
    return jnp.concatenate([rope_1d(xr, ang_r), rope_1d(xc, ang_c)], axis=-1).astype(x.dtype)


def blocked_attention(q, k, v):
    Bn, Hk, G, Tq, dk = q.shape
    nb = Tq // Q_BLOCK
    qb = jnp.moveaxis(q.reshape(Bn, Hk, G, nb, Q_BLOCK, dk), 3, 0)
    scale = dk ** -0.5

    def one_block(qblk):
        s = jnp.einsum('bhgqd,bhkd->bhgqk', qblk, k, preferred_element_type=jnp.float32) * scale
        p = jax.nn.softmax(s, axis=-1).astype(v.dtype)
        return jnp.einsum('bhgqk,bhkd->bhgqd', p, v)

    out = lax.map(one_block, qb)
    return jnp.moveaxis(out, 0, 3).reshape(Bn, Hk, G, Tq, v.shape[-1])


def chunk_gating(a_uv, g_norm, w_s, b_s):
    Bn, T, _ = a_uv.shape
    z = jax.nn.gelu(a_uv)
    u, v = jnp.split(z, 2, axis=-1)
    v = v.reshape(Bn, T // CHUNK, CHUNK, SGU_GROUPS, SGU_GROUP_CH)
    v = rms_norm(v, g_norm.reshape(SGU_GROUPS, SGU_GROUP_CH))
    v = jnp.einsum('gpq,bnqgc->bnpgc', w_s, v) + jnp.transpose(b_s)[:, :, None]
    return u * v.reshape(Bn, T, SGU_WIDTH)


def gqa_heads(q, k, v):
    Bn, Tq = q.shape[:2]
    qt = q.reshape(Bn, Tq, GQA_KV_HEADS, GQA_GROUP, HEAD_DIM).transpose(0, 2, 3, 1, 4)
    o = blocked_attention(qt, k.transpose(0, 2, 1, 3), v.transpose(0, 2, 1, 3))
    return o.transpose(0, 3, 1, 2, 4).reshape(Bn, Tq, GQA_Q_W)


def ab_mixer(h, hc, w_in, sgu_g, sgu_w, sgu_b, q_g, k_g, w_out, ang, ctx_out):
    Bn, S, _ = h.shape
    L = hc.shape[1]
    p = h @ w_in
    out_a = chunk_gating(p[..., :A_END], sgu_g, sgu_w, sgu_b)
    q = rope_2d(rms_norm(p[..., A_END:K_OFF].reshape(Bn, S, GQA_HEADS, HEAD_DIM), q_g), *ang)
    k = rope_2d(rms_norm(p[..., K_OFF:V_OFF].reshape(Bn, S, GQA_KV_HEADS, HEAD_DIM), k_g), *ang)
    v = p[..., V_OFF:].reshape(Bn, S, GQA_KV_HEADS, HEAD_DIM)
    pc = hc @ (w_in if ctx_out else w_in[:, K_OFF:])
    pc_kv = pc[..., K_OFF:] if ctx_out else pc
    kc = rms_norm(pc_kv[..., :GQA_KV_W].reshape(Bn, L, GQA_KV_HEADS, HEAD_DIM), k_g)
    vc = pc_kv[..., GQA_KV_W:].reshape(Bn, L, GQA_KV_HEADS, HEAD_DIM)
    out_b = gqa_heads(q, jnp.concatenate([kc, k], axis=1), jnp.concatenate([vc, v], axis=1))
    y = jnp.concatenate([out_a, out_b], axis=-1) @ w_out
    if not ctx_out:
        return y, None
    out_ac = chunk_gating(pc[..., :A_END], sgu_g, sgu_w, sgu_b)
    qc = rms_norm(pc[..., A_END:K_OFF].reshape(Bn, L, GQA_HEADS, HEAD_DIM), q_g)
    out_bc = gqa_heads(qc, kc, vc)
    yc = jnp.concatenate([out_ac, out_bc], axis=-1) @ w_out
    return y, yc


def mla_query(cq, q_g, w_uq, ang):
    Bn, T, _ = cq.shape
    q = (rms_norm(cq, q_g) @ w_uq).reshape(Bn, T, MLA_HEADS, MLA_QK)
    if ang is None:
        return q
    return jnp.concatenate([q[..., :MLA_NOPE], rope_2d(q[..., MLA_NOPE:], *ang)], axis=-1)


def mla_kv(ck, kv_g, w_ukv, ang):
    Bn, T, _ = ck.shape
    kv = (rms_norm(ck[..., :MLA_KV_LORA], kv_g) @ w_ukv).reshape(Bn, T, MLA_HEADS, MLA_NOPE + MLA_V)
    k_nope, v = kv[..., :MLA_NOPE], kv[..., MLA_NOPE:]
    kr = ck[..., MLA_KV_LORA:][:, :, None, :]
    if ang is not None:
        kr = rope_2d(kr, *ang)
    k = jnp.concatenate([k_nope, jnp.broadcast_to(kr, (Bn, T, MLA_HEADS, MLA_ROPE))], axis=-1)
    return k, v


def mla_heads(q, k, v):
    Bn, Tq = q.shape[:2]
    o = blocked_attention(q.transpose(0, 2, 1, 3)[:, :, None], k.transpose(0, 2, 1, 3), v.transpose(0, 2, 1, 3))
    return o[:, :, 0].transpose(0, 2, 1, 3).reshape(Bn, Tq, MLA_HEADS * MLA_V)


def mla_mixer(h, hc, w_in, q_g, kv_g, w_uq, w_ukv, w_out, ang, ctx_out):
    p = h @ w_in
    q = mla_query(p[..., :MLA_Q_LORA], q_g, w_uq, ang)
    k, v = mla_kv(p[..., MLA_Q_LORA:], kv_g, w_ukv, ang)
    pc = hc @ (w_in if ctx_out else w_in[:, MLA_Q_LORA:])
    pc_kv = pc[..., MLA_Q_LORA:] if ctx_out else pc
    kc, vc = mla_kv(pc_kv, kv_g, w_ukv, None)
    y = mla_heads(q, jnp.concatenate([kc, k], axis=1), jnp.concatenate([vc, v], axis=1)) @ w_out
    if not ctx_out:
        return y, None
    qc = mla_query(pc[..., :MLA_Q_LORA], q_g, w_uq, None)
    yc = mla_heads(qc, kc, vc) @ w_out
    return y, yc


def moe_ffn(h, router_w, router_b, w1, b1, w2, b2):
    T, D = h.shape
    logits = (h @ router_w + router_b).astype(jnp.float32)
    top_logit, top_idx = lax.top_k(logits, TOP_K)
    gate = jax.nn.softmax(top_logit, axis=-1)
    flat_e = top_idx.reshape(-1)
    order = jnp.argsort(flat_e)
    e_sorted = flat_e[order]
    tok_sorted = order // TOP_K
    counts = jnp.bincount(flat_e, length=N_EXPERTS)
    padded = (counts + MOE_BLOCK - 1) // MOE_BLOCK * MOE_BLOCK
    pad_end = jnp.cumsum(padded)
    pad_start = pad_end - padded
    start = jnp.cumsum(counts) - counts
    dest = pad_start[e_sorted] + jnp.arange(T * TOP_K, dtype=jnp.int32) - start[e_sorted]
    n_rows = T * TOP_K + N_EXPERTS * MOE_BLOCK
    n_blocks = n_rows // MOE_BLOCK
    row_tok = jnp.full((n_rows,), T, dtype=jnp.int32).at[dest].set(tok_sorted)
    h_pad = jnp.concatenate([h, jnp.zeros((1, D), h.dtype)], axis=0)
    xs = h_pad[row_tok].reshape(n_blocks, MOE_BLOCK, D)
    blk_start = jnp.arange(n_blocks, dtype=jnp.int32) * MOE_BLOCK
    blk_e = jnp.minimum(jnp.searchsorted(pad_end, blk_start, side='right'), N_EXPERTS - 1)

    def expert_block(args):
        xb, e = args
        hid = xb @ w1[e] + b1[e]
        g, lin = jnp.split(hid, 2, axis=-1)
        g = jnp.minimum(g, SWIGLU_LIMIT)
        lin = jnp.clip(lin, -SWIGLU_LIMIT, SWIGLU_LIMIT)
        act = g * jax.nn.sigmoid(SWIGLU_ALPHA * g) * (lin + 1.0)
        return act @ w2[e] + b2[e]

    ys = lax.map(expert_block, (xs, blk_e)).reshape(n_rows, D)
    y_assign = ys[dest] * gate.reshape(-1)[order][:, None].astype(ys.dtype)
    return jnp.zeros((T, D), h.dtype).at[tok_sorted].add(y_assign)


def setup_inputs(seed: int = 0) -> dict:
    key = jax.random.key(seed)
    ks = jax.random.split(key, 32)
    n_even = (DEPTH + 1) // 2
    n_odd = DEPTH // 2
    f32 = jnp.float32

    def nrm(k, shape, fan_in):
        return jax.random.normal(k, shape, f32) * fan_in ** -0.5

    def gain(k, shape):
        return 1.0 + 0.02 * jax.random.normal(k, shape, f32)

    def small(k, shape):
        return 0.01 * jax.random.normal(k, shape, f32)

    D = D_MODEL
    return {
        "x": jax.random.normal(ks[0], (BATCH, SEQ, D), f32),
        "c": jax.random.normal(ks[1], (BATCH, D), f32),
        "ctx": jax.random.normal(ks[2], (BATCH, CTX_LEN, D), f32),
        "c_ctx": jax.random.normal(ks[3], (D,), f32),
        "ada_w": nrm(ks[4], (DEPTH, D, 6 * D), D),
        "ada_b": small(ks[5], (DEPTH, 6 * D)),
        "norm_mix_g": gain(ks[6], (DEPTH, D)),
        "norm_ffn_g": gain(ks[7], (DEPTH, D)),
        "ab_w_in": nrm(ks[8], (n_even, D, AB_IN_W), D),
        "sgu_norm_g": gain(ks[9], (n_even, SGU_WIDTH)),
        "sgu_w": nrm(ks[10], (n_even, SGU_GROUPS, CHUNK, CHUNK), CHUNK),
        "sgu_b": gain(ks[11], (n_even, SGU_GROUPS, CHUNK)),
        "gqa_q_norm_g": gain(ks[12], (n_even, HEAD_DIM)),
        "gqa_k_norm_g": gain(ks[13], (n_even, HEAD_DIM)),
        "ab_w_out": nrm(ks[14], (n_even, AB_MIX_W, D), AB_MIX_W),
        "mla_w_in": nrm(ks[15], (n_odd, D, MLA_IN_W), D),
        "mla_q_norm_g": gain(ks[16], (n_odd, MLA_Q_LORA)),
        "mla_kv_norm_g": gain(ks[17], (n_odd, MLA_KV_LORA)),
        "mla_w_uq": nrm(ks[18], (n_odd, MLA_Q_LORA, MLA_HEADS * MLA_QK), MLA_Q_LORA),
        "mla_w_ukv": nrm(ks[19], (n_odd, MLA_KV_LORA, MLA_HEADS * (MLA_NOPE + MLA_V)), MLA_KV_LORA),
        "mla_w_out": nrm(ks[20], (n_odd, MLA_HEADS * MLA_V, D), MLA_HEADS * MLA_V),
        "router_w": nrm(ks[21], (DEPTH, D, N_EXPERTS), D),
        "router_b": small(ks[22], (DEPTH, N_EXPERTS)),
        "moe_w1": nrm(ks[23], (DEPTH, N_EXPERTS, D, 2 * D_EXPERT), D),
        "moe_b1": small(ks[24], (DEPTH, N_EXPERTS, 2 * D_EXPERT)),
        "moe_w2": nrm(ks[25], (DEPTH, N_EXPERTS, D_EXPERT, D), D_EXPERT),
        "moe_b2": small(ks[26], (DEPTH, N_EXPERTS, D)),
        "final_norm_g": gain(ks[27], (D,)),
    }


def reference(x, c, ctx, c_ctx, ada_w, ada_b, norm_mix_g, norm_ffn_g, ab_w_in, sgu_norm_g, sgu_w, sgu_b,
              gqa_q_norm_g, gqa_k_norm_g, ab_w_out, mla_w_in, mla_q_norm_g, mla_kv_norm_g, mla_w_uq, mla_w_ukv,
              mla_w_out, router_w, router_b, moe_w1, moe_b1, moe_w2, moe_b2, final_norm_g):
    Bn, S, D = x.shape
    L = ctx.shape[1]
    ROWS = S // GRID_W
    ang_gqa = grid_angles(ROWS, HEAD_DIM)
    ang_mla = grid_angles(ROWS, MLA_ROPE)
    silu_c = jax.nn.silu(c)
    silu_cc = jax.nn.silu(c_ctx)
    xc = ctx
    for i in range(DEPTH):
        ctx_out = i < DEPTH - 1
        j = i // 2
        mod = (silu_c @ ada_w[i] + ada_b[i])[:, None, :]
        modc = silu_cc @ ada_w[i] + ada_b[i]
        sh1, sc1, g1, sh2, sc2, g2 = jnp.split(mod, 6, axis=-1)
        sh1c, sc1c, g1c, sh2c, sc2c, g2c = jnp.split(modc, 6, axis=-1)
        h = rms_norm(x, norm_mix_g[i]) * (1.0 + sc1) + sh1
        hc = rms_norm(xc, norm_mix_g[i]) * (1.0 + sc1c) + sh1c
        if i % 2 == 0:
            y, yc = ab_mixer(h, hc, ab_w_in[j], sgu_norm_g[j], sgu_w[j], sgu_b[j], gqa_q_norm_g[j],
                             gqa_k_norm_g[j], ab_w_out[j], ang_gqa, ctx_out)
        else:
            y, yc = mla_mixer(h, hc, mla_w_in[j], mla_q_norm_g[j], mla_kv_norm_g[j], mla_w_uq[j],
                              mla_w_ukv[j], mla_w_out[j], ang_mla, ctx_out)
        x = x + g1 * y
        h2 = (rms_norm(x, norm_ffn_g[i]) * (1.0 + sc2) + sh2).reshape(Bn * S, D)
        if ctx_out:
            xc = xc + g1c * yc
            h2c = (rms_norm(xc, norm_ffn_g[i]) * (1.0 + sc2c) + sh2c).reshape(Bn * L, D)
            f = moe_ffn(jnp.concatenate([h2, h2c], axis=0), router_w[i], router_b[i],
                        moe_w1[i], moe_b1[i], moe_w2[i], moe_b2[i])
            x = x + g2 * f[:Bn * S].reshape(Bn, S, D)
            xc = xc + g2c * f[Bn * S:].reshape(Bn, L, D)
        else:
            f = moe_ffn(h2, router_w[i], router_b[i], moe_w1[i], moe_b1[i], moe_w2[i], moe_b2[i])
            x = x + g2 * f.reshape(Bn, S, D)
    return rms_norm(x, final_norm_g)
```

```python
import functools

import jax
import jax.numpy as jnp
import numpy as np
from jax import lax
from jax.experimental import pallas as pl
from jax.experimental.pallas import tpu as pltpu

F32 = jnp.float32
BF16 = jnp.bfloat16

GRID_W = 64
EPS = 1e-6
ROPE_THETA = 10000.0
CHUNK = 128
SGU_GROUPS = 4
SGU_GROUP_CH = 128
SGU_WIDTH = SGU_GROUPS * SGU_GROUP_CH
GQA_HEADS = 4
GQA_KV_HEADS = 2
HEAD_DIM = 128
GQA_Q_W = GQA_HEADS * HEAD_DIM
GQA_KV_W = GQA_KV_HEADS * HEAD_DIM
A_END = 2 * SGU_WIDTH
K_OFF = A_END + GQA_Q_W
V_OFF = K_OFF + GQA_KV_W
MLA_HEADS = 8
MLA_Q_LORA = 512
MLA_KV_LORA = 256
MLA_NOPE = 128
MLA_ROPE = 64
MLA_V = 128
MLA_QK = MLA_NOPE + MLA_ROPE
MLA_QK_PAD = 256
MLA_IN_PAD = 896
N_EXPERTS = 32
TOP_K = 4
D_EXPERT = 1024
SWIGLU_LIMIT = 7.0
SWIGLU_ALPHA = 1.702

LANES = 128
VMEM_LIMIT = 56 * 1024 * 1024


def _cparams(*sem):
    return pltpu.CompilerParams(dimension_semantics=sem, vmem_limit_bytes=VMEM_LIMIT)


def _dot(a, b):
    return jnp.dot(a, b, preferred_element_type=F32)


def _dot_nt(a, b):
    return lax.dot_general(a, b, (((1,), (1,)), ((), ())), preferred_element_type=F32)


def _split2(a):
    hi = a.astype(BF16)
    lo = (a - hi.astype(F32)).astype(BF16)
    return hi, lo


def _rms(x, g):
    return x * lax.rsqrt(jnp.mean(x * x, axis=-1, keepdims=True) + EPS) * g


def _rope(x, c, s_a, s_b, half):
    n = x.shape[-1]
    return x * c + pltpu.roll(x, n - half, 1) * s_a + pltpu.roll(x, half, 1) * s_b


def _ada_kernel(c_ref, w_ref, b_ref, o_ref):
    c = c_ref[...]
    sc = c * jax.nn.sigmoid(c)
    c_hi, c_lo = _split2(sc)
    w = w_ref[0]
    w_hi, w_lo = _split2(w)
    acc = _dot(c_hi, w_hi) + _dot(c_hi, w_lo) + _dot(c_lo, w_hi)
    o_ref[0] = acc + b_ref[0]


def _ada(cond, ada_w, ada_b):
    depth, d, n = ada_w.shape
    rows = cond.shape[0]
    tn = 1536
    return pl.pallas_call(
        _ada_kernel,
        out_shape=jax.ShapeDtypeStruct((depth, rows, n), F32),
        grid=(depth, n // tn),
        in_specs=[
            pl.BlockSpec((rows, d), lambda l, j: (0, 0)),
            pl.BlockSpec((1, d, tn), lambda l, j: (l, 0, j)),
            pl.BlockSpec((1, 1, tn), lambda l, j: (l, 0, j)),
        ],
        out_specs=pl.BlockSpec((1, rows, tn), lambda l, j: (l, 0, j)),
        compiler_params=_cparams("arbitrary", "arbitrary"),
        name="ada_mod",
    )(cond, ada_w, ada_b.reshape(depth, 1, n))


def _proj_ab_kernel(*refs, use_rope, tm):
    if use_rope:
        (x_ref, sc_ref, sh_ref, g_ref, w_ref, sgug_ref, sguw_ref, sgub_ref, qg_ref, kg_ref,
         c_ref, sa_ref, sb_ref, oa_ref, q_ref, k_ref, v_ref) = refs
    else:
        (x_ref, sc_ref, sh_ref, g_ref, w_ref, sgug_ref, sguw_ref, sgub_ref, qg_ref, kg_ref,
         oa_ref, q_ref, k_ref, v_ref) = refs
    x = x_ref[...]
    h = _rms(x, g_ref[...]) * (1.0 + sc_ref[0]) + sh_ref[0]
    p = _dot(h.astype(BF16), w_ref[...])

    n_chunks = tm // CHUNK
    for g in range(SGU_GROUPS):
        u = jax.nn.gelu(p[:, g * SGU_GROUP_CH:(g + 1) * SGU_GROUP_CH])
        v = jax.nn.gelu(p[:, SGU_WIDTH + g * SGU_GROUP_CH:SGU_WIDTH + (g + 1) * SGU_GROUP_CH])
        v = _rms(v, sgug_ref[:, g * SGU_GROUP_CH:(g + 1) * SGU_GROUP_CH]).astype(BF16)
        vcat = jnp.concatenate([v[c * CHUNK:(c + 1) * CHUNK, :] for c in range(n_chunks)], axis=1)
        mixed = _dot(sguw_ref[g], vcat)
        bias = sgub_ref[g]
        for c in range(n_chunks):
            gate = mixed[:, c * SGU_GROUP_CH:(c + 1) * SGU_GROUP_CH] + bias
            oa_ref[c * CHUNK:(c + 1) * CHUNK, g * SGU_GROUP_CH:(g + 1) * SGU_GROUP_CH] = (
                u[c * CHUNK:(c + 1) * CHUNK, :] * gate).astype(oa_ref.dtype)

    qg = qg_ref[...] * (HEAD_DIM ** -0.5)
    for hd in range(GQA_HEADS):
        qh = _rms(p[:, A_END + hd * HEAD_DIM:A_END + (hd + 1) * HEAD_DIM], qg)
        if use_rope:
            qh = _rope(qh, c_ref[...], sa_ref[...], sb_ref[...], HEAD_DIM // 4)
        q_ref[:, hd * HEAD_DIM:(hd + 1) * HEAD_DIM] = qh.astype(q_ref.dtype)
    for hd in range(GQA_KV_HEADS):
        kh = _rms(p[:, K_OFF + hd * HEAD_DIM:K_OFF + (hd + 1) * HEAD_DIM], kg_ref[...])
        if use_rope:
            kh = _rope(kh, c_ref[...], sa_ref[...], sb_ref[...], HEAD_DIM // 4)
        k_ref[:, hd * HEAD_DIM:(hd + 1) * HEAD_DIM] = kh.astype(k_ref.dtype)
    v_ref[...] = p[:, V_OFF:].astype(v_ref.dtype)


def _proj_ab(x, mod, mod_rows, group_rows, norm_g, w_in, sgu_g, sgu_w, sgu_bfull, q_g, k_g, rope, tm):
    m, d = x.shape
    per_group = group_rows // tm
    sc_j, sh_j = mod_rows
    use_rope = rope is not None
    in_specs = [
        pl.BlockSpec((tm, d), lambda i: (i, 0)),
        pl.BlockSpec((1, 1, d), lambda i: ((i // per_group) * 6 + sc_j, 0, 0)),
        pl.BlockSpec((1, 1, d), lambda i: ((i // per_group) * 6 + sh_j, 0, 0)),
        pl.BlockSpec((1, d), lambda i: (0, 0)),
        pl.BlockSpec(w_in.shape, lambda i: (0, 0)),
        pl.BlockSpec((1, SGU_WIDTH), lambda i: (0, 0)),
        pl.BlockSpec(sgu_w.shape, lambda i: (0, 0, 0)),
        pl.BlockSpec(sgu_bfull.shape, lambda i: (0, 0, 0)),
        pl.BlockSpec((1, HEAD_DIM), lambda i: (0, 0)),
        pl.BlockSpec((1, HEAD_DIM), lambda i: (0, 0)),
    ]
    args = [x, mod, mod, norm_g, w_in, sgu_g, sgu_w, sgu_bfull, q_g, k_g]
    if use_rope:
        s_tiles = rope[0].shape[0] // tm
        for t in rope:
            in_specs.append(pl.BlockSpec((tm, HEAD_DIM), lambda i: (i % s_tiles, 0)))
            args.append(t)
    return pl.pallas_call(
        functools.partial(_proj_ab_kernel, use_rope=use_rope, tm=tm),
        out_shape=(jax.ShapeDtypeStruct((m, SGU_WIDTH), BF16),
                   jax.ShapeDtypeStruct((m, GQA_Q_W), BF16),
                   jax.ShapeDtypeStruct((m, GQA_KV_W), BF16),
                   jax.ShapeDtypeStruct((m, GQA_KV_W), BF16)),
        grid=(m // tm,),
        in_specs=in_specs,
        out_specs=(pl.BlockSpec((tm, SGU_WIDTH), lambda i: (i, 0)),
                   pl.BlockSpec((tm, GQA_Q_W), lambda i: (i, 0)),
                   pl.BlockSpec((tm, GQA_KV_W), lambda i: (i, 0)),
                   pl.BlockSpec((tm, GQA_KV_W), lambda i: (i, 0))),
        compiler_params=_cparams("parallel"),
        name="proj_ab",
    )(*args)


def _proj_mla_kernel(*refs, with_q, use_rope):
    it = iter(refs)
    x_ref, sc_ref, sh_ref, g_ref, w_ref = (next(it) for _ in range(5))
    if with_q:
        qg_ref, wuq_ref = next(it), next(it)
    kvg_ref, wuk_ref, wuv_ref = next(it), next(it), next(it)
    if use_rope:
        c_ref, sa_ref, sb_ref = next(it), next(it), next(it)
    if with_q:
        q_ref = next(it)
    k_ref, v_ref = next(it), next(it)

    x = x_ref[...]
    h = _rms(x, g_ref[...]) * (1.0 + sc_ref[0]) + sh_ref[0]
    p = _dot(h.astype(BF16), w_ref[...])
    off = MLA_Q_LORA if with_q else 0
    ckv = _rms(p[:, off:off + MLA_KV_LORA], kvg_ref[...]).astype(BF16)
    kr = p[:, off + MLA_KV_LORA:off + MLA_KV_LORA + LANES]
    if use_rope:
        kr = _rope(kr, c_ref[...], sa_ref[...], sb_ref[...], MLA_ROPE // 4)
    kr = kr.astype(k_ref.dtype)
    k_nope = _dot(ckv, wuk_ref[...])
    v_ref[...] = _dot(ckv, wuv_ref[...]).astype(v_ref.dtype)
    for hd in range(MLA_HEADS):
        k_ref[:, hd * MLA_QK_PAD:hd * MLA_QK_PAD + MLA_NOPE] = (
            k_nope[:, hd * MLA_NOPE:(hd + 1) * MLA_NOPE].astype(k_ref.dtype))
        k_ref[:, hd * MLA_QK_PAD + MLA_NOPE:(hd + 1) * MLA_QK_PAD] = kr
    if with_q:
        cq = _rms(p[:, :MLA_Q_LORA], qg_ref[...]).astype(BF16)
        q = _dot(cq, wuq_ref[...]) * (MLA_QK ** -0.5)
        for hd in range(MLA_HEADS):
            q_ref[:, hd * MLA_QK_PAD:hd * MLA_QK_PAD + MLA_NOPE] = (
                q[:, hd * MLA_QK_PAD:hd * MLA_QK_PAD + MLA_NOPE].astype(q_ref.dtype))
            qr = q[:, hd * MLA_QK_PAD + MLA_NOPE:(hd + 1) * MLA_QK_PAD]
            if use_rope:
                qr = _rope(qr, c_ref[...], sa_ref[...], sb_ref[...], MLA_ROPE // 4)
            q_ref[:, hd * MLA_QK_PAD + MLA_NOPE:(hd + 1) * MLA_QK_PAD] = qr.astype(q_ref.dtype)


def _proj_mla(x, mod, mod_rows, group_rows, norm_g, w_in, q_g, w_uq, kv_g, w_uk, w_uv, rope, with_q, tm):
    m, d = x.shape
    per_group = group_rows // tm
    sc_j, sh_j = mod_rows
    use_rope = rope is not None
    const2 = lambda i: (0, 0)
    in_specs = [
        pl.BlockSpec((tm, d), lambda i: (i, 0)),
        pl.BlockSpec((1, 1, d), lambda i: ((i // per_group) * 6 + sc_j, 0, 0)),
        pl.BlockSpec((1, 1, d), lambda i: ((i // per_group) * 6 + sh_j, 0, 0)),
        pl.BlockSpec((1, d), const2),
        pl.BlockSpec(w_in.shape, const2),
    ]
    args = [x, mod, mod, norm_g, w_in]
    if with_q:
        in_specs += [pl.BlockSpec(q_g.shape, const2), pl.BlockSpec(w_uq.shape, const2)]
        args += [q_g, w_uq]
    in_specs += [pl.BlockSpec(kv_g.shape, const2), pl.BlockSpec(w_uk.shape, const2),
                 pl.BlockSpec(w_uv.shape, const2)]
    args += [kv_g, w_uk, w_uv]
    if use_rope:
        s_tiles = rope[0].shape[0] // tm
        for t in rope:
            in_specs.append(pl.BlockSpec((tm, LANES), lambda i: (i % s_tiles, 0)))
            args.append(t)
    kw = MLA_HEADS * MLA_QK_PAD
    vw = MLA_HEADS * MLA_V
    out_shape = [jax.ShapeDtypeStruct((m, kw), BF16), jax.ShapeDtypeStruct((m, vw), BF16)]
    out_specs = [pl.BlockSpec((tm, kw), lambda i: (i, 0)), pl.BlockSpec((tm, vw), lambda i: (i, 0))]
    if with_q:
        out_shape = [jax.ShapeDtypeStruct((m, kw), BF16)] + out_shape
        out_specs = [pl.BlockSpec((tm, kw), lambda i: (i, 0))] + out_specs
    return pl.pallas_call(
        functools.partial(_proj_mla_kernel, with_q=with_q, use_rope=use_rope),
        out_shape=tuple(out_shape),
        grid=(m // tm,),
        in_specs=in_specs,
        out_specs=tuple(out_specs),
        compiler_params=_cparams("parallel"),
        name="proj_mla",
    )(*args)


def _attn_kernel(*refs, n_seg, tq, n_q):
    q_ref = refs[0]
    kv = refs[1:1 + 2 * n_seg]
    o_ref = refs[1 + 2 * n_seg]

    def body(c, carry):
        r0 = pl.multiple_of(c * tq, tq)
        q = q_ref[pl.ds(r0, tq), :]
        ss = [_dot_nt(q, kv[2 * j][...]) for j in range(n_seg)]
        mx = ss[0].max(axis=-1, keepdims=True)
        for s in ss[1:]:
            mx = jnp.maximum(mx, s.max(axis=-1, keepdims=True))
        acc = None
        den = None
        for j, s in enumerate(ss):
            e = jnp.exp(s - mx)
            d = e.sum(axis=-1, keepdims=True)
            o = _dot(e.astype(BF16), kv[2 * j + 1][...])
            acc = o if acc is None else acc + o
            den = d if den is None else den + d
        o_ref[pl.ds(r0, tq), :] = (acc / den).astype(o_ref.dtype)
        return carry

    lax.fori_loop(0, n_q, body, 0)


def _attention(q, segs, batch, heads, kv_heads, dk, dv, tq):
    t_q = q.shape[0] // batch
    group = heads // kv_heads
    in_specs = [pl.BlockSpec((t_q, dk), lambda b, h: (b, h))]
    args = [q]
    for k, v in segs:
        t_k = k.shape[0] // batch
        in_specs.append(pl.BlockSpec((t_k, dk), lambda b, h: (b, h // group)))
        in_specs.append(pl.BlockSpec((t_k, dv), lambda b, h: (b, h // group)))
        args += [k, v]
    tq = min(tq, t_q)
    return pl.pallas_call(
        functools.partial(_attn_kernel, n_seg=len(segs), tq=tq, n_q=t_q // tq),
        out_shape=jax.ShapeDtypeStruct((q.shape[0], heads * dv), BF16),
        grid=(batch, heads),
        in_specs=in_specs,
        out_specs=pl.BlockSpec((t_q, dv), lambda b, h: (b, h)),
        compiler_params=_cparams("parallel", "parallel"),
        name="attention",
    )(*args)


def _out_router_kernel(*refs, n_in, tm):
    ins = refs[:2 * n_in]
    (x_ref, g1_ref, sc_ref, sh_ref, ng_ref, rw_ref, rb_ref, cnt0_ref,
     xo_ref, h2_ref, idx_ref, gate_ref, rank_ref, cnt_ref) = refs[2 * n_in:]
    i = pl.program_id(0)

    @pl.when(i == 0)
    def _():
        cnt_ref[...] = cnt0_ref[...]

    y = None
    for j in range(n_in):
        t = _dot(ins[2 * j][...], ins[2 * j + 1][...])
        y = t if y is None else y + t
    xn = x_ref[...] + g1_ref[0] * y
    xo_ref[...] = xn
    h2 = _rms(xn, ng_ref[...]) * (1.0 + sc_ref[0]) + sh_ref[0]
    h2_ref[...] = h2

    h_hi, h_lo = _split2(h2)
    w_hi, w_lo = _split2(rw_ref[...])
    logits = _dot_nt(w_hi, h_hi) + _dot_nt(w_hi, h_lo) + _dot_nt(w_lo, h_hi) + rb_ref[...]

    e_iota = lax.broadcasted_iota(jnp.int32, logits.shape, 0)
    work = logits
    tops, idxs = [], []
    for _ in range(TOP_K):
        mx = work.max(axis=0, keepdims=True)
        ix = jnp.where(work == mx, e_iota, N_EXPERTS).min(axis=0, keepdims=True)
        tops.append(mx)
        idxs.append(ix)
        work = jnp.where(e_iota == ix, -jnp.inf, work)
    exps = [jnp.exp(t - tops[0]) for t in tops]
    den = exps[0] + exps[1] + exps[2] + exps[3]

    onehots = [(e_iota == ix).astype(F32) for ix in idxs]
    cnt = onehots[0] + onehots[1] + onehots[2] + onehots[3]
    r_io = lax.broadcasted_iota(jnp.int32, (tm, tm), 0)
    c_io = lax.broadcasted_iota(jnp.int32, (tm, tm), 1)
    upper = (r_io < c_io).astype(BF16)
    before = _dot(cnt.astype(BF16), upper) + cnt_ref[:, :1]
    for k in range(TOP_K):
        idx_ref[k:k + 1, :] = idxs[k]
        gate_ref[k:k + 1, :] = exps[k] / den
        rank_ref[k:k + 1, :] = (onehots[k] * before).sum(axis=0, keepdims=True).astype(jnp.int32)
    cnt_ref[...] = cnt_ref[...] + cnt.sum(axis=1, keepdims=True)


def _out_router(ins, x, mod, mod_rows, group_rows, norm_g, router_wt, router_b, cnt0, tm):
    m, d = x.shape
    per_group = group_rows // tm
    g1_j, sc_j, sh_j = mod_rows
    in_specs, args = [], []
    for a, w in ins:
        in_specs += [pl.BlockSpec((tm, a.shape[1]), lambda i: (i, 0)), pl.BlockSpec(w.shape, lambda i: (0, 0))]
        args += [a, w]
    mod_spec = lambda j: pl.BlockSpec((1, 1, d), lambda i: ((i // per_group) * 6 + j, 0, 0))
    in_specs += [
        pl.BlockSpec((tm, d), lambda i: (i, 0)),
        mod_spec(g1_j), mod_spec(sc_j), mod_spec(sh_j),
        pl.BlockSpec((1, d), lambda i: (0, 0)),
        pl.BlockSpec(router_wt.shape, lambda i: (0, 0)),
        pl.BlockSpec(router_b.shape, lambda i: (0, 0)),
        pl.BlockSpec(cnt0.shape, lambda i: (0, 0)),
    ]
    args += [x, mod, mod, mod, norm_g, router_wt, router_b, cnt0]
    small = lambda dt: jax.ShapeDtypeStruct((TOP_K, m), dt)
    small_spec = pl.BlockSpec((TOP_K, tm), lambda i: (0, i))
    return pl.pallas_call(
        functools.partial(_out_router_kernel, n_in=len(ins), tm=tm),
        out_shape=(jax.ShapeDtypeStruct((m, d), F32), jax.ShapeDtypeStruct((m, d), F32),
                   small(jnp.int32), small(F32), small(jnp.int32),
                   jax.ShapeDtypeStruct(cnt0.shape, F32)),
        grid=(m // tm,),
        in_specs=in_specs,
        out_specs=(pl.BlockSpec((tm, d), lambda i: (i, 0)), pl.BlockSpec((tm, d), lambda i: (i, 0)),
                   small_spec, small_spec, small_spec,
                   pl.BlockSpec(cnt0.shape, lambda i: (0, 0))),
        compiler_params=_cparams("arbitrary"),
        name="out_router",
    )(*args)


def _moe_kernel(blk_e_ref, nused_ref, tok_ref, h_hbm, w1_ref, b1_ref, w2_ref, b2_ref, o_ref, xbuf, sem, *, tm):
    i = pl.program_id(0)

    @pl.when(i < nused_ref[0])
    def _():
        def issue(r, carry):
            t = tok_ref[0, 0, r]
            pltpu.make_async_copy(h_hbm.at[pl.ds(t, 1)], xbuf.at[pl.ds(r, 1)], sem).start()
            return carry

        lax.fori_loop(0, tm, issue, 0)
        pltpu.make_async_copy(h_hbm.at[pl.ds(0, tm)], xbuf, sem).wait()
        x = xbuf[...].astype(BF16)
        hid = _dot(x, w1_ref[0]) + b1_ref[0]
        f = hid.shape[1] // 2
        g = jnp.minimum(hid[:, :f], SWIGLU_LIMIT)
        lin = jnp.clip(hid[:, f:], -SWIGLU_LIMIT, SWIGLU_LIMIT)
        act = g * jax.nn.sigmoid(SWIGLU_ALPHA * g) * (lin + 1.0)
        o_ref[...] = _dot(act.astype(BF16), w2_ref[0]) + b2_ref[0]

    @pl.when(i >= nused_ref[0])
    def _():
        o_ref[...] = jnp.zeros_like(o_ref)


def _moe(h2, row_tok, blk_e, n_used, w1, b1, w2, b2, tm):
    n_tiles = row_tok.shape[0]
    d = h2.shape[1]
    f2 = w1.shape[2]
    grid_spec = pltpu.PrefetchScalarGridSpec(
        num_scalar_prefetch=2,
        grid=(n_tiles,),
        in_specs=[
            pl.BlockSpec((1, 1, tm), lambda i, be, nu: (i, 0, 0), memory_space=pltpu.SMEM),
            pl.BlockSpec(memory_space=pl.ANY),
            pl.BlockSpec((1, d, f2), lambda i, be, nu: (be[i], 0, 0)),
            pl.BlockSpec((1, 1, f2), lambda i, be, nu: (be[i], 0, 0)),
            pl.BlockSpec((1, f2 // 2, d), lambda i, be, nu: (be[i], 0, 0)),
            pl.BlockSpec((1, 1, d), lambda i, be, nu: (be[i], 0, 0)),
        ],
        out_specs=pl.BlockSpec((tm, d), lambda i, be, nu: (i, 0)),
        scratch_shapes=[pltpu.VMEM((tm, d), F32), pltpu.SemaphoreType.DMA(())],
    )
    return pl.pallas_call(
        functools.partial(_moe_kernel, tm=tm),
        out_shape=jax.ShapeDtypeStruct((n_tiles * tm, d), F32),
        grid_spec=grid_spec,
        compiler_params=_cparams("arbitrary"),
        name="moe_experts",
    )(blk_e, n_used, row_tok, h2, w1, b1, w2, b2)


def _combine_kernel(pos_ref, ys_hbm, gate_ref, x_ref, g2_ref, *rest, tq, final):
    if final:
        fg_ref, o_ref, buf, sem = rest
    else:
        o_ref, buf, sem = rest

    def issue(t, carry):
        for k in range(TOP_K):
            r = pos_ref[0, 0, k * tq + t]
            pltpu.make_async_copy(ys_hbm.at[pl.ds(r, 1)], buf.at[k, pl.ds(t, 1)], sem).start()
        return carry

    lax.fori_loop(0, tq, issue, 0)
    for k in range(TOP_K):
        pltpu.make_async_copy(ys_hbm.at[pl.ds(0, tq)], buf.at[k], sem).wait()
    f = None
    d = x_ref.shape[1]
    for k in range(TOP_K):
        gcol = jnp.broadcast_to(gate_ref[k:k + 1, :], (LANES, tq)).T
        t = buf[k] * jnp.tile(gcol, (1, d // LANES))
        f = t if f is None else f + t
    xn = x_ref[...] + g2_ref[0] * f
    if final:
        xn = _rms(xn, fg_ref[...])
    o_ref[...] = xn


def _combine(ys, pos_tiles, gate, x, mod, g2_j, group_rows, final_g, tq):
    m, d = x.shape
    per_group = group_rows // tq
    final = final_g is not None
    in_specs = [
        pl.BlockSpec((1, 1, TOP_K * tq), lambda i: (i, 0, 0), memory_space=pltpu.SMEM),
        pl.BlockSpec(memory_space=pl.ANY),
        pl.BlockSpec((TOP_K, tq), lambda i: (0, i)),
        pl.BlockSpec((tq, d), lambda i: (i, 0)),
        pl.BlockSpec((1, 1, d), lambda i: ((i // per_group) * 6 + g2_j, 0, 0)),
    ]
    args = [pos_tiles, ys, gate, x, mod]
    if final:
        in_specs.append(pl.BlockSpec((1, d), lambda i: (0, 0)))
        args.append(final_g)
    return pl.pallas_call(
        functools.partial(_combine_kernel, tq=tq, final=final),
        out_shape=jax.ShapeDtypeStruct((m, d), F32),
        grid=(m // tq,),
        in_specs=in_specs,
        out_specs=pl.BlockSpec((tq, d), lambda i: (i, 0)),
        scratch_shapes=[pltpu.VMEM((TOP_K, tq, d), F32), pltpu.SemaphoreType.DMA(())],
        compiler_params=_cparams("arbitrary"),
        name="moe_combine",
    )(*args)


def _rope_tables(seq, rot_dim):
    t = np.arange(seq)
    row = (t // GRID_W).astype(np.float32)
    col = (t % GRID_W).astype(np.float32)
    axis_dim = rot_dim // 2
    quarter = axis_dim // 2
    inv = jnp.asarray(ROPE_THETA, F32) ** (-jnp.arange(0, axis_dim, 2, dtype=F32) / axis_dim)
    ang_r = jnp.asarray(row)[:, None] * inv
    ang_c = jnp.asarray(col)[:, None] * inv
    zeros = jnp.zeros_like(ang_r)
    cos = jnp.concatenate([jnp.cos(ang_r)] * 2 + [jnp.cos(ang_c)] * 2, axis=1)
    s_a = jnp.concatenate([-jnp.sin(ang_r), zeros, -jnp.sin(ang_c), zeros], axis=1)
    s_b = jnp.concatenate([zeros, jnp.sin(ang_r), zeros, jnp.sin(ang_c)], axis=1)
    pad = LANES - rot_dim
    if pad:
        cos, s_a, s_b = (jnp.pad(a, ((0, 0), (0, pad))) for a in (cos, s_a, s_b))
    del quarter
    return cos, s_a, s_b


def _route(idx, rank, counts, tm, n_tiles):
    t_tot = idx.shape[1]
    counts = counts.astype(jnp.int32)
    padded = (counts + tm - 1) // tm * tm
    pad_end = jnp.cumsum(padded)
    pad_start = pad_end - padded
    start = jnp.cumsum(counts) - counts
    pos = pad_start[idx] + rank
    flat_e = idx.T.reshape(-1)
    order = jnp.argsort(flat_e)
    rows = jnp.arange(n_tiles * tm, dtype=jnp.int32)
    tile_start = jnp.arange(n_tiles, dtype=jnp.int32) * tm
    blk_e = jnp.minimum(jnp.searchsorted(pad_end, tile_start, side='right'), N_EXPERTS - 1).astype(jnp.int32)
    row_e = jnp.repeat(blk_e, tm)
    j = rows - pad_start[row_e]
    valid = j < counts[row_e]
    src = jnp.clip(start[row_e] + j, 0, t_tot * TOP_K - 1)
    row_tok = jnp.where(valid, order[src] // TOP_K, 0).astype(jnp.int32)
    n_used = (pad_end[-1] // tm).astype(jnp.int32).reshape(1)
    return pos, row_tok.reshape(n_tiles, 1, tm), blk_e, n_used


def _pos_tiles(pos, tq):
    k, m = pos.shape
    return pos.reshape(k, m // tq, tq).transpose(1, 0, 2).reshape(m // tq, 1, k * tq)


def _pick(m, pref):
    t = pref
    while m % t:
        t //= 2
    return t


def kernel(x, c, ctx, c_ctx, ada_w, ada_b, norm_mix_g, norm_ffn_g, ab_w_in, sgu_norm_g, sgu_w, sgu_b,
           gqa_q_norm_g, gqa_k_norm_g, ab_w_out, mla_w_in, mla_q_norm_g, mla_kv_norm_g, mla_w_uq, mla_w_ukv,
           mla_w_out, router_w, router_b, moe_w1, moe_b1, moe_w2, moe_b2, final_norm_g):
    bn, s, d = x.shape
    l = ctx.shape[1]
    depth = ada_w.shape[0]
    m_lat, m_ctx = bn * s, bn * l
    tm_lat = _pick(s, 512)
    tm_ctx = _pick(l, 512)
    tm_moe = 512
    tq_comb = _pick(l, 256)

    n_cond = (bn + 1 + 7) // 8 * 8
    cond = jnp.concatenate([c, c_ctx[None, :], jnp.zeros((n_cond - bn - 1, d), F32)], axis=0)
    mod_all = _ada(cond, ada_w, ada_b)

    rope_gqa = _rope_tables(s, HEAD_DIM)
    rope_mla = _rope_tables(s, MLA_ROPE)

    xl = x.reshape(m_lat, d)
    xc = ctx.reshape(m_ctx, d)
    for i in range(depth):
        ctx_out = i < depth - 1
        j = i // 2
        mod_lat = mod_all[i, :bn].reshape(bn * 6, 1, d)
        mod_ctx = mod_all[i, bn].reshape(6, 1, d)
        ng = norm_mix_g[i].reshape(1, d)
        if i % 2 == 0:
            w_in = ab_w_in[j].astype(BF16)
            w_out = ab_w_out[j].astype(BF16)
            sgu_g = sgu_norm_g[j].reshape(1, SGU_WIDTH)
            sgu_wb = sgu_w[j].astype(BF16)
            sgu_bfull = jnp.broadcast_to(sgu_b[j][:, :, None], (SGU_GROUPS, CHUNK, SGU_GROUP_CH))
            q_g = gqa_q_norm_g[j].reshape(1, HEAD_DIM)
            k_g = gqa_k_norm_g[j].reshape(1, HEAD_DIM)
            oa, q, k, v = _proj_ab(xl, mod_lat, (1, 0), s, ng, w_in, sgu_g, sgu_wb, sgu_bfull, q_g, k_g,
                                   rope_gqa, tm_lat)
            oac, qc, kc, vc = _proj_ab(xc, mod_ctx, (1, 0), m_ctx, ng, w_in, sgu_g, sgu_wb, sgu_bfull, q_g, k_g,
                                       None, tm_ctx)
            ob = _attention(q, [(kc, vc), (k, v)], bn, GQA_HEADS, GQA_KV_HEADS, HEAD_DIM, HEAD_DIM, 512)
            ins = [(oa, w_out[:SGU_WIDTH]), (ob, w_out[SGU_WIDTH:])]
            if ctx_out:
                obc = _attention(qc, [(kc, vc)], bn, GQA_HEADS, GQA_KV_HEADS, HEAD_DIM, HEAD_DIM, 512)
                ins_c = [(oac, w_out[:SGU_WIDTH]), (obc, w_out[SGU_WIDTH:])]
        else:
            w_in_f = mla_w_in[j]
            w_in = jnp.pad(w_in_f, ((0, 0), (0, MLA_IN_PAD - w_in_f.shape[1]))).astype(BF16)
            w_in_kv = w_in[:, MLA_Q_LORA:]
            w_out = mla_w_out[j].astype(BF16)
            q_g = mla_q_norm_g[j].reshape(1, MLA_Q_LORA)
            kv_g = mla_kv_norm_g[j].reshape(1, MLA_KV_LORA)
            w_uq = mla_w_uq[j].reshape(MLA_Q_LORA, MLA_HEADS, MLA_QK)
            w_uq = jnp.pad(w_uq, ((0, 0), (0, 0), (0, MLA_QK_PAD - MLA_QK)))
            w_uq = w_uq.reshape(MLA_Q_LORA, MLA_HEADS * MLA_QK_PAD).astype(BF16)
            w_ukv = mla_w_ukv[j].reshape(MLA_KV_LORA, MLA_HEADS, MLA_NOPE + MLA_V)
            w_uk = w_ukv[:, :, :MLA_NOPE].reshape(MLA_KV_LORA, MLA_HEADS * MLA_NOPE).astype(BF16)
            w_uv = w_ukv[:, :, MLA_NOPE:].reshape(MLA_KV_LORA, MLA_HEADS * MLA_V).astype(BF16)
            q, k, v = _proj_mla(xl, mod_lat, (1, 0), s, ng, w_in, q_g, w_uq, kv_g, w_uk, w_uv, rope_mla, True,
                                tm_lat)
            if ctx_out:
                qc, kc, vc = _proj_mla(xc, mod_ctx, (1, 0), m_ctx, ng, w_in, q_g, w_uq, kv_g, w_uk, w_uv, None,
                                       True, tm_ctx)
            else:
                kc, vc = _proj_mla(xc, mod_ctx, (1, 0), m_ctx, ng, w_in_kv, None, None, kv_g, w_uk, w_uv, None,
                                   False, tm_ctx)
            ob = _attention(q, [(kc, vc), (k, v)], bn, MLA_HEADS, MLA_HEADS, MLA_QK_PAD, MLA_V, 512)
            ins = [(ob, w_out)]
            if ctx_out:
                obc = _attention(qc, [(kc, vc)], bn, MLA_HEADS, MLA_HEADS, MLA_QK_PAD, MLA_V, 512)
                ins_c = [(obc, w_out)]

        nfg = norm_ffn_g[i].reshape(1, d)
        rwt = router_w[i].T
        rb = router_b[i].reshape(N_EXPERTS, 1)
        cnt0 = jnp.zeros((N_EXPERTS, LANES), F32)
        xl, h2, idx, gate, rank, cnt = _out_router(ins, xl, mod_lat, (2, 4, 3), s, nfg, rwt, rb, cnt0, tm_lat)
        if ctx_out:
            xc, h2c, idx_c, gate_c, rank_c, cnt = _out_router(ins_c, xc, mod_ctx, (2, 4, 3), m_ctx, nfg, rwt, rb,
                                                                 cnt, tm_ctx)
            h2 = jnp.concatenate([h2, h2c], axis=0)
            idx = jnp.concatenate([idx, idx_c], axis=1)
            rank = jnp.concatenate([rank, rank_c], axis=1)
        t_tot = h2.shape[0]
        n_tiles = (t_tot * TOP_K) // tm_moe + N_EXPERTS
        pos, row_tok, blk_e, n_used = _route(idx, rank, cnt[:, 0], tm_moe, n_tiles)
        ys = _moe(h2, row_tok, blk_e, n_used, moe_w1[i].astype(BF16), moe_b1[i].reshape(N_EXPERTS, 1, -1),
                  moe_w2[i].astype(BF16), moe_b2[i].reshape(N_EXPERTS, 1, -1), tm_moe)
        last = i == depth - 1
        fg = final_norm_g.reshape(1, d) if last else None
        xl = _combine(ys, _pos_tiles(pos[:, :m_lat], tq_comb), gate, xl, mod_lat, 5, s, fg, tq_comb)
        if ctx_out:
            xc = _combine(ys, _pos_tiles(pos[:, m_lat:], tq_comb), gate_c, xc, mod_ctx, 5, m_ctx, None, tq_comb)
    return xl.reshape(bn, s, d)
```

```python
import functools

import jax
import jax.numpy as jnp
import numpy as np
from jax import lax
from jax.experimental import pallas as pl
from jax.experimental.pallas import tpu as pltpu

F32 = jnp.float32
BF16 = jnp.bfloat16

GRID_W = 64
EPS = 1e-6
ROPE_THETA = 10000.0
CHUNK = 128
SGU_GROUPS = 4
SGU_GROUP_CH = 128
SGU_WIDTH = SGU_GROUPS * SGU_GROUP_CH
GQA_HEADS = 4
GQA_KV_HEADS = 2
HEAD_DIM = 128
GQA_Q_W = GQA_HEADS * HEAD_DIM
GQA_KV_W = GQA_KV_HEADS * HEAD_DIM
A_END = 2 * SGU_WIDTH
K_OFF = A_END + GQA_Q_W
V_OFF = K_OFF + GQA_KV_W
MLA_HEADS = 8
MLA_Q_LORA = 512
MLA_KV_LORA = 256
MLA_NOPE = 128
MLA_ROPE = 64
MLA_V = 128
MLA_QK = MLA_NOPE + MLA_ROPE
MLA_QK_PAD = 256
MLA_IN_PAD = 896
N_EXPERTS = 32
TOP_K = 4
D_EXPERT = 1024
SWIGLU_LIMIT = 7.0
SWIGLU_ALPHA = 1.702

LANES = 128
VMEM_LIMIT = 56 * 1024 * 1024


def _cparams(*sem):
    return pltpu.CompilerParams(dimension_semantics=sem, vmem_limit_bytes=VMEM_LIMIT)


def _dot(a, b):
    return jnp.dot(a, b, preferred_element_type=F32)


def _dot_nt(a, b):
    return lax.dot_general(a, b, (((1,), (1,)), ((), ())), preferred_element_type=F32)


def _split2(a):
    hi = a.astype(BF16)
    lo = (a - hi.astype(F32)).astype(BF16)
    return hi, lo


def _rms(x, g):
    return x * lax.rsqrt(jnp.mean(x * x, axis=-1, keepdims=True) + EPS) * g


def _rope(x, c, s_a, s_b, half):
    n = x.shape[-1]
    return x * c + pltpu.roll(x, n - half, 1) * s_a + pltpu.roll(x, half, 1) * s_b


def _ada_kernel(c_ref, w_ref, b_ref, o_ref):
    c = c_ref[...]
    sc = c * jax.nn.sigmoid(c)
    c_hi, c_lo = _split2(sc)
    w = w_ref[0]
    w_hi, w_lo = _split2(w)
    acc = _dot(c_hi, w_hi) + _dot(c_hi, w_lo) + _dot(c_lo, w_hi)
    o_ref[0] = acc + b_ref[0]


def _ada(cond, ada_w, ada_b):
    depth, d, n = ada_w.shape
    rows = cond.shape[0]
    tn = 1536
    return pl.pallas_call(
        _ada_kernel,
        out_shape=jax.ShapeDtypeStruct((depth, rows, n), F32),
        grid=(depth, n // tn),
        in_specs=[
            pl.BlockSpec((rows, d), lambda l, j: (0, 0)),
            pl.BlockSpec((1, d, tn), lambda l, j: (l, 0, j)),
            pl.BlockSpec((1, 1, tn), lambda l, j: (l, 0, j)),
        ],
        out_specs=pl.BlockSpec((1, rows, tn), lambda l, j: (l, 0, j)),
        compiler_params=_cparams("arbitrary", "arbitrary"),
        name="ada_mod",
    )(cond, ada_w, ada_b.reshape(depth, 1, n))


def _proj_ab_kernel(*refs, use_rope, tm):
    if use_rope:
        (x_ref, sc_ref, sh_ref, g_ref, w_ref, sgug_ref, sguw_ref, sgub_ref, qg_ref, kg_ref,
         c_ref, sa_ref, sb_ref, oa_ref, q_ref, k_ref, v_ref) = refs
    else:
        (x_ref, sc_ref, sh_ref, g_ref, w_ref, sgug_ref, sguw_ref, sgub_ref, qg_ref, kg_ref,
         oa_ref, q_ref, k_ref, v_ref) = refs
    x = x_ref[...]
    h = _rms(x, g_ref[...]) * (1.0 + sc_ref[0]) + sh_ref[0]
    p = _dot(h.astype(BF16), w_ref[...])

    n_chunks = tm // CHUNK
    for g in range(SGU_GROUPS):
        u = jax.nn.gelu(p[:, g * SGU_GROUP_CH:(g + 1) * SGU_GROUP_CH])
        v = jax.nn.gelu(p[:, SGU_WIDTH + g * SGU_GROUP_CH:SGU_WIDTH + (g + 1) * SGU_GROUP_CH])
        v = _rms(v, sgug_ref[:, g * SGU_GROUP_CH:(g + 1) * SGU_GROUP_CH]).astype(BF16)
        vcat = jnp.concatenate([v[c * CHUNK:(c + 1) * CHUNK, :] for c in range(n_chunks)], axis=1)
        mixed = _dot(sguw_ref[g], vcat)
        bias = sgub_ref[g]
        for c in range(n_chunks):
            gate = mixed[:, c * SGU_GROUP_CH:(c + 1) * SGU_GROUP_CH] + bias
            oa_ref[c * CHUNK:(c + 1) * CHUNK, g * SGU_GROUP_CH:(g + 1) * SGU_GROUP_CH] = (
                u[c * CHUNK:(c + 1) * CHUNK, :] * gate).astype(oa_ref.dtype)

    qg = qg_ref[...] * (HEAD_DIM ** -0.5)
    for hd in range(GQA_HEADS):
        qh = _rms(p[:, A_END + hd * HEAD_DIM:A_END + (hd + 1) * HEAD_DIM], qg)
        if use_rope:
            qh = _rope(qh, c_ref[...], sa_ref[...], sb_ref[...], HEAD_DIM // 4)
        q_ref[:, hd * HEAD_DIM:(hd + 1) * HEAD_DIM] = qh.astype(q_ref.dtype)
    for hd in range(GQA_KV_HEADS):
        kh = _rms(p[:, K_OFF + hd * HEAD_DIM:K_OFF + (hd + 1) * HEAD_DIM], kg_ref[...])
        if use_rope:
            kh = _rope(kh, c_ref[...], sa_ref[...], sb_ref[...], HEAD_DIM // 4)
        k_ref[:, hd * HEAD_DIM:(hd + 1) * HEAD_DIM] = kh.astype(k_ref.dtype)
    v_ref[...] = p[:, V_OFF:].astype(v_ref.dtype)


def _proj_ab(x, mod, mod_rows, group_rows, norm_g, w_in, sgu_g, sgu_w, sgu_bfull, q_g, k_g, rope, tm):
    m, d = x.shape
    per_group = group_rows // tm
    sc_j, sh_j = mod_rows
    use_rope = rope is not None
    in_specs = [
        pl.BlockSpec((tm, d), lambda i: (i, 0)),
        pl.BlockSpec((1, 1, d), lambda i: ((i // per_group) * 6 + sc_j, 0, 0)),
        pl.BlockSpec((1, 1, d), lambda i: ((i // per_group) * 6 + sh_j, 0, 0)),
        pl.BlockSpec((1, d), lambda i: (0, 0)),
        pl.BlockSpec(w_in.shape, lambda i: (0, 0)),
        pl.BlockSpec((1, SGU_WIDTH), lambda i: (0, 0)),
        pl.BlockSpec(sgu_w.shape, lambda i: (0, 0, 0)),
        pl.BlockSpec(sgu_bfull.shape, lambda i: (0, 0, 0)),
        pl.BlockSpec((1, HEAD_DIM), lambda i: (0, 0)),
        pl.BlockSpec((1, HEAD_DIM), lambda i: (0, 0)),
    ]
    args = [x, mod, mod, norm_g, w_in, sgu_g, sgu_w, sgu_bfull, q_g, k_g]
    if use_rope:
        s_tiles = rope[0].shape[0] // tm
        for t in rope:
            in_specs.append(pl.BlockSpec((tm, HEAD_DIM), lambda i: (i % s_tiles, 0)))
            args.append(t)
    return pl.pallas_call(
        functools.partial(_proj_ab_kernel, use_rope=use_rope, tm=tm),
        out_shape=(jax.ShapeDtypeStruct((m, SGU_WIDTH), BF16),
                   jax.ShapeDtypeStruct((m, GQA_Q_W), BF16),
                   jax.ShapeDtypeStruct((m, GQA_KV_W), BF16),
                   jax.ShapeDtypeStruct((m, GQA_KV_W), BF16)),
        grid=(m // tm,),
        in_specs=in_specs,
        out_specs=(pl.BlockSpec((tm, SGU_WIDTH), lambda i: (i, 0)),
                   pl.BlockSpec((tm, GQA_Q_W), lambda i: (i, 0)),
                   pl.BlockSpec((tm, GQA_KV_W), lambda i: (i, 0)),
                   pl.BlockSpec((tm, GQA_KV_W), lambda i: (i, 0))),
        compiler_params=_cparams("parallel"),
        name="proj_ab",
    )(*args)


def _proj_mla_kernel(*refs, with_q, use_rope):
    it = iter(refs)
    x_ref, sc_ref, sh_ref, g_ref, w_ref = (next(it) for _ in range(5))
    if with_q:
        qg_ref, wuq_ref = next(it), next(it)
    kvg_ref, wuk_ref, wuv_ref = next(it), next(it), next(it)
    if use_rope:
        c_ref, sa_ref, sb_ref = next(it), next(it), next(it)
    if with_q:
        q_ref = next(it)
    k_ref, v_ref = next(it), next(it)

    x = x_ref[...]
    h = _rms(x, g_ref[...]) * (1.0 + sc_ref[0]) + sh_ref[0]
    p = _dot(h.astype(BF16), w_ref[...])
    off = MLA_Q_LORA if with_q else 0
    ckv = _rms(p[:, off:off + MLA_KV_LORA], kvg_ref[...]).astype(BF16)
    kr = p[:, off + MLA_KV_LORA:off + MLA_KV_LORA + LANES]
    if use_rope:
        kr = _rope(kr, c_ref[...], sa_ref[...], sb_ref[...], MLA_ROPE // 4)
    kr = kr.astype(k_ref.dtype)
    k_nope = _dot(ckv, wuk_ref[...])
    v_ref[...] = _dot(ckv, wuv_ref[...]).astype(v_ref.dtype)
    for hd in range(MLA_HEADS):
        k_ref[:, hd * MLA_QK_PAD:hd * MLA_QK_PAD + MLA_NOPE] = (
            k_nope[:, hd * MLA_NOPE:(hd + 1) * MLA_NOPE].astype(k_ref.dtype))
        k_ref[:, hd * MLA_QK_PAD + MLA_NOPE:(hd + 1) * MLA_QK_PAD] = kr
    if with_q:
        cq = _rms(p[:, :MLA_Q_LORA], qg_ref[...]).astype(BF16)
        q = _dot(cq, wuq_ref[...]) * (MLA_QK ** -0.5)
        for hd in range(MLA_HEADS):
            q_ref[:, hd * MLA_QK_PAD:hd * MLA_QK_PAD + MLA_NOPE] = (
                q[:, hd * MLA_QK_PAD:hd * MLA_QK_PAD + MLA_NOPE].astype(q_ref.dtype))
            qr = q[:, hd * MLA_QK_PAD + MLA_NOPE:(hd + 1) * MLA_QK_PAD]
            if use_rope:
                qr = _rope(qr, c_ref[...], sa_ref[...], sb_ref[...], MLA_ROPE // 4)
            q_ref[:, hd * MLA_QK_PAD + MLA_NOPE:(hd + 1) * MLA_QK_PAD] = qr.astype(q_ref.dtype)


def _proj_mla(x, mod, mod_rows, group_rows, norm_g, w_in, q_g, w_uq, kv_g, w_uk, w_uv, rope, with_q, tm):
    m, d = x.shape
    per_group = group_rows // tm
    sc_j, sh_j = mod_rows
    use_rope = rope is not None
    const2 = lambda i: (0, 0)
    in_specs = [
        pl.BlockSpec((tm, d), lambda i: (i, 0)),
        pl.BlockSpec((1, 1, d), lambda i: ((i // per_group) * 6 + sc_j, 0, 0)),
        pl.BlockSpec((1, 1, d), lambda i: ((i // per_group) * 6 + sh_j, 0, 0)),
        pl.BlockSpec((1, d), const2),
        pl.BlockSpec(w_in.shape, const2),
    ]
    args = [x, mod, mod, norm_g, w_in]
    if with_q:
        in_specs += [pl.BlockSpec(q_g.shape, const2), pl.BlockSpec(w_uq.shape, const2)]
        args += [q_g, w_uq]
    in_specs += [pl.BlockSpec(kv_g.shape, const2), pl.BlockSpec(w_uk.shape, const2),
                 pl.BlockSpec(w_uv.shape, const2)]
    args += [kv_g, w_uk, w_uv]
    if use_rope:
        s_tiles = rope[0].shape[0] // tm
        for t in rope:
            in_specs.append(pl.BlockSpec((tm, LANES), lambda i: (i % s_tiles, 0)))
            args.append(t)
    kw = MLA_HEADS * MLA_QK_PAD
    vw = MLA_HEADS * MLA_V
    out_shape = [jax.ShapeDtypeStruct((m, kw), BF16), jax.ShapeDtypeStruct((m, vw), BF16)]
    out_specs = [pl.BlockSpec((tm, kw), lambda i: (i, 0)), pl.BlockSpec((tm, vw), lambda i: (i, 0))]
    if with_q:
        out_shape = [jax.ShapeDtypeStruct((m, kw), BF16)] + out_shape
        out_specs = [pl.BlockSpec((tm, kw), lambda i: (i, 0))] + out_specs
    return pl.pallas_call(
        functools.partial(_proj_mla_kernel, with_q=with_q, use_rope=use_rope),
        out_shape=tuple(out_shape),
        grid=(m // tm,),
        in_specs=in_specs,
        out_specs=tuple(out_specs),
        compiler_params=_cparams("parallel"),
        name="proj_mla",
    )(*args)


def _attn_kernel(*refs, n_seg, tq, n_q):
    q_ref = refs[0]
    kv = refs[1:1 + 2 * n_seg]
    o_ref = refs[1 + 2 * n_seg]

    def body(c, carry):
        r0 = pl.multiple_of(c * tq, tq)
        q = q_ref[pl.ds(r0, tq), :]
        ss = [_dot_nt(q, kv[2 * j][...]) for j in range(n_seg)]
        mx = ss[0].max(axis=-1, keepdims=True)
        for s in ss[1:]:
            mx = jnp.maximum(mx, s.max(axis=-1, keepdims=True))
        acc = None
        den = None
        for j, s in enumerate(ss):
            e = jnp.exp(s - mx)
            d = e.sum(axis=-1, keepdims=True)
            o = _dot(e.astype(BF16), kv[2 * j + 1][...])
            acc = o if acc is None else acc + o
            den = d if den is None else den + d
        o_ref[pl.ds(r0, tq), :] = (acc / den).astype(o_ref.dtype)
        return carry

    lax.fori_loop(0, n_q, body, 0)


def _attention(q, segs, batch, heads, kv_heads, dk, dv, tq):
    t_q = q.shape[0] // batch
    group = heads // kv_heads
    in_specs = [pl.BlockSpec((t_q, dk), lambda b, h: (b, h))]
    args = [q]
    for k, v in segs:
        t_k = k.shape[0] // batch
        in_specs.append(pl.BlockSpec((t_k, dk), lambda b, h: (b, h // group)))
        in_specs.append(pl.BlockSpec((t_k, dv), lambda b, h: (b, h // group)))
        args += [k, v]
    tq = min(tq, t_q)
    return pl.pallas_call(
        functools.partial(_attn_kernel, n_seg=len(segs), tq=tq, n_q=t_q // tq),
        out_shape=jax.ShapeDtypeStruct((q.shape[0], heads * dv), BF16),
        grid=(batch, heads),
        in_specs=in_specs,
        out_specs=pl.BlockSpec((t_q, dv), lambda b, h: (b, h)),
        compiler_params=_cparams("parallel", "parallel"),
        name="attention",
    )(*args)


def _out_router_kernel(*refs, n_in, tm):
    ins = refs[:2 * n_in]
    (x_ref, g1_ref, sc_ref, sh_ref, ng_ref, rw_ref, rb_ref, cnt0_ref,
     xo_ref, h2_ref, idx_ref, gate_ref, cnt_ref) = refs[2 * n_in:]
    i = pl.program_id(0)

    @pl.when(i == 0)
    def _():
        cnt_ref[...] = cnt0_ref[...]

    y = None
    for j in range(n_in):
        t = _dot(ins[2 * j][...], ins[2 * j + 1][...])
        y = t if y is None else y + t
    xn = x_ref[...] + g1_ref[0] * y
    xo_ref[...] = xn
    h2 = _rms(xn, ng_ref[...]) * (1.0 + sc_ref[0]) + sh_ref[0]
    h2_ref[...] = h2

    h_hi, h_lo = _split2(h2)
    w_hi, w_lo = _split2(rw_ref[...])
    logits = _dot_nt(w_hi, h_hi) + _dot_nt(w_hi, h_lo) + _dot_nt(w_lo, h_hi) + rb_ref[...]

    e_iota = lax.broadcasted_iota(jnp.int32, logits.shape, 0)
    work = logits
    tops, idxs = [], []
    for _ in range(TOP_K):
        mx = work.max(axis=0, keepdims=True)
        ix = jnp.where(work == mx, e_iota, N_EXPERTS).min(axis=0, keepdims=True)
        tops.append(mx)
        idxs.append(ix)
        work = jnp.where(e_iota == ix, -jnp.inf, work)
    exps = [jnp.exp(t - tops[0]) for t in tops]
    den = exps[0] + exps[1] + exps[2] + exps[3]

    for k in range(TOP_K):
        idx_ref[k:k + 1, :] = idxs[k]
        gate_ref[k:k + 1, :] = exps[k] / den
    cnt = sum((e_iota == ix).astype(F32) for ix in idxs)
    cnt_ref[...] = cnt_ref[...] + cnt.sum(axis=1, keepdims=True)


def _out_router(ins, x, mod, mod_rows, group_rows, norm_g, router_wt, router_b, cnt0, tm):
    m, d = x.shape
    per_group = group_rows // tm
    g1_j, sc_j, sh_j = mod_rows
    in_specs, args = [], []
    for a, w in ins:
        in_specs += [pl.BlockSpec((tm, a.shape[1]), lambda i: (i, 0)), pl.BlockSpec(w.shape, lambda i: (0, 0))]
        args += [a, w]
    mod_spec = lambda j: pl.BlockSpec((1, 1, d), lambda i: ((i // per_group) * 6 + j, 0, 0))
    in_specs += [
        pl.BlockSpec((tm, d), lambda i: (i, 0)),
        mod_spec(g1_j), mod_spec(sc_j), mod_spec(sh_j),
        pl.BlockSpec((1, d), lambda i: (0, 0)),
        pl.BlockSpec(router_wt.shape, lambda i: (0, 0)),
        pl.BlockSpec(router_b.shape, lambda i: (0, 0)),
        pl.BlockSpec(cnt0.shape, lambda i: (0, 0)),
    ]
    args += [x, mod, mod, mod, norm_g, router_wt, router_b, cnt0]
    small = lambda dt: jax.ShapeDtypeStruct((TOP_K, m), dt)
    small_spec = pl.BlockSpec((TOP_K, tm), lambda i: (0, i))
    return pl.pallas_call(
        functools.partial(_out_router_kernel, n_in=len(ins), tm=tm),
        out_shape=(jax.ShapeDtypeStruct((m, d), F32), jax.ShapeDtypeStruct((m, d), F32),
                   small(jnp.int32), small(F32),
                   jax.ShapeDtypeStruct(cnt0.shape, F32)),
        grid=(m // tm,),
        in_specs=in_specs,
        out_specs=(pl.BlockSpec((tm, d), lambda i: (i, 0)), pl.BlockSpec((tm, d), lambda i: (i, 0)),
                   small_spec, small_spec,
                   pl.BlockSpec(cnt0.shape, lambda i: (0, 0))),
        compiler_params=_cparams("arbitrary"),
        name="out_router",
    )(*args)


def _moe_kernel(blk_e_ref, nused_ref, tok0_ref, tokn_ref, dstp_ref, dstc_ref, h_hbm, w1_ref, b1_ref, w2_ref, b2_ref,
                o_hbm, x0, x1, y0, y1, gsem, ssem, *, tm, dump_row):
    i = pl.program_id(0)
    n_used = nused_ref[0]
    xs, ys = (x0, x1), (y0, y1)

    def gather(tok_ref, xbuf, sem):
        for r in range(tm):
            pltpu.make_async_copy(h_hbm.at[pl.ds(tok_ref[0, 0, r], 1)], xbuf.at[pl.ds(r, 1)], sem).start()

    def scatter(dst_ref, ybuf, sem, to_dump=None):
        for r in range(tm):
            row = dst_ref[0, 0, r]
            if to_dump is not None:
                row = jnp.where(to_dump, dump_row + r, row)
            pltpu.make_async_copy(ybuf.at[pl.ds(r, 1)], o_hbm.at[pl.ds(row, 1)], sem).start()

    def wait_gather(xbuf, sem):
        pltpu.make_async_copy(h_hbm.at[pl.ds(0, tm)], xbuf, sem).wait()

    def wait_scatter(ybuf, sem):
        pltpu.make_async_copy(ybuf, o_hbm.at[pl.ds(0, tm)], sem).wait()

    @pl.when(i == 0)
    def _():
        gather(tok0_ref, x0, gsem.at[0])
        y1[...] = jnp.zeros_like(y1)

    def step(p):
        xb, xn, yb, yp = xs[p], xs[1 - p], ys[p], ys[1 - p]
        wait_gather(xb, gsem.at[p])

        @pl.when(i >= 1)
        def _():
            wait_scatter(yb, ssem.at[p])

        scatter(dstp_ref, yp, ssem.at[1 - p], to_dump=(i == 0))
        gather(tokn_ref, xn, gsem.at[1 - p])
        x = xb[...].astype(BF16)
        hid = _dot(x, w1_ref[0]) + b1_ref[0]
        f = hid.shape[1] // 2
        g = jnp.minimum(hid[:, :f], SWIGLU_LIMIT)
        lin = jnp.clip(hid[:, f:], -SWIGLU_LIMIT, SWIGLU_LIMIT)
        act = g * jax.nn.sigmoid(SWIGLU_ALPHA * g) * (lin + 1.0)
        yb[...] = _dot(act.astype(BF16), w2_ref[0]) + b2_ref[0]

        @pl.when(i == n_used - 1)
        def _():
            wait_scatter(yp, ssem.at[1 - p])
            scatter(dstc_ref, yb, ssem.at[p])
            wait_scatter(yb, ssem.at[p])
            wait_gather(xn, gsem.at[1 - p])

    for p in range(2):
        @pl.when(jnp.logical_and(i < n_used, i % 2 == p))
        def _(p=p):
            step(p)


def _moe(h2, row_tok, row_dst, blk_e, n_used, w1, b1, w2, b2, out_rows, dump_row, tm):
    n_tiles = row_tok.shape[0]
    d = h2.shape[1]
    f2 = w1.shape[2]
    smem = lambda fn: pl.BlockSpec((1, 1, tm), fn, memory_space=pltpu.SMEM)
    grid_spec = pltpu.PrefetchScalarGridSpec(
        num_scalar_prefetch=2,
        grid=(n_tiles,),
        in_specs=[
            smem(lambda i, be, nu: (0, 0, 0)),
            smem(lambda i, be, nu: (jnp.minimum(i + 1, nu[0] - 1), 0, 0)),
            smem(lambda i, be, nu: (jnp.maximum(i - 1, 0), 0, 0)),
            smem(lambda i, be, nu: (i, 0, 0)),
            pl.BlockSpec(memory_space=pl.ANY),
            pl.BlockSpec((1, d, f2), lambda i, be, nu: (be[i], 0, 0)),
            pl.BlockSpec((1, 1, f2), lambda i, be, nu: (be[i], 0, 0)),
            pl.BlockSpec((1, f2 // 2, d), lambda i, be, nu: (be[i], 0, 0)),
            pl.BlockSpec((1, 1, d), lambda i, be, nu: (be[i], 0, 0)),
        ],
        out_specs=pl.BlockSpec(memory_space=pl.ANY),
        scratch_shapes=[pltpu.VMEM((tm, d), F32), pltpu.VMEM((tm, d), F32),
                        pltpu.VMEM((tm, d), F32), pltpu.VMEM((tm, d), F32),
                        pltpu.SemaphoreType.DMA((2,)), pltpu.SemaphoreType.DMA((2,))],
    )
    return pl.pallas_call(
        functools.partial(_moe_kernel, tm=tm, dump_row=dump_row),
        out_shape=jax.ShapeDtypeStruct((out_rows, d), F32),
        grid_spec=grid_spec,
        compiler_params=_cparams("arbitrary"),
        name="moe_experts",
    )(blk_e, n_used, row_tok, row_tok, row_dst, row_dst, h2, w1, b1, w2, b2)


def _combine_kernel(*refs, tq, final):
    ys = refs[:TOP_K]
    if final:
        gate_ref, x_ref, g2_ref, fg_ref, o_ref = refs[TOP_K:]
    else:
        gate_ref, x_ref, g2_ref, o_ref = refs[TOP_K:]
    f = None
    d = x_ref.shape[1]
    for k in range(TOP_K):
        gcol = jnp.broadcast_to(gate_ref[k:k + 1, :], (LANES, tq)).T
        t = ys[k][...] * jnp.tile(gcol, (1, d // LANES))
        f = t if f is None else f + t
    xn = x_ref[...] + g2_ref[0] * f
    if final:
        xn = _rms(xn, fg_ref[...])
    o_ref[...] = xn


def _combine(ys, t_tot, t_off, gate, x, mod, g2_j, group_rows, final_g, tq):
    m, d = x.shape
    per_group = group_rows // tq
    final = final_g is not None
    in_specs = [pl.BlockSpec((tq, d), functools.partial(lambda i, k: ((k * t_tot + t_off) // tq + i, 0), k=k))
                for k in range(TOP_K)]
    in_specs += [
        pl.BlockSpec((TOP_K, tq), lambda i: (0, i)),
        pl.BlockSpec((tq, d), lambda i: (i, 0)),
        pl.BlockSpec((1, 1, d), lambda i: ((i // per_group) * 6 + g2_j, 0, 0)),
    ]
    args = [ys] * TOP_K + [gate, x, mod]
    if final:
        in_specs.append(pl.BlockSpec((1, d), lambda i: (0, 0)))
        args.append(final_g)
    return pl.pallas_call(
        functools.partial(_combine_kernel, tq=tq, final=final),
        out_shape=jax.ShapeDtypeStruct((m, d), F32),
        grid=(m // tq,),
        in_specs=in_specs,
        out_specs=pl.BlockSpec((tq, d), lambda i: (i, 0)),
        compiler_params=_cparams("parallel"),
        name="moe_combine",
    )(*args)


def _rope_tables(seq, rot_dim):
    t = np.arange(seq)
    row = (t // GRID_W).astype(np.float32)
    col = (t % GRID_W).astype(np.float32)
    axis_dim = rot_dim // 2
    quarter = axis_dim // 2
    inv = jnp.asarray(ROPE_THETA, F32) ** (-jnp.arange(0, axis_dim, 2, dtype=F32) / axis_dim)
    ang_r = jnp.asarray(row)[:, None] * inv
    ang_c = jnp.asarray(col)[:, None] * inv
    zeros = jnp.zeros_like(ang_r)
    cos = jnp.concatenate([jnp.cos(ang_r)] * 2 + [jnp.cos(ang_c)] * 2, axis=1)
    s_a = jnp.concatenate([-jnp.sin(ang_r), zeros, -jnp.sin(ang_c), zeros], axis=1)
    s_b = jnp.concatenate([zeros, jnp.sin(ang_r), zeros, jnp.sin(ang_c)], axis=1)
    pad = LANES - rot_dim
    if pad:
        cos, s_a, s_b = (jnp.pad(a, ((0, 0), (0, pad))) for a in (cos, s_a, s_b))
    del quarter
    return cos, s_a, s_b


def _route(idx, counts, tm):
    t_tot = idx.shape[1]
    n_assign = t_tot * TOP_K
    n_rows = n_assign + N_EXPERTS * tm
    n_tiles = n_rows // tm
    counts = counts.astype(jnp.int32)
    pad_need = (-counts) % tm
    flat_e = idx.T.reshape(-1)
    slot = jnp.arange(tm, dtype=jnp.int32)[None, :]
    pad_keys = jnp.where(slot < pad_need[:, None], jnp.arange(N_EXPERTS, dtype=jnp.int32)[:, None], N_EXPERTS)
    keys = jnp.concatenate([flat_e, pad_keys.reshape(-1)])
    vals = jnp.concatenate([jnp.arange(n_assign, dtype=jnp.int32), jnp.full((N_EXPERTS * tm,), -1, jnp.int32)])
    skeys, svals = lax.sort((keys, vals), num_keys=1, is_stable=True)
    real = svals >= 0
    row_tok = jnp.where(real, svals // TOP_K, 0)
    rows = jnp.arange(n_rows, dtype=jnp.int32)
    row_dst = jnp.where(real, (svals % TOP_K) * t_tot + svals // TOP_K, n_assign + rows % tm)
    tile_e = skeys[::tm]
    n_used = jnp.sum(tile_e < N_EXPERTS).astype(jnp.int32)
    last_e = tile_e[jnp.maximum(n_used - 1, 0)]
    blk_e = jnp.where(tile_e < N_EXPERTS, tile_e, last_e).astype(jnp.int32)
    return row_tok.reshape(n_tiles, 1, tm), row_dst.reshape(n_tiles, 1, tm), blk_e, n_used.reshape(1)


def _pick(m, pref):
    t = pref
    while m % t:
        t //= 2
    return t


def kernel(x, c, ctx, c_ctx, ada_w, ada_b, norm_mix_g, norm_ffn_g, ab_w_in, sgu_norm_g, sgu_w, sgu_b,
           gqa_q_norm_g, gqa_k_norm_g, ab_w_out, mla_w_in, mla_q_norm_g, mla_kv_norm_g, mla_w_uq, mla_w_ukv,
           mla_w_out, router_w, router_b, moe_w1, moe_b1, moe_w2, moe_b2, final_norm_g):
    bn, s, d = x.shape
    l = ctx.shape[1]
    depth = ada_w.shape[0]
    m_lat, m_ctx = bn * s, bn * l
    tm_lat = _pick(s, 512)
    tm_ctx = _pick(l, 512)
    tm_moe = 512
    tq_comb = _pick(l, 256)

    n_cond = (bn + 1 + 7) // 8 * 8
    cond = jnp.concatenate([c, c_ctx[None, :], jnp.zeros((n_cond - bn - 1, d), F32)], axis=0)
    mod_all = _ada(cond, ada_w, ada_b)

    rope_gqa = _rope_tables(s, HEAD_DIM)
    rope_mla = _rope_tables(s, MLA_ROPE)

    xl = x.reshape(m_lat, d)
    xc = ctx.reshape(m_ctx, d)
    for i in range(depth):
        ctx_out = i < depth - 1
        j = i // 2
        mod_lat = mod_all[i, :bn].reshape(bn * 6, 1, d)
        mod_ctx = mod_all[i, bn].reshape(6, 1, d)
        ng = norm_mix_g[i].reshape(1, d)
        if i % 2 == 0:
            w_in = ab_w_in[j].astype(BF16)
            w_out = ab_w_out[j].astype(BF16)
            sgu_g = sgu_norm_g[j].reshape(1, SGU_WIDTH)
            sgu_wb = sgu_w[j].astype(BF16)
            sgu_bfull = jnp.broadcast_to(sgu_b[j][:, :, None], (SGU_GROUPS, CHUNK, SGU_GROUP_CH))
            q_g = gqa_q_norm_g[j].reshape(1, HEAD_DIM)
            k_g = gqa_k_norm_g[j].reshape(1, HEAD_DIM)
            oa, q, k, v = _proj_ab(xl, mod_lat, (1, 0), s, ng, w_in, sgu_g, sgu_wb, sgu_bfull, q_g, k_g,
                                   rope_gqa, tm_lat)
            oac, qc, kc, vc = _proj_ab(xc, mod_ctx, (1, 0), m_ctx, ng, w_in, sgu_g, sgu_wb, sgu_bfull, q_g, k_g,
                                       None, tm_ctx)
            ob = _attention(q, [(kc, vc), (k, v)], bn, GQA_HEADS, GQA_KV_HEADS, HEAD_DIM, HEAD_DIM, 512)
            ins = [(oa, w_out[:SGU_WIDTH]), (ob, w_out[SGU_WIDTH:])]
            if ctx_out:
                obc = _attention(qc, [(kc, vc)], bn, GQA_HEADS, GQA_KV_HEADS, HEAD_DIM, HEAD_DIM, 512)
                ins_c = [(oac, w_out[:SGU_WIDTH]), (obc, w_out[SGU_WIDTH:])]
        else:
            w_in_f = mla_w_in[j]
            w_in = jnp.pad(w_in_f, ((0, 0), (0, MLA_IN_PAD - w_in_f.shape[1]))).astype(BF16)
            w_in_kv = w_in[:, MLA_Q_LORA:]
            w_out = mla_w_out[j].astype(BF16)
            q_g = mla_q_norm_g[j].reshape(1, MLA_Q_LORA)
            kv_g = mla_kv_norm_g[j].reshape(1, MLA_KV_LORA)
            w_uq = mla_w_uq[j].reshape(MLA_Q_LORA, MLA_HEADS, MLA_QK)
            w_uq = jnp.pad(w_uq, ((0, 0), (0, 0), (0, MLA_QK_PAD - MLA_QK)))
            w_uq = w_uq.reshape(MLA_Q_LORA, MLA_HEADS * MLA_QK_PAD).astype(BF16)
            w_ukv = mla_w_ukv[j].reshape(MLA_KV_LORA, MLA_HEADS, MLA_NOPE + MLA_V)
            w_uk = w_ukv[:, :, :MLA_NOPE].reshape(MLA_KV_LORA, MLA_HEADS * MLA_NOPE).astype(BF16)
            w_uv = w_ukv[:, :, MLA_NOPE:].reshape(MLA_KV_LORA, MLA_HEADS * MLA_V).astype(BF16)
            q, k, v = _proj_mla(xl, mod_lat, (1, 0), s, ng, w_in, q_g, w_uq, kv_g, w_uk, w_uv, rope_mla, True,
                                tm_lat)
            if ctx_out:
                qc, kc, vc = _proj_mla(xc, mod_ctx, (1, 0), m_ctx, ng, w_in, q_g, w_uq, kv_g, w_uk, w_uv, None,
                                       True, tm_ctx)
            else:
                kc, vc = _proj_mla(xc, mod_ctx, (1, 0), m_ctx, ng, w_in_kv, None, None, kv_g, w_uk, w_uv, None,
                                   False, tm_ctx)
            ob = _attention(q, [(kc, vc), (k, v)], bn, MLA_HEADS, MLA_HEADS, MLA_QK_PAD, MLA_V, 512)
            ins = [(ob, w_out)]
            if ctx_out:
                obc = _attention(qc, [(kc, vc)], bn, MLA_HEADS, MLA_HEADS, MLA_QK_PAD, MLA_V, 512)
                ins_c = [(obc, w_out)]

        nfg = norm_ffn_g[i].reshape(1, d)
        rwt = router_w[i].T
        rb = router_b[i].reshape(N_EXPERTS, 1)
        cnt0 = jnp.zeros((N_EXPERTS, LANES), F32)
        xl, h2, idx, gate, cnt = _out_router(ins, xl, mod_lat, (2, 4, 3), s, nfg, rwt, rb, cnt0, tm_lat)
        if ctx_out:
            xc, h2c, idx_c, gate_c, cnt = _out_router(ins_c, xc, mod_ctx, (2, 4, 3), m_ctx, nfg, rwt, rb, cnt, tm_ctx)
            h2 = jnp.concatenate([h2, h2c], axis=0)
            idx = jnp.concatenate([idx, idx_c], axis=1)
        t_tot = h2.shape[0]
        n_assign = t_tot * TOP_K
        row_tok, row_dst, blk_e, n_used = _route(idx, cnt[:, 0], tm_moe)
        ys = _moe(h2, row_tok, row_dst, blk_e, n_used, moe_w1[i].astype(BF16), moe_b1[i].reshape(N_EXPERTS, 1, -1),
                  moe_w2[i].astype(BF16), moe_b2[i].reshape(N_EXPERTS, 1, -1), n_assign + tm_moe,
                  n_assign, tm_moe)
        last = i == depth - 1
        fg = final_norm_g.reshape(1, d) if last else None
        xl = _combine(ys, t_tot, 0, gate, xl, mod_lat, 5, s, fg, tq_comb)
        if ctx_out:
            xc = _combine(ys, t_tot, m_lat, gate_c, xc, mod_ctx, 5, m_ctx, None, tq_comb)
    return xl.reshape(bn, s, d)
```

```python
import functools

import jax
import jax.numpy as jnp
import numpy as np
from jax import lax
from jax.experimental import pallas as pl
from jax.experimental.pallas import tpu as pltpu

F32 = jnp.float32
BF16 = jnp.bfloat16

GRID_W = 64
EPS = 1e-6
ROPE_THETA = 10000.0
CHUNK = 128
SGU_GROUPS = 4
SGU_GROUP_CH = 128
SGU_WIDTH = SGU_GROUPS * SGU_GROUP_CH
GQA_HEADS = 4
GQA_KV_HEADS = 2
HEAD_DIM = 128
GQA_Q_W = GQA_HEADS * HEAD_DIM
GQA_KV_W = GQA_KV_HEADS * HEAD_DIM
A_END = 2 * SGU_WIDTH
K_OFF = A_END + GQA_Q_W
V_OFF = K_OFF + GQA_KV_W
MLA_HEADS = 8
MLA_Q_LORA = 512
MLA_KV_LORA = 256
MLA_NOPE = 128
MLA_ROPE = 64
MLA_V = 128
MLA_QK = MLA_NOPE + MLA_ROPE
MLA_QK_PAD = 256
MLA_IN_PAD = 896
N_EXPERTS = 32
TOP_K = 4
D_EXPERT = 1024
SWIGLU_LIMIT = 7.0
SWIGLU_ALPHA = 1.702

LANES = 128
ATTN_TQ = 256
VMEM_LIMIT = 56 * 1024 * 1024


def _cparams(*sem):
    return pltpu.CompilerParams(dimension_semantics=sem, vmem_limit_bytes=VMEM_LIMIT)


def _dot(a, b):
    return jnp.dot(a, b, preferred_element_type=F32)


def _dot_nt(a, b):
    return lax.dot_general(a, b, (((1,), (1,)), ((), ())), preferred_element_type=F32)


def _split2(a):
    hi = a.astype(BF16)
    lo = (a - hi.astype(F32)).astype(BF16)
    return hi, lo


def _rms(x, g):
    return x * lax.rsqrt(jnp.mean(x * x, axis=-1, keepdims=True) + EPS) * g


def _rope(x, c, s_a, s_b, half):
    n = x.shape[-1]
    return x * c + pltpu.roll(x, n - half, 1) * s_a + pltpu.roll(x, half, 1) * s_b


def _ada_kernel(c_ref, w_ref, b_ref, o_ref):
    c = c_ref[...]
    sc = c * jax.nn.sigmoid(c)
    c_hi, c_lo = _split2(sc)
    w = w_ref[0]
    w_hi, w_lo = _split2(w)
    acc = _dot(c_hi, w_hi) + _dot(c_hi, w_lo) + _dot(c_lo, w_hi)
    o_ref[0] = acc + b_ref[0]


def _ada(cond, ada_w, ada_b):
    depth, d, n = ada_w.shape
    rows = cond.shape[0]
    tn = 1536
    return pl.pallas_call(
        _ada_kernel,
        out_shape=jax.ShapeDtypeStruct((depth, rows, n), F32),
        grid=(depth, n // tn),
        in_specs=[
            pl.BlockSpec((rows, d), lambda l, j: (0, 0)),
            pl.BlockSpec((1, d, tn), lambda l, j: (l, 0, j)),
            pl.BlockSpec((1, 1, tn), lambda l, j: (l, 0, j)),
        ],
        out_specs=pl.BlockSpec((1, rows, tn), lambda l, j: (l, 0, j)),
        compiler_params=_cparams("arbitrary", "arbitrary"),
        name="ada_mod",
    )(cond, ada_w, ada_b.reshape(depth, 1, n))


def _proj_ab_kernel(*refs, use_rope, tm):
    if use_rope:
        (x_ref, sc_ref, sh_ref, g_ref, w_ref, sgug_ref, sguw_ref, sgub_ref, qg_ref, kg_ref,
         c_ref, sa_ref, sb_ref, oa_ref, q_ref, k_ref, v_ref) = refs
    else:
        (x_ref, sc_ref, sh_ref, g_ref, w_ref, sgug_ref, sguw_ref, sgub_ref, qg_ref, kg_ref,
         oa_ref, q_ref, k_ref, v_ref) = refs
    x = x_ref[...]
    h = _rms(x, g_ref[...]) * (1.0 + sc_ref[0]) + sh_ref[0]
    p = _dot(h.astype(BF16), w_ref[...])

    n_chunks = tm // CHUNK
    for g in range(SGU_GROUPS):
        u = jax.nn.gelu(p[:, g * SGU_GROUP_CH:(g + 1) * SGU_GROUP_CH])
        v = jax.nn.gelu(p[:, SGU_WIDTH + g * SGU_GROUP_CH:SGU_WIDTH + (g + 1) * SGU_GROUP_CH])
        v = _rms(v, sgug_ref[:, g * SGU_GROUP_CH:(g + 1) * SGU_GROUP_CH]).astype(BF16)
        vcat = jnp.concatenate([v[c * CHUNK:(c + 1) * CHUNK, :] for c in range(n_chunks)], axis=1)
        mixed = _dot(sguw_ref[g], vcat)
        bias = sgub_ref[g]
        for c in range(n_chunks):
            gate = mixed[:, c * SGU_GROUP_CH:(c + 1) * SGU_GROUP_CH] + bias
            oa_ref[c * CHUNK:(c + 1) * CHUNK, g * SGU_GROUP_CH:(g + 1) * SGU_GROUP_CH] = (
                u[c * CHUNK:(c + 1) * CHUNK, :] * gate).astype(oa_ref.dtype)

    qg = qg_ref[...] * (HEAD_DIM ** -0.5)
    for hd in range(GQA_HEADS):
        qh = _rms(p[:, A_END + hd * HEAD_DIM:A_END + (hd + 1) * HEAD_DIM], qg)
        if use_rope:
            qh = _rope(qh, c_ref[...], sa_ref[...], sb_ref[...], HEAD_DIM // 4)
        q_ref[:, hd * HEAD_DIM:(hd + 1) * HEAD_DIM] = qh.astype(q_ref.dtype)
    for hd in range(GQA_KV_HEADS):
        kh = _rms(p[:, K_OFF + hd * HEAD_DIM:K_OFF + (hd + 1) * HEAD_DIM], kg_ref[...])
        if use_rope:
            kh = _rope(kh, c_ref[...], sa_ref[...], sb_ref[...], HEAD_DIM // 4)
        k_ref[:, hd * HEAD_DIM:(hd + 1) * HEAD_DIM] = kh.astype(k_ref.dtype)
    v_ref[...] = p[:, V_OFF:].astype(v_ref.dtype)


def _proj_ab(x, mod, mod_rows, group_rows, norm_g, w_in, sgu_g, sgu_w, sgu_bfull, q_g, k_g, rope, tm):
    m, d = x.shape
    per_group = group_rows // tm
    sc_j, sh_j = mod_rows
    use_rope = rope is not None
    in_specs = [
        pl.BlockSpec((tm, d), lambda i: (i, 0)),
        pl.BlockSpec((1, 1, d), lambda i: ((i // per_group) * 6 + sc_j, 0, 0)),
        pl.BlockSpec((1, 1, d), lambda i: ((i // per_group) * 6 + sh_j, 0, 0)),
        pl.BlockSpec((1, d), lambda i: (0, 0)),
        pl.BlockSpec(w_in.shape, lambda i: (0, 0)),
        pl.BlockSpec((1, SGU_WIDTH), lambda i: (0, 0)),
        pl.BlockSpec(sgu_w.shape, lambda i: (0, 0, 0)),
        pl.BlockSpec(sgu_bfull.shape, lambda i: (0, 0, 0)),
        pl.BlockSpec((1, HEAD_DIM), lambda i: (0, 0)),
        pl.BlockSpec((1, HEAD_DIM), lambda i: (0, 0)),
    ]
    args = [x, mod, mod, norm_g, w_in, sgu_g, sgu_w, sgu_bfull, q_g, k_g]
    if use_rope:
        s_tiles = rope[0].shape[0] // tm
        for t in rope:
            in_specs.append(pl.BlockSpec((tm, HEAD_DIM), lambda i: (i % s_tiles, 0)))
            args.append(t)
    return pl.pallas_call(
        functools.partial(_proj_ab_kernel, use_rope=use_rope, tm=tm),
        out_shape=(jax.ShapeDtypeStruct((m, SGU_WIDTH), BF16),
                   jax.ShapeDtypeStruct((m, GQA_Q_W), BF16),
                   jax.ShapeDtypeStruct((m, GQA_KV_W), BF16),
                   jax.ShapeDtypeStruct((m, GQA_KV_W), BF16)),
        grid=(m // tm,),
        in_specs=in_specs,
        out_specs=(pl.BlockSpec((tm, SGU_WIDTH), lambda i: (i, 0)),
                   pl.BlockSpec((tm, GQA_Q_W), lambda i: (i, 0)),
                   pl.BlockSpec((tm, GQA_KV_W), lambda i: (i, 0)),
                   pl.BlockSpec((tm, GQA_KV_W), lambda i: (i, 0))),
        compiler_params=_cparams("parallel"),
        name="proj_ab",
    )(*args)


def _proj_mla_kernel(*refs, with_q, use_rope):
    it = iter(refs)
    x_ref, sc_ref, sh_ref, g_ref, w_ref = (next(it) for _ in range(5))
    if with_q:
        qg_ref, wuq_ref = next(it), next(it)
    kvg_ref, wuk_ref, wuv_ref = next(it), next(it), next(it)
    if use_rope:
        c_ref, sa_ref, sb_ref = next(it), next(it), next(it)
    if with_q:
        q_ref = next(it)
    k_ref, v_ref = next(it), next(it)

    x = x_ref[...]
    h = _rms(x, g_ref[...]) * (1.0 + sc_ref[0]) + sh_ref[0]
    p = _dot(h.astype(BF16), w_ref[...])
    off = MLA_Q_LORA if with_q else 0
    ckv = _rms(p[:, off:off + MLA_KV_LORA], kvg_ref[...]).astype(BF16)
    kr = p[:, off + MLA_KV_LORA:off + MLA_KV_LORA + LANES]
    if use_rope:
        kr = _rope(kr, c_ref[...], sa_ref[...], sb_ref[...], MLA_ROPE // 4)
    kr = kr.astype(k_ref.dtype)
    k_nope = _dot(ckv, wuk_ref[...])
    v_ref[...] = _dot(ckv, wuv_ref[...]).astype(v_ref.dtype)
    for hd in range(MLA_HEADS):
        k_ref[:, hd * MLA_QK_PAD:hd * MLA_QK_PAD + MLA_NOPE] = (
            k_nope[:, hd * MLA_NOPE:(hd + 1) * MLA_NOPE].astype(k_ref.dtype))
        k_ref[:, hd * MLA_QK_PAD + MLA_NOPE:(hd + 1) * MLA_QK_PAD] = kr
    if with_q:
        cq = _rms(p[:, :MLA_Q_LORA], qg_ref[...]).astype(BF16)
        q = _dot(cq, wuq_ref[...]) * (MLA_QK ** -0.5)
        for hd in range(MLA_HEADS):
            q_ref[:, hd * MLA_QK_PAD:hd * MLA_QK_PAD + MLA_NOPE] = (
                q[:, hd * MLA_QK_PAD:hd * MLA_QK_PAD + MLA_NOPE].astype(q_ref.dtype))
            qr = q[:, hd * MLA_QK_PAD + MLA_NOPE:(hd + 1) * MLA_QK_PAD]
            if use_rope:
                qr = _rope(qr, c_ref[...], sa_ref[...], sb_ref[...], MLA_ROPE // 4)
            q_ref[:, hd * MLA_QK_PAD + MLA_NOPE:(hd + 1) * MLA_QK_PAD] = qr.astype(q_ref.dtype)


def _proj_mla(x, mod, mod_rows, group_rows, norm_g, w_in, q_g, w_uq, kv_g, w_uk, w_uv, rope, with_q, tm):
    m, d = x.shape
    per_group = group_rows // tm
    sc_j, sh_j = mod_rows
    use_rope = rope is not None
    const2 = lambda i: (0, 0)
    in_specs = [
        pl.BlockSpec((tm, d), lambda i: (i, 0)),
        pl.BlockSpec((1, 1, d), lambda i: ((i // per_group) * 6 + sc_j, 0, 0)),
        pl.BlockSpec((1, 1, d), lambda i: ((i // per_group) * 6 + sh_j, 0, 0)),
        pl.BlockSpec((1, d), const2),
        pl.BlockSpec(w_in.shape, const2),
    ]
    args = [x, mod, mod, norm_g, w_in]
    if with_q:
        in_specs += [pl.BlockSpec(q_g.shape, const2), pl.BlockSpec(w_uq.shape, const2)]
        args += [q_g, w_uq]
    in_specs += [pl.BlockSpec(kv_g.shape, const2), pl.BlockSpec(w_uk.shape, const2),
                 pl.BlockSpec(w_uv.shape, const2)]
    args += [kv_g, w_uk, w_uv]
    if use_rope:
        s_tiles = rope[0].shape[0] // tm
        for t in rope:
            in_specs.append(pl.BlockSpec((tm, LANES), lambda i: (i % s_tiles, 0)))
            args.append(t)
    kw = MLA_HEADS * MLA_QK_PAD
    vw = MLA_HEADS * MLA_V
    out_shape = [jax.ShapeDtypeStruct((m, kw), BF16), jax.ShapeDtypeStruct((m, vw), BF16)]
    out_specs = [pl.BlockSpec((tm, kw), lambda i: (i, 0)), pl.BlockSpec((tm, vw), lambda i: (i, 0))]
    if with_q:
        out_shape = [jax.ShapeDtypeStruct((m, kw), BF16)] + out_shape
        out_specs = [pl.BlockSpec((tm, kw), lambda i: (i, 0))] + out_specs
    return pl.pallas_call(
        functools.partial(_proj_mla_kernel, with_q=with_q, use_rope=use_rope),
        out_shape=tuple(out_shape),
        grid=(m // tm,),
        in_specs=in_specs,
        out_specs=tuple(out_specs),
        compiler_params=_cparams("parallel"),
        name="proj_mla",
    )(*args)


def _attn_kernel(*refs, n_seg, tq, n_q):
    q_ref = refs[0]
    kv = refs[1:1 + 2 * n_seg]
    o_ref = refs[1 + 2 * n_seg]

    for c in range(n_q):
        q = q_ref[c * tq:(c + 1) * tq, :]
        ss = [_dot_nt(q, kv[2 * j][...]) for j in range(n_seg)]
        mx = ss[0].max(axis=-1, keepdims=True)
        for s in ss[1:]:
            mx = jnp.maximum(mx, s.max(axis=-1, keepdims=True))
        acc = None
        den = None
        for j, s in enumerate(ss):
            e = jnp.exp(s - mx)
            d = e.sum(axis=-1, keepdims=True)
            o = _dot(e.astype(BF16), kv[2 * j + 1][...])
            acc = o if acc is None else acc + o
            den = d if den is None else den + d
        o_ref[c * tq:(c + 1) * tq, :] = (acc / den).astype(o_ref.dtype)


def _attention(q, segs, batch, heads, kv_heads, dk, dv, tq):
    t_q = q.shape[0] // batch
    group = heads // kv_heads
    in_specs = [pl.BlockSpec((t_q, dk), lambda b, h: (b, h))]
    args = [q]
    for k, v in segs:
        t_k = k.shape[0] // batch
        in_specs.append(pl.BlockSpec((t_k, dk), lambda b, h: (b, h // group)))
        in_specs.append(pl.BlockSpec((t_k, dv), lambda b, h: (b, h // group)))
        args += [k, v]
    tq = min(tq, t_q)
    return pl.pallas_call(
        functools.partial(_attn_kernel, n_seg=len(segs), tq=tq, n_q=t_q // tq),
        out_shape=jax.ShapeDtypeStruct((q.shape[0], heads * dv), BF16),
        grid=(batch, heads),
        in_specs=in_specs,
        out_specs=pl.BlockSpec((t_q, dv), lambda b, h: (b, h)),
        compiler_params=_cparams("parallel", "parallel"),
        name="attention",
    )(*args)


def _out_router_kernel(*refs, n_in, tm):
    ins = refs[:2 * n_in]
    (x_ref, g1_ref, sc_ref, sh_ref, ng_ref, rw_ref, rb_ref, cnt0_ref,
     xo_ref, h2_ref, idx_ref, gate_ref, cnt_ref) = refs[2 * n_in:]
    i = pl.program_id(0)

    @pl.when(i == 0)
    def _():
        cnt_ref[...] = cnt0_ref[...]

    y = None
    for j in range(n_in):
        t = _dot(ins[2 * j][...], ins[2 * j + 1][...])
        y = t if y is None else y + t
    xn = x_ref[...] + g1_ref[0] * y
    xo_ref[...] = xn
    h2 = _rms(xn, ng_ref[...]) * (1.0 + sc_ref[0]) + sh_ref[0]
    h2_ref[...] = h2

    h_hi, h_lo = _split2(h2)
    w_hi, w_lo = _split2(rw_ref[...])
    logits = _dot_nt(w_hi, h_hi) + _dot_nt(w_hi, h_lo) + _dot_nt(w_lo, h_hi) + rb_ref[...]

    e_iota = lax.broadcasted_iota(jnp.int32, logits.shape, 0)
    work = logits
    tops, idxs = [], []
    for _ in range(TOP_K):
        mx = work.max(axis=0, keepdims=True)
        ix = jnp.where(work == mx, e_iota, N_EXPERTS).min(axis=0, keepdims=True)
        tops.append(mx)
        idxs.append(ix)
        work = jnp.where(e_iota == ix, -jnp.inf, work)
    exps = [jnp.exp(t - tops[0]) for t in tops]
    den = exps[0] + exps[1] + exps[2] + exps[3]

    for k in range(TOP_K):
        idx_ref[k:k + 1, :] = idxs[k]
        gate_ref[k:k + 1, :] = exps[k] / den
    cnt = sum((e_iota == ix).astype(F32) for ix in idxs)
    cnt_ref[...] = cnt_ref[...] + cnt.sum(axis=1, keepdims=True)


def _out_router(ins, x, mod, mod_rows, group_rows, norm_g, router_wt, router_b, cnt0, tm):
    m, d = x.shape
    per_group = group_rows // tm
    g1_j, sc_j, sh_j = mod_rows
    in_specs, args = [], []
    for a, w in ins:
        in_specs += [pl.BlockSpec((tm, a.shape[1]), lambda i: (i, 0)), pl.BlockSpec(w.shape, lambda i: (0, 0))]
        args += [a, w]
    mod_spec = lambda j: pl.BlockSpec((1, 1, d), lambda i: ((i // per_group) * 6 + j, 0, 0))
    in_specs += [
        pl.BlockSpec((tm, d), lambda i: (i, 0)),
        mod_spec(g1_j), mod_spec(sc_j), mod_spec(sh_j),
        pl.BlockSpec((1, d), lambda i: (0, 0)),
        pl.BlockSpec(router_wt.shape, lambda i: (0, 0)),
        pl.BlockSpec(router_b.shape, lambda i: (0, 0)),
        pl.BlockSpec(cnt0.shape, lambda i: (0, 0)),
    ]
    args += [x, mod, mod, mod, norm_g, router_wt, router_b, cnt0]
    small = lambda dt: jax.ShapeDtypeStruct((TOP_K, m), dt)
    small_spec = pl.BlockSpec((TOP_K, tm), lambda i: (0, i))
    return pl.pallas_call(
        functools.partial(_out_router_kernel, n_in=len(ins), tm=tm),
        out_shape=(jax.ShapeDtypeStruct((m, d), F32), jax.ShapeDtypeStruct((m, d), F32),
                   small(jnp.int32), small(F32),
                   jax.ShapeDtypeStruct(cnt0.shape, F32)),
        grid=(m // tm,),
        in_specs=in_specs,
        out_specs=(pl.BlockSpec((tm, d), lambda i: (i, 0)), pl.BlockSpec((tm, d), lambda i: (i, 0)),
                   small_spec, small_spec,
                   pl.BlockSpec(cnt0.shape, lambda i: (0, 0))),
        compiler_params=_cparams("arbitrary"),
        name="out_router",
    )(*args)


def _moe_kernel(blk_e_ref, nused_ref, tok0_ref, tokn_ref, dstp_ref, dstc_ref, h_hbm, w1_ref, b1_ref, w2_ref, b2_ref,
                o_hbm, x0, x1, y0, y1, gsem, ssem, *, tm):
    i = pl.program_id(0)
    n_used = nused_ref[0]
    xs, ys = (x0, x1), (y0, y1)

    def gather(tok_ref, xbuf, sem):
        for r in range(tm):
            pltpu.make_async_copy(h_hbm.at[pl.ds(tok_ref[0, 0, r], 1)], xbuf.at[pl.ds(r, 1)], sem).start()

    def scatter(dst_ref, ybuf, sem):
        for r in range(tm):
            pltpu.make_async_copy(ybuf.at[pl.ds(r, 1)], o_hbm.at[pl.ds(dst_ref[0, 0, r], 1)], sem).start(priority=1)

    def wait_gather(xbuf, sem):
        pltpu.make_async_copy(h_hbm.at[pl.ds(0, tm)], xbuf, sem).wait()

    def wait_scatter(ybuf, sem):
        pltpu.make_async_copy(ybuf, o_hbm.at[pl.ds(0, tm)], sem).wait()

    @pl.when(i == 0)
    def _():
        gather(tok0_ref, x0, gsem.at[0])
        y1[...] = jnp.zeros_like(y1)

    def step(p):
        xb, xn, yb, yp = xs[p], xs[1 - p], ys[p], ys[1 - p]
        wait_gather(xb, gsem.at[p])

        @pl.when(i >= 1)
        def _():
            wait_scatter(yb, ssem.at[p])

        @pl.when(n_used > 0)
        def _():
            scatter(dstp_ref, yp, ssem.at[1 - p])
            gather(tokn_ref, xn, gsem.at[1 - p])

        x = xb[...].astype(BF16)
        hid = _dot(x, w1_ref[0]) + b1_ref[0]
        f = hid.shape[1] // 2
        g = jnp.minimum(hid[:, :f], SWIGLU_LIMIT)
        lin = jnp.clip(hid[:, f:], -SWIGLU_LIMIT, SWIGLU_LIMIT)
        act = g * jax.nn.sigmoid(SWIGLU_ALPHA * g) * (lin + 1.0)
        yb[...] = _dot(act.astype(BF16), w2_ref[0]) + b2_ref[0]

        @pl.when(i == n_used - 1)
        def _():
            wait_scatter(yp, ssem.at[1 - p])
            scatter(dstc_ref, yb, ssem.at[p])
            wait_scatter(yb, ssem.at[p])
            wait_gather(xn, gsem.at[1 - p])

    for p in range(2):
        @pl.when(jnp.logical_and(i < n_used, i % 2 == p))
        def _(p=p):
            step(p)


def _moe(h2, row_tok, row_dst, blk_e, n_used, w1, b1, w2, b2, out_rows, tm):
    n_tiles = row_tok.shape[0]
    d = h2.shape[1]
    f2 = w1.shape[2]
    smem = lambda fn: pl.BlockSpec((1, 1, tm), fn, memory_space=pltpu.SMEM)
    grid_spec = pltpu.PrefetchScalarGridSpec(
        num_scalar_prefetch=2,
        grid=(n_tiles,),
        in_specs=[
            smem(lambda i, be, nu: (0, 0, 0)),
            smem(lambda i, be, nu: (jnp.minimum(i + 1, nu[0] - 1), 0, 0)),
            smem(lambda i, be, nu: (i, 0, 0)),
            smem(lambda i, be, nu: (i + 1, 0, 0)),
            pl.BlockSpec(memory_space=pl.ANY),
            pl.BlockSpec((1, d, f2), lambda i, be, nu: (be[i], 0, 0)),
            pl.BlockSpec((1, 1, f2), lambda i, be, nu: (be[i], 0, 0)),
            pl.BlockSpec((1, f2 // 2, d), lambda i, be, nu: (be[i], 0, 0)),
            pl.BlockSpec((1, 1, d), lambda i, be, nu: (be[i], 0, 0)),
        ],
        out_specs=pl.BlockSpec(memory_space=pl.ANY),
        scratch_shapes=[pltpu.VMEM((tm, d), F32), pltpu.VMEM((tm, d), F32),
                        pltpu.VMEM((tm, d), F32), pltpu.VMEM((tm, d), F32),
                        pltpu.SemaphoreType.DMA((2,)), pltpu.SemaphoreType.DMA((2,))],
    )
    return pl.pallas_call(
        functools.partial(_moe_kernel, tm=tm),
        out_shape=jax.ShapeDtypeStruct((out_rows, d), F32),
        grid_spec=grid_spec,
        compiler_params=_cparams("arbitrary"),
        name="moe_experts",
    )(blk_e, n_used, row_tok, row_tok, row_dst, row_dst, h2, w1, b1, w2, b2)


def _combine_kernel(*refs, tq, final):
    ys = refs[:TOP_K]
    if final:
        gate_ref, x_ref, g2_ref, fg_ref, o_ref = refs[TOP_K:]
    else:
        gate_ref, x_ref, g2_ref, o_ref = refs[TOP_K:]
    f = None
    d = x_ref.shape[1]
    for k in range(TOP_K):
        gcol = jnp.broadcast_to(gate_ref[k:k + 1, :], (LANES, tq)).T
        t = ys[k][...] * jnp.tile(gcol, (1, d // LANES))
        f = t if f is None else f + t
    xn = x_ref[...] + g2_ref[0] * f
    if final:
        xn = _rms(xn, fg_ref[...])
    o_ref[...] = xn


def _combine(ys, t_tot, t_off, gate, x, mod, g2_j, group_rows, final_g, tq):
    m, d = x.shape
    per_group = group_rows // tq
    final = final_g is not None
    in_specs = [pl.BlockSpec((tq, d), functools.partial(lambda i, k: ((k * t_tot + t_off) // tq + i, 0), k=k))
                for k in range(TOP_K)]
    in_specs += [
        pl.BlockSpec((TOP_K, tq), lambda i: (0, i)),
        pl.BlockSpec((tq, d), lambda i: (i, 0)),
        pl.BlockSpec((1, 1, d), lambda i: ((i // per_group) * 6 + g2_j, 0, 0)),
    ]
    args = [ys] * TOP_K + [gate, x, mod]
    if final:
        in_specs.append(pl.BlockSpec((1, d), lambda i: (0, 0)))
        args.append(final_g)
    return pl.pallas_call(
        functools.partial(_combine_kernel, tq=tq, final=final),
        out_shape=jax.ShapeDtypeStruct((m, d), F32),
        grid=(m // tq,),
        in_specs=in_specs,
        out_specs=pl.BlockSpec((tq, d), lambda i: (i, 0)),
        compiler_params=_cparams("parallel"),
        name="moe_combine",
    )(*args)


def _rope_tables(seq, rot_dim):
    t = np.arange(seq)
    row = (t // GRID_W).astype(np.float32)
    col = (t % GRID_W).astype(np.float32)
    axis_dim = rot_dim // 2
    quarter = axis_dim // 2
    inv = jnp.asarray(ROPE_THETA, F32) ** (-jnp.arange(0, axis_dim, 2, dtype=F32) / axis_dim)
    ang_r = jnp.asarray(row)[:, None] * inv
    ang_c = jnp.asarray(col)[:, None] * inv
    zeros = jnp.zeros_like(ang_r)
    cos = jnp.concatenate([jnp.cos(ang_r)] * 2 + [jnp.cos(ang_c)] * 2, axis=1)
    s_a = jnp.concatenate([-jnp.sin(ang_r), zeros, -jnp.sin(ang_c), zeros], axis=1)
    s_b = jnp.concatenate([zeros, jnp.sin(ang_r), zeros, jnp.sin(ang_c)], axis=1)
    pad = LANES - rot_dim
    if pad:
        cos, s_a, s_b = (jnp.pad(a, ((0, 0), (0, pad))) for a in (cos, s_a, s_b))
    del quarter
    return cos, s_a, s_b


def _route(idx, counts, tm):
    t_tot = idx.shape[1]
    n_assign = t_tot * TOP_K
    n_rows = n_assign + N_EXPERTS * tm
    n_tiles = n_rows // tm
    counts = counts.astype(jnp.int32)
    pad_need = (-counts) % tm
    flat_e = idx.T.reshape(-1)
    slot = jnp.arange(tm, dtype=jnp.int32)[None, :]
    pad_keys = jnp.where(slot < pad_need[:, None], jnp.arange(N_EXPERTS, dtype=jnp.int32)[:, None], N_EXPERTS)
    keys = jnp.concatenate([flat_e, pad_keys.reshape(-1)])
    vals = jnp.concatenate([jnp.arange(n_assign, dtype=jnp.int32), jnp.full((N_EXPERTS * tm,), -1, jnp.int32)])
    skeys, svals = lax.sort((keys, vals), num_keys=1, is_stable=True)
    real = svals >= 0
    row_tok = jnp.where(real, svals // TOP_K, 0)
    rows = jnp.arange(n_rows, dtype=jnp.int32)
    row_dst = jnp.where(real, (svals % TOP_K) * t_tot + svals // TOP_K, n_assign + rows % tm)
    tile_e = skeys[::tm]
    n_used = jnp.sum(tile_e < N_EXPERTS).astype(jnp.int32)
    last_e = tile_e[jnp.maximum(n_used - 1, 0)]
    blk_e = jnp.where(tile_e < N_EXPERTS, tile_e, last_e).astype(jnp.int32)
    row_dst = jnp.concatenate([n_assign + jnp.arange(tm, dtype=jnp.int32), row_dst])
    return row_tok.reshape(n_tiles, 1, tm), row_dst.reshape(n_tiles + 1, 1, tm), blk_e, n_used.reshape(1)


def _pick(m, pref):
    t = pref
    while m % t:
        t //= 2
    return t


def kernel(x, c, ctx, c_ctx, ada_w, ada_b, norm_mix_g, norm_ffn_g, ab_w_in, sgu_norm_g, sgu_w, sgu_b,
           gqa_q_norm_g, gqa_k_norm_g, ab_w_out, mla_w_in, mla_q_norm_g, mla_kv_norm_g, mla_w_uq, mla_w_ukv,
           mla_w_out, router_w, router_b, moe_w1, moe_b1, moe_w2, moe_b2, final_norm_g):
    bn, s, d = x.shape
    l = ctx.shape[1]
    depth = ada_w.shape[0]
    m_lat, m_ctx = bn * s, bn * l
    tm_lat = _pick(s, 512)
    tm_ctx = _pick(l, 512)
    tm_moe = 512
    tq_comb = _pick(l, 256)

    n_cond = (bn + 1 + 7) // 8 * 8
    cond = jnp.concatenate([c, c_ctx[None, :], jnp.zeros((n_cond - bn - 1, d), F32)], axis=0)
    mod_all = _ada(cond, ada_w, ada_b)

    rope_gqa = _rope_tables(s, HEAD_DIM)
    rope_mla = _rope_tables(s, MLA_ROPE)

    xl = x.reshape(m_lat, d)
    xc = ctx.reshape(m_ctx, d)
    for i in range(depth):
        ctx_out = i < depth - 1
        j = i // 2
        mod_lat = mod_all[i, :bn].reshape(bn * 6, 1, d)
        mod_ctx = mod_all[i, bn].reshape(6, 1, d)
        ng = norm_mix_g[i].reshape(1, d)
        if i % 2 == 0:
            w_in = ab_w_in[j].astype(BF16)
            w_out = ab_w_out[j].astype(BF16)
            sgu_g = sgu_norm_g[j].reshape(1, SGU_WIDTH)
            sgu_wb = sgu_w[j].astype(BF16)
            sgu_bfull = jnp.broadcast_to(sgu_b[j][:, :, None], (SGU_GROUPS, CHUNK, SGU_GROUP_CH))
            q_g = gqa_q_norm_g[j].reshape(1, HEAD_DIM)
            k_g = gqa_k_norm_g[j].reshape(1, HEAD_DIM)
            oa, q, k, v = _proj_ab(xl, mod_lat, (1, 0), s, ng, w_in, sgu_g, sgu_wb, sgu_bfull, q_g, k_g,
                                   rope_gqa, tm_lat)
            oac, qc, kc, vc = _proj_ab(xc, mod_ctx, (1, 0), m_ctx, ng, w_in, sgu_g, sgu_wb, sgu_bfull, q_g, k_g,
                                       None, tm_ctx)
            ob = _attention(q, [(kc, vc), (k, v)], bn, GQA_HEADS, GQA_KV_HEADS, HEAD_DIM, HEAD_DIM, ATTN_TQ)
            ins = [(oa, w_out[:SGU_WIDTH]), (ob, w_out[SGU_WIDTH:])]
            if ctx_out:
                obc = _attention(qc, [(kc, vc)], bn, GQA_HEADS, GQA_KV_HEADS, HEAD_DIM, HEAD_DIM, ATTN_TQ)
                ins_c = [(oac, w_out[:SGU_WIDTH]), (obc, w_out[SGU_WIDTH:])]
        else:
            w_in_f = mla_w_in[j]
            w_in = jnp.pad(w_in_f, ((0, 0), (0, MLA_IN_PAD - w_in_f.shape[1]))).astype(BF16)
            w_in_kv = w_in[:, MLA_Q_LORA:]
            w_out = mla_w_out[j].astype(BF16)
            q_g = mla_q_norm_g[j].reshape(1, MLA_Q_LORA)
            kv_g = mla_kv_norm_g[j].reshape(1, MLA_KV_LORA)
            w_uq = mla_w_uq[j].reshape(MLA_Q_LORA, MLA_HEADS, MLA_QK)
            w_uq = jnp.pad(w_uq, ((0, 0), (0, 0), (0, MLA_QK_PAD - MLA_QK)))
            w_uq = w_uq.reshape(MLA_Q_LORA, MLA_HEADS * MLA_QK_PAD).astype(BF16)
            w_ukv = mla_w_ukv[j].reshape(MLA_KV_LORA, MLA_HEADS, MLA_NOPE + MLA_V)
            w_uk = w_ukv[:, :, :MLA_NOPE].reshape(MLA_KV_LORA, MLA_HEADS * MLA_NOPE).astype(BF16)
            w_uv = w_ukv[:, :, MLA_NOPE:].reshape(MLA_KV_LORA, MLA_HEADS * MLA_V).astype(BF16)
            q, k, v = _proj_mla(xl, mod_lat, (1, 0), s, ng, w_in, q_g, w_uq, kv_g, w_uk, w_uv, rope_mla, True,
                                tm_lat)
            if ctx_out:
                qc, kc, vc = _proj_mla(xc, mod_ctx, (1, 0), m_ctx, ng, w_in, q_g, w_uq, kv_g, w_uk, w_uv, None,
                                       True, tm_ctx)
            else:
                kc, vc = _proj_mla(xc, mod_ctx, (1, 0), m_ctx, ng, w_in_kv, None, None, kv_g, w_uk, w_uv, None,
                                   False, tm_ctx)
            ob = _attention(q, [(kc, vc), (k, v)], bn, MLA_HEADS, MLA_HEADS, MLA_QK_PAD, MLA_V, ATTN_TQ)
            ins = [(ob, w_out)]
            if ctx_out:
                obc = _attention(qc, [(kc, vc)], bn, MLA_HEADS, MLA_HEADS, MLA_QK_PAD, MLA_V, ATTN_TQ)
                ins_c = [(obc, w_out)]

        nfg = norm_ffn_g[i].reshape(1, d)
        rwt = router_w[i].T
        rb = router_b[i].reshape(N_EXPERTS, 1)
        cnt0 = jnp.zeros((N_EXPERTS, LANES), F32)
        xl, h2, idx, gate, cnt = _out_router(ins, xl, mod_lat, (2, 4, 3), s, nfg, rwt, rb, cnt0, tm_lat)
        if ctx_out:
            xc, h2c, idx_c, gate_c, cnt = _out_router(ins_c, xc, mod_ctx, (2, 4, 3), m_ctx, nfg, rwt, rb, cnt, tm_ctx)
            h2 = jnp.concatenate([h2, h2c], axis=0)
            idx = jnp.concatenate([idx, idx_c], axis=1)
        t_tot = h2.shape[0]
        n_assign = t_tot * TOP_K
        row_tok, row_dst, blk_e, n_used = _route(idx, cnt[:, 0], tm_moe)
        ys = _moe(h2, row_tok, row_dst, blk_e, n_used, moe_w1[i].astype(BF16), moe_b1[i].reshape(N_EXPERTS, 1, -1),
                  moe_w2[i].astype(BF16), moe_b2[i].reshape(N_EXPERTS, 1, -1), n_assign + tm_moe, tm_moe)
        last = i == depth - 1
        fg = final_norm_g.reshape(1, d) if last else None
        xl = _combine(ys, t_tot, 0, gate, xl, mod_lat, 5, s, fg, tq_comb)
        if ctx_out:
            xc = _combine(ys, t_tot, m_lat, gate_c, xc, mod_ctx, 5, m_ctx, None, tq_comb)
    return xl.reshape(bn, s, d)
```

```python
import functools

import jax
import jax.numpy as jnp
import numpy as np
from jax import lax
from jax.experimental import pallas as pl
from jax.experimental.pallas import tpu as pltpu

F32 = jnp.float32
BF16 = jnp.bfloat16

GRID_W = 64
EPS = 1e-6
ROPE_THETA = 10000.0
CHUNK = 128
SGU_GROUPS = 4
SGU_GROUP_CH = 128
SGU_WIDTH = SGU_GROUPS * SGU_GROUP_CH
GQA_HEADS = 4
GQA_KV_HEADS = 2
HEAD_DIM = 128
GQA_Q_W = GQA_HEADS * HEAD_DIM
GQA_KV_W = GQA_KV_HEADS * HEAD_DIM
A_END = 2 * SGU_WIDTH
K_OFF = A_END + GQA_Q_W
V_OFF = K_OFF + GQA_KV_W
MLA_HEADS = 8
MLA_Q_LORA = 512
MLA_KV_LORA = 256
MLA_NOPE = 128
MLA_ROPE = 64
MLA_V = 128
MLA_QK = MLA_NOPE + MLA_ROPE
MLA_QK_PAD = 256
MLA_IN_PAD = 896
N_EXPERTS = 32
TOP_K = 4
D_EXPERT = 1024
SWIGLU_LIMIT = 7.0
SWIGLU_ALPHA = 1.702

LANES = 128
SUBLANES = 8
ATTN_TQ = 256
VMEM_LIMIT = 56 * 1024 * 1024


def _cparams(*sem):
    return pltpu.CompilerParams(dimension_semantics=sem, vmem_limit_bytes=VMEM_LIMIT)


def _dot(a, b):
    return jnp.dot(a, b, preferred_element_type=F32)


def _dot_nt(a, b):
    return lax.dot_general(a, b, (((1,), (1,)), ((), ())), preferred_element_type=F32)


def _split2(a):
    hi = a.astype(BF16)
    lo = (a - hi.astype(F32)).astype(BF16)
    return hi, lo


def _rms(x, g):
    return x * lax.rsqrt(jnp.mean(x * x, axis=-1, keepdims=True) + EPS) * g


def _store_token_tiles(ref, val):
    n = val.shape[0]
    for s in range(SUBLANES):
        ref[pl.ds(s, n, stride=SUBLANES), :] = val[:, s * LANES:(s + 1) * LANES]


def _load_token_tiles(ref):
    n = ref.shape[0] // SUBLANES
    return jnp.concatenate([ref[pl.ds(s, n, stride=SUBLANES), :] for s in range(SUBLANES)], axis=1)


def _rope(x, c, s_a, s_b, half):
    n = x.shape[-1]
    return x * c + pltpu.roll(x, n - half, 1) * s_a + pltpu.roll(x, half, 1) * s_b


def _ada_kernel(c_ref, w_ref, b_ref, o_ref):
    c = c_ref[...]
    sc = c * jax.nn.sigmoid(c)
    c_hi, c_lo = _split2(sc)
    w = w_ref[0]
    w_hi, w_lo = _split2(w)
    acc = _dot(c_hi, w_hi) + _dot(c_hi, w_lo) + _dot(c_lo, w_hi)
    o_ref[0] = acc + b_ref[0]


def _ada(cond, ada_w, ada_b):
    depth, d, n = ada_w.shape
    rows = cond.shape[0]
    tn = 1536
    return pl.pallas_call(
        _ada_kernel,
        out_shape=jax.ShapeDtypeStruct((depth, rows, n), F32),
        grid=(depth, n // tn),
        in_specs=[
            pl.BlockSpec((rows, d), lambda l, j: (0, 0)),
            pl.BlockSpec((1, d, tn), lambda l, j: (l, 0, j)),
            pl.BlockSpec((1, 1, tn), lambda l, j: (l, 0, j)),
        ],
        out_specs=pl.BlockSpec((1, rows, tn), lambda l, j: (l, 0, j)),
        compiler_params=_cparams("arbitrary", "arbitrary"),
        name="ada_mod",
    )(cond, ada_w, ada_b.reshape(depth, 1, n))


def _proj_ab_kernel(*refs, use_rope, tm):
    if use_rope:
        (x_ref, sc_ref, sh_ref, g_ref, w_ref, sgug_ref, sguw_ref, sgub_ref, qg_ref, kg_ref,
         c_ref, sa_ref, sb_ref, oa_ref, q_ref, k_ref, v_ref) = refs
    else:
        (x_ref, sc_ref, sh_ref, g_ref, w_ref, sgug_ref, sguw_ref, sgub_ref, qg_ref, kg_ref,
         oa_ref, q_ref, k_ref, v_ref) = refs
    x = x_ref[...]
    h = _rms(x, g_ref[...]) * (1.0 + sc_ref[0]) + sh_ref[0]
    p = _dot(h.astype(BF16), w_ref[...])

    n_chunks = tm // CHUNK
    for g in range(SGU_GROUPS):
        u = jax.nn.gelu(p[:, g * SGU_GROUP_CH:(g + 1) * SGU_GROUP_CH])
        v = jax.nn.gelu(p[:, SGU_WIDTH + g * SGU_GROUP_CH:SGU_WIDTH + (g + 1) * SGU_GROUP_CH])
        v = _rms(v, sgug_ref[:, g * SGU_GROUP_CH:(g + 1) * SGU_GROUP_CH]).astype(BF16)
        vcat = jnp.concatenate([v[c * CHUNK:(c + 1) * CHUNK, :] for c in range(n_chunks)], axis=1)
        mixed = _dot(sguw_ref[g], vcat)
        bias = sgub_ref[g]
        for c in range(n_chunks):
            gate = mixed[:, c * SGU_GROUP_CH:(c + 1) * SGU_GROUP_CH] + bias
            oa_ref[c * CHUNK:(c + 1) * CHUNK, g * SGU_GROUP_CH:(g + 1) * SGU_GROUP_CH] = (
                u[c * CHUNK:(c + 1) * CHUNK, :] * gate).astype(oa_ref.dtype)

    qg = qg_ref[...] * (HEAD_DIM ** -0.5)
    for hd in range(GQA_HEADS):
        qh = _rms(p[:, A_END + hd * HEAD_DIM:A_END + (hd + 1) * HEAD_DIM], qg)
        if use_rope:
            qh = _rope(qh, c_ref[...], sa_ref[...], sb_ref[...], HEAD_DIM // 4)
        q_ref[:, hd * HEAD_DIM:(hd + 1) * HEAD_DIM] = qh.astype(q_ref.dtype)
    for hd in range(GQA_KV_HEADS):
        kh = _rms(p[:, K_OFF + hd * HEAD_DIM:K_OFF + (hd + 1) * HEAD_DIM], kg_ref[...])
        if use_rope:
            kh = _rope(kh, c_ref[...], sa_ref[...], sb_ref[...], HEAD_DIM // 4)
        k_ref[:, hd * HEAD_DIM:(hd + 1) * HEAD_DIM] = kh.astype(k_ref.dtype)
    v_ref[...] = p[:, V_OFF:].astype(v_ref.dtype)


def _proj_ab(x, mod, mod_rows, group_rows, norm_g, w_in, sgu_g, sgu_w, sgu_bfull, q_g, k_g, rope, tm):
    m, d = x.shape
    per_group = group_rows // tm
    sc_j, sh_j = mod_rows
    use_rope = rope is not None
    in_specs = [
        pl.BlockSpec((tm, d), lambda i: (i, 0)),
        pl.BlockSpec((1, 1, d), lambda i: ((i // per_group) * 6 + sc_j, 0, 0)),
        pl.BlockSpec((1, 1, d), lambda i: ((i // per_group) * 6 + sh_j, 0, 0)),
        pl.BlockSpec((1, d), lambda i: (0, 0)),
        pl.BlockSpec(w_in.shape, lambda i: (0, 0)),
        pl.BlockSpec((1, SGU_WIDTH), lambda i: (0, 0)),
        pl.BlockSpec(sgu_w.shape, lambda i: (0, 0, 0)),
        pl.BlockSpec(sgu_bfull.shape, lambda i: (0, 0, 0)),
        pl.BlockSpec((1, HEAD_DIM), lambda i: (0, 0)),
        pl.BlockSpec((1, HEAD_DIM), lambda i: (0, 0)),
    ]
    args = [x, mod, mod, norm_g, w_in, sgu_g, sgu_w, sgu_bfull, q_g, k_g]
    if use_rope:
        s_tiles = rope[0].shape[0] // tm
        for t in rope:
            in_specs.append(pl.BlockSpec((tm, HEAD_DIM), lambda i: (i % s_tiles, 0)))
            args.append(t)
    return pl.pallas_call(
        functools.partial(_proj_ab_kernel, use_rope=use_rope, tm=tm),
        out_shape=(jax.ShapeDtypeStruct((m, SGU_WIDTH), BF16),
                   jax.ShapeDtypeStruct((m, GQA_Q_W), BF16),
                   jax.ShapeDtypeStruct((m, GQA_KV_W), BF16),
                   jax.ShapeDtypeStruct((m, GQA_KV_W), BF16)),
        grid=(m // tm,),
        in_specs=in_specs,
        out_specs=(pl.BlockSpec((tm, SGU_WIDTH), lambda i: (i, 0)),
                   pl.BlockSpec((tm, GQA_Q_W), lambda i: (i, 0)),
                   pl.BlockSpec((tm, GQA_KV_W), lambda i: (i, 0)),
                   pl.BlockSpec((tm, GQA_KV_W), lambda i: (i, 0))),
        compiler_params=_cparams("parallel"),
        name="proj_ab",
    )(*args)


def _proj_mla_kernel(*refs, with_q, use_rope):
    it = iter(refs)
    x_ref, sc_ref, sh_ref, g_ref, w_ref = (next(it) for _ in range(5))
    if with_q:
        qg_ref, wuq_ref = next(it), next(it)
    kvg_ref, wuk_ref, wuv_ref = next(it), next(it), next(it)
    if use_rope:
        c_ref, sa_ref, sb_ref = next(it), next(it), next(it)
    if with_q:
        q_ref = next(it)
    k_ref, v_ref = next(it), next(it)

    x = x_ref[...]
    h = _rms(x, g_ref[...]) * (1.0 + sc_ref[0]) + sh_ref[0]
    p = _dot(h.astype(BF16), w_ref[...])
    off = MLA_Q_LORA if with_q else 0
    ckv = _rms(p[:, off:off + MLA_KV_LORA], kvg_ref[...]).astype(BF16)
    kr = p[:, off + MLA_KV_LORA:off + MLA_KV_LORA + LANES]
    if use_rope:
        kr = _rope(kr, c_ref[...], sa_ref[...], sb_ref[...], MLA_ROPE // 4)
    kr = kr.astype(k_ref.dtype)
    k_nope = _dot(ckv, wuk_ref[...])
    v_ref[...] = _dot(ckv, wuv_ref[...]).astype(v_ref.dtype)
    for hd in range(MLA_HEADS):
        k_ref[:, hd * MLA_QK_PAD:hd * MLA_QK_PAD + MLA_NOPE] = (
            k_nope[:, hd * MLA_NOPE:(hd + 1) * MLA_NOPE].astype(k_ref.dtype))
        k_ref[:, hd * MLA_QK_PAD + MLA_NOPE:(hd + 1) * MLA_QK_PAD] = kr
    if with_q:
        cq = _rms(p[:, :MLA_Q_LORA], qg_ref[...]).astype(BF16)
        q = _dot(cq, wuq_ref[...]) * (MLA_QK ** -0.5)
        for hd in range(MLA_HEADS):
            q_ref[:, hd * MLA_QK_PAD:hd * MLA_QK_PAD + MLA_NOPE] = (
                q[:, hd * MLA_QK_PAD:hd * MLA_QK_PAD + MLA_NOPE].astype(q_ref.dtype))
            qr = q[:, hd * MLA_QK_PAD + MLA_NOPE:(hd + 1) * MLA_QK_PAD]
            if use_rope:
                qr = _rope(qr, c_ref[...], sa_ref[...], sb_ref[...], MLA_ROPE // 4)
            q_ref[:, hd * MLA_QK_PAD + MLA_NOPE:(hd + 1) * MLA_QK_PAD] = qr.astype(q_ref.dtype)


def _proj_mla(x, mod, mod_rows, group_rows, norm_g, w_in, q_g, w_uq, kv_g, w_uk, w_uv, rope, with_q, tm):
    m, d = x.shape
    per_group = group_rows // tm
    sc_j, sh_j = mod_rows
    use_rope = rope is not None
    const2 = lambda i: (0, 0)
    in_specs = [
        pl.BlockSpec((tm, d), lambda i: (i, 0)),
        pl.BlockSpec((1, 1, d), lambda i: ((i // per_group) * 6 + sc_j, 0, 0)),
        pl.BlockSpec((1, 1, d), lambda i: ((i // per_group) * 6 + sh_j, 0, 0)),
        pl.BlockSpec((1, d), const2),
        pl.BlockSpec(w_in.shape, const2),
    ]
    args = [x, mod, mod, norm_g, w_in]
    if with_q:
        in_specs += [pl.BlockSpec(q_g.shape, const2), pl.BlockSpec(w_uq.shape, const2)]
        args += [q_g, w_uq]
    in_specs += [pl.BlockSpec(kv_g.shape, const2), pl.BlockSpec(w_uk.shape, const2),
                 pl.BlockSpec(w_uv.shape, const2)]
    args += [kv_g, w_uk, w_uv]
    if use_rope:
        s_tiles = rope[0].shape[0] // tm
        for t in rope:
            in_specs.append(pl.BlockSpec((tm, LANES), lambda i: (i % s_tiles, 0)))
            args.append(t)
    kw = MLA_HEADS * MLA_QK_PAD
    vw = MLA_HEADS * MLA_V
    out_shape = [jax.ShapeDtypeStruct((m, kw), BF16), jax.ShapeDtypeStruct((m, vw), BF16)]
    out_specs = [pl.BlockSpec((tm, kw), lambda i: (i, 0)), pl.BlockSpec((tm, vw), lambda i: (i, 0))]
    if with_q:
        out_shape = [jax.ShapeDtypeStruct((m, kw), BF16)] + out_shape
        out_specs = [pl.BlockSpec((tm, kw), lambda i: (i, 0))] + out_specs
    return pl.pallas_call(
        functools.partial(_proj_mla_kernel, with_q=with_q, use_rope=use_rope),
        out_shape=tuple(out_shape),
        grid=(m // tm,),
        in_specs=in_specs,
        out_specs=tuple(out_specs),
        compiler_params=_cparams("parallel"),
        name="proj_mla",
    )(*args)


def _attn_kernel(*refs, n_seg, tq, n_q):
    q_ref = refs[0]
    kv = refs[1:1 + 2 * n_seg]
    o_ref = refs[1 + 2 * n_seg]

    for c in range(n_q):
        q = q_ref[c * tq:(c + 1) * tq, :]
        ss = [_dot_nt(q, kv[2 * j][...]) for j in range(n_seg)]
        mx = ss[0].max(axis=-1, keepdims=True)
        for s in ss[1:]:
            mx = jnp.maximum(mx, s.max(axis=-1, keepdims=True))
        acc = None
        den = None
        for j, s in enumerate(ss):
            e = jnp.exp(s - mx)
            d = e.sum(axis=-1, keepdims=True)
            o = _dot(e.astype(BF16), kv[2 * j + 1][...])
            acc = o if acc is None else acc + o
            den = d if den is None else den + d
        o_ref[c * tq:(c + 1) * tq, :] = (acc / den).astype(o_ref.dtype)


def _attention(q, segs, batch, heads, kv_heads, dk, dv, tq):
    t_q = q.shape[0] // batch
    group = heads // kv_heads
    in_specs = [pl.BlockSpec((t_q, dk), lambda b, h: (b, h))]
    args = [q]
    for k, v in segs:
        t_k = k.shape[0] // batch
        in_specs.append(pl.BlockSpec((t_k, dk), lambda b, h: (b, h // group)))
        in_specs.append(pl.BlockSpec((t_k, dv), lambda b, h: (b, h // group)))
        args += [k, v]
    tq = min(tq, t_q)
    return pl.pallas_call(
        functools.partial(_attn_kernel, n_seg=len(segs), tq=tq, n_q=t_q // tq),
        out_shape=jax.ShapeDtypeStruct((q.shape[0], heads * dv), BF16),
        grid=(batch, heads),
        in_specs=in_specs,
        out_specs=pl.BlockSpec((t_q, dv), lambda b, h: (b, h)),
        compiler_params=_cparams("parallel", "parallel"),
        name="attention",
    )(*args)


def _out_router_kernel(*refs, n_in, tm):
    ins = refs[:2 * n_in]
    (x_ref, g1_ref, sc_ref, sh_ref, ng_ref, rw_ref, rb_ref, cnt0_ref,
     xo_ref, h2_ref, idx_ref, gate_ref, cnt_ref) = refs[2 * n_in:]
    i = pl.program_id(0)

    @pl.when(i == 0)
    def _():
        cnt_ref[...] = cnt0_ref[...]

    y = None
    for j in range(n_in):
        t = _dot(ins[2 * j][...], ins[2 * j + 1][...])
        y = t if y is None else y + t
    xn = x_ref[...] + g1_ref[0] * y
    xo_ref[...] = xn
    h2 = _rms(xn, ng_ref[...]) * (1.0 + sc_ref[0]) + sh_ref[0]
    _store_token_tiles(h2_ref, h2)

    h_hi, h_lo = _split2(h2)
    w_hi, w_lo = _split2(rw_ref[...])
    logits = _dot_nt(w_hi, h_hi) + _dot_nt(w_hi, h_lo) + _dot_nt(w_lo, h_hi) + rb_ref[...]

    e_iota = lax.broadcasted_iota(jnp.int32, logits.shape, 0)
    work = logits
    tops, idxs = [], []
    for _ in range(TOP_K):
        mx = work.max(axis=0, keepdims=True)
        ix = jnp.where(work == mx, e_iota, N_EXPERTS).min(axis=0, keepdims=True)
        tops.append(mx)
        idxs.append(ix)
        work = jnp.where(e_iota == ix, -jnp.inf, work)
    exps = [jnp.exp(t - tops[0]) for t in tops]
    den = exps[0] + exps[1] + exps[2] + exps[3]

    for k in range(TOP_K):
        idx_ref[k:k + 1, :] = idxs[k]
        gate_ref[k:k + 1, :] = exps[k] / den
    cnt = sum((e_iota == ix).astype(F32) for ix in idxs)
    cnt_ref[...] = cnt_ref[...] + cnt.sum(axis=1, keepdims=True)


def _out_router(ins, x, mod, mod_rows, group_rows, norm_g, router_wt, router_b, cnt0, tm):
    m, d = x.shape
    per_group = group_rows // tm
    g1_j, sc_j, sh_j = mod_rows
    in_specs, args = [], []
    for a, w in ins:
        in_specs += [pl.BlockSpec((tm, a.shape[1]), lambda i: (i, 0)), pl.BlockSpec(w.shape, lambda i: (0, 0))]
        args += [a, w]
    mod_spec = lambda j: pl.BlockSpec((1, 1, d), lambda i: ((i // per_group) * 6 + j, 0, 0))
    in_specs += [
        pl.BlockSpec((tm, d), lambda i: (i, 0)),
        mod_spec(g1_j), mod_spec(sc_j), mod_spec(sh_j),
        pl.BlockSpec((1, d), lambda i: (0, 0)),
        pl.BlockSpec(router_wt.shape, lambda i: (0, 0)),
        pl.BlockSpec(router_b.shape, lambda i: (0, 0)),
        pl.BlockSpec(cnt0.shape, lambda i: (0, 0)),
    ]
    args += [x, mod, mod, mod, norm_g, router_wt, router_b, cnt0]
    small = lambda dt: jax.ShapeDtypeStruct((TOP_K, m), dt)
    small_spec = pl.BlockSpec((TOP_K, tm), lambda i: (0, i))
    return pl.pallas_call(
        functools.partial(_out_router_kernel, n_in=len(ins), tm=tm),
        out_shape=(jax.ShapeDtypeStruct((m, d), F32), jax.ShapeDtypeStruct((m * SUBLANES, d // SUBLANES), F32),
                   small(jnp.int32), small(F32),
                   jax.ShapeDtypeStruct(cnt0.shape, F32)),
        grid=(m // tm,),
        in_specs=in_specs,
        out_specs=(pl.BlockSpec((tm, d), lambda i: (i, 0)),
                   pl.BlockSpec((tm * SUBLANES, d // SUBLANES), lambda i: (i, 0)),
                   small_spec, small_spec,
                   pl.BlockSpec(cnt0.shape, lambda i: (0, 0))),
        compiler_params=_cparams("arbitrary"),
        name="out_router",
    )(*args)


def _moe_kernel(blk_e_ref, nused_ref, tok0_ref, tokn_ref, dstp_ref, dstc_ref, h_hbm, w1_ref, b1_ref, w2_ref, b2_ref,
                o_hbm, x0, x1, y0, y1, gsem, ssem, *, tm):
    i = pl.program_id(0)
    n_used = nused_ref[0]
    xs, ys = (x0, x1), (y0, y1)

    def tile_at(ref, row):
        return ref.at[pl.ds(pl.multiple_of(row, SUBLANES), SUBLANES)]

    def gather(tok_ref, xbuf, sem):
        for r in range(tm):
            pltpu.make_async_copy(tile_at(h_hbm, tok_ref[0, 0, r]), xbuf.at[pl.ds(r * SUBLANES, SUBLANES)], sem).start()

    def scatter(dst_ref, ybuf, sem):
        for r in range(tm):
            pltpu.make_async_copy(ybuf.at[pl.ds(r * SUBLANES, SUBLANES)], tile_at(o_hbm, dst_ref[0, 0, r]),
                                  sem).start(priority=1)

    def wait_gather(xbuf, sem):
        pltpu.make_async_copy(h_hbm.at[pl.ds(0, tm * SUBLANES)], xbuf, sem).wait()

    def wait_scatter(ybuf, sem):
        pltpu.make_async_copy(ybuf, o_hbm.at[pl.ds(0, tm * SUBLANES)], sem).wait()

    @pl.when(i == 0)
    def _():
        gather(tok0_ref, x0, gsem.at[0])
        y1[...] = jnp.zeros_like(y1)

    def step(p):
        xb, xn, yb, yp = xs[p], xs[1 - p], ys[p], ys[1 - p]
        wait_gather(xb, gsem.at[p])

        @pl.when(i >= 1)
        def _():
            wait_scatter(yb, ssem.at[p])

        @pl.when(n_used > 0)
        def _():
            scatter(dstp_ref, yp, ssem.at[1 - p])
            gather(tokn_ref, xn, gsem.at[1 - p])

        x = _load_token_tiles(xb).astype(BF16)
        hid = _dot(x, w1_ref[0]) + b1_ref[0]
        f = hid.shape[1] // 2
        g = jnp.minimum(hid[:, :f], SWIGLU_LIMIT)
        lin = jnp.clip(hid[:, f:], -SWIGLU_LIMIT, SWIGLU_LIMIT)
        act = g * jax.nn.sigmoid(SWIGLU_ALPHA * g) * (lin + 1.0)
        _store_token_tiles(yb, _dot(act.astype(BF16), w2_ref[0]) + b2_ref[0])

        @pl.when(i == n_used - 1)
        def _():
            wait_scatter(yp, ssem.at[1 - p])
            scatter(dstc_ref, yb, ssem.at[p])
            wait_scatter(yb, ssem.at[p])
            wait_gather(xn, gsem.at[1 - p])

    for p in range(2):
        @pl.when(jnp.logical_and(i < n_used, i % 2 == p))
        def _(p=p):
            step(p)


def _moe(h2, row_tok, row_dst, blk_e, n_used, w1, b1, w2, b2, out_rows, tm):
    n_tiles = row_tok.shape[0]
    d = w1.shape[1]
    f2 = w1.shape[2]
    tile_rows = (tm * SUBLANES, d // SUBLANES)
    smem = lambda fn: pl.BlockSpec((1, 1, tm), fn, memory_space=pltpu.SMEM)
    grid_spec = pltpu.PrefetchScalarGridSpec(
        num_scalar_prefetch=2,
        grid=(n_tiles,),
        in_specs=[
            smem(lambda i, be, nu: (0, 0, 0)),
            smem(lambda i, be, nu: (jnp.minimum(i + 1, nu[0] - 1), 0, 0)),
            smem(lambda i, be, nu: (i, 0, 0)),
            smem(lambda i, be, nu: (i + 1, 0, 0)),
            pl.BlockSpec(memory_space=pl.ANY),
            pl.BlockSpec((1, d, f2), lambda i, be, nu: (be[i], 0, 0)),
            pl.BlockSpec((1, 1, f2), lambda i, be, nu: (be[i], 0, 0)),
            pl.BlockSpec((1, f2 // 2, d), lambda i, be, nu: (be[i], 0, 0)),
            pl.BlockSpec((1, 1, d), lambda i, be, nu: (be[i], 0, 0)),
        ],
        out_specs=pl.BlockSpec(memory_space=pl.ANY),
        scratch_shapes=[pltpu.VMEM(tile_rows, F32), pltpu.VMEM(tile_rows, F32),
                        pltpu.VMEM(tile_rows, F32), pltpu.VMEM(tile_rows, F32),
                        pltpu.SemaphoreType.DMA((2,)), pltpu.SemaphoreType.DMA((2,))],
    )
    return pl.pallas_call(
        functools.partial(_moe_kernel, tm=tm),
        out_shape=jax.ShapeDtypeStruct((out_rows * SUBLANES, d // SUBLANES), F32),
        grid_spec=grid_spec,
        compiler_params=_cparams("arbitrary"),
        name="moe_experts",
    )(blk_e, n_used, row_tok, row_tok, row_dst, row_dst, h2, w1, b1, w2, b2)


def _combine_kernel(*refs, tq, final):
    ys = refs[:TOP_K]
    if final:
        gate_ref, x_ref, g2_ref, fg_ref, o_ref = refs[TOP_K:]
    else:
        gate_ref, x_ref, g2_ref, o_ref = refs[TOP_K:]
    f = None
    d = x_ref.shape[1]
    for k in range(TOP_K):
        gcol = jnp.broadcast_to(gate_ref[k:k + 1, :], (LANES, tq)).T
        t = _load_token_tiles(ys[k]) * jnp.tile(gcol, (1, d // LANES))
        f = t if f is None else f + t
    xn = x_ref[...] + g2_ref[0] * f
    if final:
        xn = _rms(xn, fg_ref[...])
    o_ref[...] = xn


def _combine(ys, t_tot, t_off, gate, x, mod, g2_j, group_rows, final_g, tq):
    m, d = x.shape
    per_group = group_rows // tq
    final = final_g is not None
    in_specs = [pl.BlockSpec((tq * SUBLANES, d // SUBLANES),
                             functools.partial(lambda i, k: ((k * t_tot + t_off) // tq + i, 0), k=k))
                for k in range(TOP_K)]
    in_specs += [
        pl.BlockSpec((TOP_K, tq), lambda i: (0, i)),
        pl.BlockSpec((tq, d), lambda i: (i, 0)),
        pl.BlockSpec((1, 1, d), lambda i: ((i // per_group) * 6 + g2_j, 0, 0)),
    ]
    args = [ys] * TOP_K + [gate, x, mod]
    if final:
        in_specs.append(pl.BlockSpec((1, d), lambda i: (0, 0)))
        args.append(final_g)
    return pl.pallas_call(
        functools.partial(_combine_kernel, tq=tq, final=final),
        out_shape=jax.ShapeDtypeStruct((m, d), F32),
        grid=(m // tq,),
        in_specs=in_specs,
        out_specs=pl.BlockSpec((tq, d), lambda i: (i, 0)),
        compiler_params=_cparams("parallel"),
        name="moe_combine",
    )(*args)


def _rope_tables(seq, rot_dim):
    t = np.arange(seq)
    row = (t // GRID_W).astype(np.float32)
    col = (t % GRID_W).astype(np.float32)
    axis_dim = rot_dim // 2
    quarter = axis_dim // 2
    inv = jnp.asarray(ROPE_THETA, F32) ** (-jnp.arange(0, axis_dim, 2, dtype=F32) / axis_dim)
    ang_r = jnp.asarray(row)[:, None] * inv
    ang_c = jnp.asarray(col)[:, None] * inv
    zeros = jnp.zeros_like(ang_r)
    cos = jnp.concatenate([jnp.cos(ang_r)] * 2 + [jnp.cos(ang_c)] * 2, axis=1)
    s_a = jnp.concatenate([-jnp.sin(ang_r), zeros, -jnp.sin(ang_c), zeros], axis=1)
    s_b = jnp.concatenate([zeros, jnp.sin(ang_r), zeros, jnp.sin(ang_c)], axis=1)
    pad = LANES - rot_dim
    if pad:
        cos, s_a, s_b = (jnp.pad(a, ((0, 0), (0, pad))) for a in (cos, s_a, s_b))
    del quarter
    return cos, s_a, s_b


def _route(idx, counts, tm):
    t_tot = idx.shape[1]
    n_assign = t_tot * TOP_K
    n_rows = n_assign + N_EXPERTS * tm
    n_tiles = n_rows // tm
    counts = counts.astype(jnp.int32)
    pad_need = (-counts) % tm
    flat_e = idx.T.reshape(-1)
    slot = jnp.arange(tm, dtype=jnp.int32)[None, :]
    pad_keys = jnp.where(slot < pad_need[:, None], jnp.arange(N_EXPERTS, dtype=jnp.int32)[:, None], N_EXPERTS)
    keys = jnp.concatenate([flat_e, pad_keys.reshape(-1)])
    vals = jnp.concatenate([jnp.arange(n_assign, dtype=jnp.int32), jnp.full((N_EXPERTS * tm,), -1, jnp.int32)])
    skeys, svals = lax.sort((keys, vals), num_keys=1, is_stable=True)
    real = svals >= 0
    row_tok = jnp.where(real, svals // TOP_K, 0)
    rows = jnp.arange(n_rows, dtype=jnp.int32)
    row_dst = jnp.where(real, (svals % TOP_K) * t_tot + svals // TOP_K, n_assign + rows % tm)
    tile_e = skeys[::tm]
    n_used = jnp.sum(tile_e < N_EXPERTS).astype(jnp.int32)
    last_e = tile_e[jnp.maximum(n_used - 1, 0)]
    blk_e = jnp.where(tile_e < N_EXPERTS, tile_e, last_e).astype(jnp.int32)
    row_dst = jnp.concatenate([n_assign + jnp.arange(tm, dtype=jnp.int32), row_dst])
    row_tok = (row_tok * SUBLANES).reshape(n_tiles, 1, tm)
    row_dst = (row_dst * SUBLANES).reshape(n_tiles + 1, 1, tm)
    return row_tok, row_dst, blk_e, n_used.reshape(1)


def _pick(m, pref):
    t = pref
    while m % t:
        t //= 2
    return t


def kernel(x, c, ctx, c_ctx, ada_w, ada_b, norm_mix_g, norm_ffn_g, ab_w_in, sgu_norm_g, sgu_w, sgu_b,
           gqa_q_norm_g, gqa_k_norm_g, ab_w_out, mla_w_in, mla_q_norm_g, mla_kv_norm_g, mla_w_uq, mla_w_ukv,
           mla_w_out, router_w, router_b, moe_w1, moe_b1, moe_w2, moe_b2, final_norm_g):
    bn, s, d = x.shape
    l = ctx.shape[1]
    depth = ada_w.shape[0]
    m_lat, m_ctx = bn * s, bn * l
    tm_lat = _pick(s, 512)
    tm_ctx = _pick(l, 512)
    tm_moe = 512
    tq_comb = _pick(l, 256)

    n_cond = (bn + 1 + 7) // 8 * 8
    cond = jnp.concatenate([c, c_ctx[None, :], jnp.zeros((n_cond - bn - 1, d), F32)], axis=0)
    mod_all = _ada(cond, ada_w, ada_b)

    rope_gqa = _rope_tables(s, HEAD_DIM)
    rope_mla = _rope_tables(s, MLA_ROPE)

    xl = x.reshape(m_lat, d)
    xc = ctx.reshape(m_ctx, d)
    for i in range(depth):
        ctx_out = i < depth - 1
        j = i // 2
        mod_lat = mod_all[i, :bn].reshape(bn * 6, 1, d)
        mod_ctx = mod_all[i, bn].reshape(6, 1, d)
        ng = norm_mix_g[i].reshape(1, d)
        if i % 2 == 0:
            w_in = ab_w_in[j].astype(BF16)
            w_out = ab_w_out[j].astype(BF16)
            sgu_g = sgu_norm_g[j].reshape(1, SGU_WIDTH)
            sgu_wb = sgu_w[j].astype(BF16)
            sgu_bfull = jnp.broadcast_to(sgu_b[j][:, :, None], (SGU_GROUPS, CHUNK, SGU_GROUP_CH))
            q_g = gqa_q_norm_g[j].reshape(1, HEAD_DIM)
            k_g = gqa_k_norm_g[j].reshape(1, HEAD_DIM)
            oa, q, k, v = _proj_ab(xl, mod_lat, (1, 0), s, ng, w_in, sgu_g, sgu_wb, sgu_bfull, q_g, k_g,
                                   rope_gqa, tm_lat)
            oac, qc, kc, vc = _proj_ab(xc, mod_ctx, (1, 0), m_ctx, ng, w_in, sgu_g, sgu_wb, sgu_bfull, q_g, k_g,
                                       None, tm_ctx)
            ob = _attention(q, [(kc, vc), (k, v)], bn, GQA_HEADS, GQA_KV_HEADS, HEAD_DIM, HEAD_DIM, ATTN_TQ)
            ins = [(oa, w_out[:SGU_WIDTH]), (ob, w_out[SGU_WIDTH:])]
            if ctx_out:
                obc = _attention(qc, [(kc, vc)], bn, GQA_HEADS, GQA_KV_HEADS, HEAD_DIM, HEAD_DIM, ATTN_TQ)
                ins_c = [(oac, w_out[:SGU_WIDTH]), (obc, w_out[SGU_WIDTH:])]
        else:
            w_in_f = mla_w_in[j]
            w_in = jnp.pad(w_in_f, ((0, 0), (0, MLA_IN_PAD - w_in_f.shape[1]))).astype(BF16)
            w_in_kv = w_in[:, MLA_Q_LORA:]
            w_out = mla_w_out[j].astype(BF16)
            q_g = mla_q_norm_g[j].reshape(1, MLA_Q_LORA)
            kv_g = mla_kv_norm_g[j].reshape(1, MLA_KV_LORA)
            w_uq = mla_w_uq[j].reshape(MLA_Q_LORA, MLA_HEADS, MLA_QK)
            w_uq = jnp.pad(w_uq, ((0, 0), (0, 0), (0, MLA_QK_PAD - MLA_QK)))
            w_uq = w_uq.reshape(MLA_Q_LORA, MLA_HEADS * MLA_QK_PAD).astype(BF16)
            w_ukv = mla_w_ukv[j].reshape(MLA_KV_LORA, MLA_HEADS, MLA_NOPE + MLA_V)
            w_uk = w_ukv[:, :, :MLA_NOPE].reshape(MLA_KV_LORA, MLA_HEADS * MLA_NOPE).astype(BF16)
            w_uv = w_ukv[:, :, MLA_NOPE:].reshape(MLA_KV_LORA, MLA_HEADS * MLA_V).astype(BF16)
            q, k, v = _proj_mla(xl, mod_lat, (1, 0), s, ng, w_in, q_g, w_uq, kv_g, w_uk, w_uv, rope_mla, True,
                                tm_lat)
            if ctx_out:
                qc, kc, vc = _proj_mla(xc, mod_ctx, (1, 0), m_ctx, ng, w_in, q_g, w_uq, kv_g, w_uk, w_uv, None,
                                       True, tm_ctx)
            else:
                kc, vc = _proj_mla(xc, mod_ctx, (1, 0), m_ctx, ng, w_in_kv, None, None, kv_g, w_uk, w_uv, None,
                                   False, tm_ctx)
            ob = _attention(q, [(kc, vc), (k, v)], bn, MLA_HEADS, MLA_HEADS, MLA_QK_PAD, MLA_V, ATTN_TQ)
            ins = [(ob, w_out)]
            if ctx_out:
                obc = _attention(qc, [(kc, vc)], bn, MLA_HEADS, MLA_HEADS, MLA_QK_PAD, MLA_V, ATTN_TQ)
                ins_c = [(obc, w_out)]

        nfg = norm_ffn_g[i].reshape(1, d)
        rwt = router_w[i].T
        rb = router_b[i].reshape(N_EXPERTS, 1)
        cnt0 = jnp.zeros((N_EXPERTS, LANES), F32)
        xl, h2, idx, gate, cnt = _out_router(ins, xl, mod_lat, (2, 4, 3), s, nfg, rwt, rb, cnt0, tm_lat)
        if ctx_out:
            xc, h2c, idx_c, gate_c, cnt = _out_router(ins_c, xc, mod_ctx, (2, 4, 3), m_ctx, nfg, rwt, rb, cnt, tm_ctx)
            h2 = jnp.concatenate([h2, h2c], axis=0)
            idx = jnp.concatenate([idx, idx_c], axis=1)
        t_tot = h2.shape[0] // SUBLANES
        n_assign = t_tot * TOP_K
        row_tok, row_dst, blk_e, n_used = _route(idx, cnt[:, 0], tm_moe)
        ys = _moe(h2, row_tok, row_dst, blk_e, n_used, moe_w1[i].astype(BF16), moe_b1[i].reshape(N_EXPERTS, 1, -1),
                  moe_w2[i].astype(BF16), moe_b2[i].reshape(N_EXPERTS, 1, -1), n_assign + tm_moe, tm_moe)
        last = i == depth - 1
        fg = final_norm_g.reshape(1, d) if last else None
        xl = _combine(ys, t_tot, 0, gate, xl, mod_lat, 5, s, fg, tq_comb)
        if ctx_out:
            xc = _combine(ys, t_tot, m_lat, gate_c, xc, mod_ctx, 5, m_ctx, None, tq_comb)
    return xl.reshape(bn, s, d)
```

```python
import functools

import jax
import jax.numpy as jnp
import numpy as np
from jax import lax
from jax.experimental import pallas as pl
from jax.experimental.pallas import tpu as pltpu

F32 = jnp.float32
BF16 = jnp.bfloat16
U32 = jnp.uint32

GRID_W = 64
EPS = 1e-6
ROPE_THETA = 10000.0
CHUNK = 128
SGU_GROUPS = 4
SGU_GROUP_CH = 128
SGU_WIDTH = SGU_GROUPS * SGU_GROUP_CH
GQA_HEADS = 4
GQA_KV_HEADS = 2
HEAD_DIM = 128
GQA_Q_W = GQA_HEADS * HEAD_DIM
GQA_KV_W = GQA_KV_HEADS * HEAD_DIM
A_END = 2 * SGU_WIDTH
K_OFF = A_END + GQA_Q_W
V_OFF = K_OFF + GQA_KV_W
MLA_HEADS = 8
MLA_Q_LORA = 512
MLA_KV_LORA = 256
MLA_NOPE = 128
MLA_ROPE = 64
MLA_V = 128
MLA_QK = MLA_NOPE + MLA_ROPE
MLA_QK_PAD = 256
MLA_IN_PAD = 896
N_EXPERTS = 32
TOP_K = 4
D_EXPERT = 1024
SWIGLU_LIMIT = 7.0
SWIGLU_ALPHA = 1.702

LANES = 128
TOKEN_ROWS = 4
DMA_UNROLL = 32
ATTN_TQ = 256
VMEM_LIMIT = 56 * 1024 * 1024


def _cparams(*sem):
    return pltpu.CompilerParams(dimension_semantics=sem, vmem_limit_bytes=VMEM_LIMIT)


def _dot(a, b):
    return jnp.dot(a, b, preferred_element_type=F32)


def _dot_nt(a, b):
    return lax.dot_general(a, b, (((1,), (1,)), ((), ())), preferred_element_type=F32)


def _split2(a):
    hi = a.astype(BF16)
    lo = (a - hi.astype(F32)).astype(BF16)
    return hi, lo


def _rms(x, g):
    return x * lax.rsqrt(jnp.mean(x * x, axis=-1, keepdims=True) + EPS) * g


def _store_token_tiles(ref, val):
    n, d = val.shape
    h = d // 2
    lo = lax.bitcast_convert_type(val[:, :h].astype(BF16).astype(F32), U32) >> 16
    hi = lax.bitcast_convert_type(val[:, h:].astype(BF16).astype(F32), U32) & jnp.uint32(0xFFFF0000)
    packed = hi | lo
    for s in range(TOKEN_ROWS):
        ref[pl.ds(s, n, stride=TOKEN_ROWS), :] = packed[:, s * LANES:(s + 1) * LANES]


def _load_token_tiles(ref):
    n = ref.shape[0] // TOKEN_ROWS
    packed = jnp.concatenate([ref[pl.ds(s, n, stride=TOKEN_ROWS), :] for s in range(TOKEN_ROWS)], axis=1)
    lo = lax.bitcast_convert_type(packed << 16, F32)
    hi = lax.bitcast_convert_type(packed & jnp.uint32(0xFFFF0000), F32)
    return jnp.concatenate([lo, hi], axis=1)


def _rope(x, c, s_a, s_b, half):
    n = x.shape[-1]
    return x * c + pltpu.roll(x, n - half, 1) * s_a + pltpu.roll(x, half, 1) * s_b


def _ada_kernel(c_ref, w_ref, b_ref, o_ref):
    c = c_ref[...]
    sc = c * jax.nn.sigmoid(c)
    c_hi, c_lo = _split2(sc)
    w = w_ref[0]
    w_hi, w_lo = _split2(w)
    acc = _dot(c_hi, w_hi) + _dot(c_hi, w_lo) + _dot(c_lo, w_hi)
    o_ref[0] = acc + b_ref[0]


def _ada(cond, ada_w, ada_b):
    depth, d, n = ada_w.shape
    rows = cond.shape[0]
    tn = 1536
    return pl.pallas_call(
        _ada_kernel,
        out_shape=jax.ShapeDtypeStruct((depth, rows, n), F32),
        grid=(depth, n // tn),
        in_specs=[
            pl.BlockSpec((rows, d), lambda l, j: (0, 0)),
            pl.BlockSpec((1, d, tn), lambda l, j: (l, 0, j)),
            pl.BlockSpec((1, 1, tn), lambda l, j: (l, 0, j)),
        ],
        out_specs=pl.BlockSpec((1, rows, tn), lambda l, j: (l, 0, j)),
        compiler_params=_cparams("arbitrary", "arbitrary"),
        name="ada_mod",
    )(cond, ada_w, ada_b.reshape(depth, 1, n))


def _proj_ab_kernel(*refs, use_rope, tm):
    if use_rope:
        (x_ref, sc_ref, sh_ref, g_ref, w_ref, sgug_ref, sguw_ref, sgub_ref, qg_ref, kg_ref,
         c_ref, sa_ref, sb_ref, oa_ref, q_ref, k_ref, v_ref) = refs
    else:
        (x_ref, sc_ref, sh_ref, g_ref, w_ref, sgug_ref, sguw_ref, sgub_ref, qg_ref, kg_ref,
         oa_ref, q_ref, k_ref, v_ref) = refs
    x = x_ref[...]
    h = _rms(x, g_ref[...]) * (1.0 + sc_ref[0]) + sh_ref[0]
    p = _dot(h.astype(BF16), w_ref[...])

    n_chunks = tm // CHUNK
    for g in range(SGU_GROUPS):
        u = jax.nn.gelu(p[:, g * SGU_GROUP_CH:(g + 1) * SGU_GROUP_CH])
        v = jax.nn.gelu(p[:, SGU_WIDTH + g * SGU_GROUP_CH:SGU_WIDTH + (g + 1) * SGU_GROUP_CH])
        v = _rms(v, sgug_ref[:, g * SGU_GROUP_CH:(g + 1) * SGU_GROUP_CH]).astype(BF16)
        vcat = jnp.concatenate([v[c * CHUNK:(c + 1) * CHUNK, :] for c in range(n_chunks)], axis=1)
        mixed = _dot(sguw_ref[g], vcat)
        bias = sgub_ref[g]
        for c in range(n_chunks):
            gate = mixed[:, c * SGU_GROUP_CH:(c + 1) * SGU_GROUP_CH] + bias
            oa_ref[c * CHUNK:(c + 1) * CHUNK, g * SGU_GROUP_CH:(g + 1) * SGU_GROUP_CH] = (
                u[c * CHUNK:(c + 1) * CHUNK, :] * gate).astype(oa_ref.dtype)

    qg = qg_ref[...] * (HEAD_DIM ** -0.5)
    for hd in range(GQA_HEADS):
        qh = _rms(p[:, A_END + hd * HEAD_DIM:A_END + (hd + 1) * HEAD_DIM], qg)
        if use_rope:
            qh = _rope(qh, c_ref[...], sa_ref[...], sb_ref[...], HEAD_DIM // 4)
        q_ref[:, hd * HEAD_DIM:(hd + 1) * HEAD_DIM] = qh.astype(q_ref.dtype)
    for hd in range(GQA_KV_HEADS):
        kh = _rms(p[:, K_OFF + hd * HEAD_DIM:K_OFF + (hd + 1) * HEAD_DIM], kg_ref[...])
        if use_rope:
            kh = _rope(kh, c_ref[...], sa_ref[...], sb_ref[...], HEAD_DIM // 4)
        k_ref[:, hd * HEAD_DIM:(hd + 1) * HEAD_DIM] = kh.astype(k_ref.dtype)
    v_ref[...] = p[:, V_OFF:].astype(v_ref.dtype)


def _proj_ab(x, mod, mod_rows, group_rows, norm_g, w_in, sgu_g, sgu_w, sgu_bfull, q_g, k_g, rope, tm):
    m, d = x.shape
    per_group = group_rows // tm
    sc_j, sh_j = mod_rows
    use_rope = rope is not None
    in_specs = [
        pl.BlockSpec((tm, d), lambda i: (i, 0)),
        pl.BlockSpec((1, 1, d), lambda i: ((i // per_group) * 6 + sc_j, 0, 0)),
        pl.BlockSpec((1, 1, d), lambda i: ((i // per_group) * 6 + sh_j, 0, 0)),
        pl.BlockSpec((1, d), lambda i: (0, 0)),
        pl.BlockSpec(w_in.shape, lambda i: (0, 0)),
        pl.BlockSpec((1, SGU_WIDTH), lambda i: (0, 0)),
        pl.BlockSpec(sgu_w.shape, lambda i: (0, 0, 0)),
        pl.BlockSpec(sgu_bfull.shape, lambda i: (0, 0, 0)),
        pl.BlockSpec((1, HEAD_DIM), lambda i: (0, 0)),
        pl.BlockSpec((1, HEAD_DIM), lambda i: (0, 0)),
    ]
    args = [x, mod, mod, norm_g, w_in, sgu_g, sgu_w, sgu_bfull, q_g, k_g]
    if use_rope:
        s_tiles = rope[0].shape[0] // tm
        for t in rope:
            in_specs.append(pl.BlockSpec((tm, HEAD_DIM), lambda i: (i % s_tiles, 0)))
            args.append(t)
    return pl.pallas_call(
        functools.partial(_proj_ab_kernel, use_rope=use_rope, tm=tm),
        out_shape=(jax.ShapeDtypeStruct((m, SGU_WIDTH), BF16),
                   jax.ShapeDtypeStruct((m, GQA_Q_W), BF16),
                   jax.ShapeDtypeStruct((m, GQA_KV_W), BF16),
                   jax.ShapeDtypeStruct((m, GQA_KV_W), BF16)),
        grid=(m // tm,),
        in_specs=in_specs,
        out_specs=(pl.BlockSpec((tm, SGU_WIDTH), lambda i: (i, 0)),
                   pl.BlockSpec((tm, GQA_Q_W), lambda i: (i, 0)),
                   pl.BlockSpec((tm, GQA_KV_W), lambda i: (i, 0)),
                   pl.BlockSpec((tm, GQA_KV_W), lambda i: (i, 0))),
        compiler_params=_cparams("parallel"),
        name="proj_ab",
    )(*args)


def _proj_mla_kernel(*refs, with_q, use_rope):
    it = iter(refs)
    x_ref, sc_ref, sh_ref, g_ref, w_ref = (next(it) for _ in range(5))
    if with_q:
        qg_ref, wuq_ref = next(it), next(it)
    kvg_ref, wuk_ref, wuv_ref = next(it), next(it), next(it)
    if use_rope:
        c_ref, sa_ref, sb_ref = next(it), next(it), next(it)
    if with_q:
        q_ref = next(it)
    k_ref, v_ref = next(it), next(it)

    x = x_ref[...]
    h = _rms(x, g_ref[...]) * (1.0 + sc_ref[0]) + sh_ref[0]
    p = _dot(h.astype(BF16), w_ref[...])
    off = MLA_Q_LORA if with_q else 0
    ckv = _rms(p[:, off:off + MLA_KV_LORA], kvg_ref[...]).astype(BF16)
    kr = p[:, off + MLA_KV_LORA:off + MLA_KV_LORA + LANES]
    if use_rope:
        kr = _rope(kr, c_ref[...], sa_ref[...], sb_ref[...], MLA_ROPE // 4)
    kr = kr.astype(k_ref.dtype)
    k_nope = _dot(ckv, wuk_ref[...])
    v_ref[...] = _dot(ckv, wuv_ref[...]).astype(v_ref.dtype)
    for hd in range(MLA_HEADS):
        k_ref[:, hd * MLA_QK_PAD:hd * MLA_QK_PAD + MLA_NOPE] = (
            k_nope[:, hd * MLA_NOPE:(hd + 1) * MLA_NOPE].astype(k_ref.dtype))
        k_ref[:, hd * MLA_QK_PAD + MLA_NOPE:(hd + 1) * MLA_QK_PAD] = kr
    if with_q:
        cq = _rms(p[:, :MLA_Q_LORA], qg_ref[...]).astype(BF16)
        q = _dot(cq, wuq_ref[...]) * (MLA_QK ** -0.5)
        for hd in range(MLA_HEADS):
            q_ref[:, hd * MLA_QK_PAD:hd * MLA_QK_PAD + MLA_NOPE] = (
                q[:, hd * MLA_QK_PAD:hd * MLA_QK_PAD + MLA_NOPE].astype(q_ref.dtype))
            qr = q[:, hd * MLA_QK_PAD + MLA_NOPE:(hd + 1) * MLA_QK_PAD]
            if use_rope:
                qr = _rope(qr, c_ref[...], sa_ref[...], sb_ref[...], MLA_ROPE // 4)
            q_ref[:, hd * MLA_QK_PAD + MLA_NOPE:(hd + 1) * MLA_QK_PAD] = qr.astype(q_ref.dtype)


def _proj_mla(x, mod, mod_rows, group_rows, norm_g, w_in, q_g, w_uq, kv_g, w_uk, w_uv, rope, with_q, tm):
    m, d = x.shape
    per_group = group_rows // tm
    sc_j, sh_j = mod_rows
    use_rope = rope is not None
    const2 = lambda i: (0, 0)
    in_specs = [
        pl.BlockSpec((tm, d), lambda i: (i, 0)),
        pl.BlockSpec((1, 1, d), lambda i: ((i // per_group) * 6 + sc_j, 0, 0)),
        pl.BlockSpec((1, 1, d), lambda i: ((i // per_group) * 6 + sh_j, 0, 0)),
        pl.BlockSpec((1, d), const2),
        pl.BlockSpec(w_in.shape, const2),
    ]
    args = [x, mod, mod, norm_g, w_in]
    if with_q:
        in_specs += [pl.BlockSpec(q_g.shape, const2), pl.BlockSpec(w_uq.shape, const2)]
        args += [q_g, w_uq]
    in_specs += [pl.BlockSpec(kv_g.shape, const2), pl.BlockSpec(w_uk.shape, const2),
                 pl.BlockSpec(w_uv.shape, const2)]
    args += [kv_g, w_uk, w_uv]
    if use_rope:
        s_tiles = rope[0].shape[0] // tm
        for t in rope:
            in_specs.append(pl.BlockSpec((tm, LANES), lambda i: (i % s_tiles, 0)))
            args.append(t)
    kw = MLA_HEADS * MLA_QK_PAD
    vw = MLA_HEADS * MLA_V
    out_shape = [jax.ShapeDtypeStruct((m, kw), BF16), jax.ShapeDtypeStruct((m, vw), BF16)]
    out_specs = [pl.BlockSpec((tm, kw), lambda i: (i, 0)), pl.BlockSpec((tm, vw), lambda i: (i, 0))]
    if with_q:
        out_shape = [jax.ShapeDtypeStruct((m, kw), BF16)] + out_shape
        out_specs = [pl.BlockSpec((tm, kw), lambda i: (i, 0))] + out_specs
    return pl.pallas_call(
        functools.partial(_proj_mla_kernel, with_q=with_q, use_rope=use_rope),
        out_shape=tuple(out_shape),
        grid=(m // tm,),
        in_specs=in_specs,
        out_specs=tuple(out_specs),
        compiler_params=_cparams("parallel"),
        name="proj_mla",
    )(*args)


def _attn_kernel(*refs, n_seg, tq, n_q):
    q_ref = refs[0]
    kv = refs[1:1 + 2 * n_seg]
    o_ref = refs[1 + 2 * n_seg]

    for c in range(n_q):
        q = q_ref[c * tq:(c + 1) * tq, :]
        ss = [_dot_nt(q, kv[2 * j][...]) for j in range(n_seg)]
        mx = ss[0].max(axis=-1, keepdims=True)
        for s in ss[1:]:
            mx = jnp.maximum(mx, s.max(axis=-1, keepdims=True))
        acc = None
        den = None
        for j, s in enumerate(ss):
            e = jnp.exp(s - mx)
            d = e.sum(axis=-1, keepdims=True)
            o = _dot(e.astype(BF16), kv[2 * j + 1][...])
            acc = o if acc is None else acc + o
            den = d if den is None else den + d
        o_ref[c * tq:(c + 1) * tq, :] = (acc / den).astype(o_ref.dtype)


def _attention(q, segs, batch, heads, kv_heads, dk, dv, tq):
    t_q = q.shape[0] // batch
    group = heads // kv_heads
    in_specs = [pl.BlockSpec((t_q, dk), lambda b, h: (b, h))]
    args = [q]
    for k, v in segs:
        t_k = k.shape[0] // batch
        in_specs.append(pl.BlockSpec((t_k, dk), lambda b, h: (b, h // group)))
        in_specs.append(pl.BlockSpec((t_k, dv), lambda b, h: (b, h // group)))
        args += [k, v]
    tq = min(tq, t_q)
    return pl.pallas_call(
        functools.partial(_attn_kernel, n_seg=len(segs), tq=tq, n_q=t_q // tq),
        out_shape=jax.ShapeDtypeStruct((q.shape[0], heads * dv), BF16),
        grid=(batch, heads),
        in_specs=in_specs,
        out_specs=pl.BlockSpec((t_q, dv), lambda b, h: (b, h)),
        compiler_params=_cparams("parallel", "parallel"),
        name="attention",
    )(*args)


def _out_router_kernel(*refs, n_in, tm):
    ins = refs[:2 * n_in]
    (x_ref, g1_ref, sc_ref, sh_ref, ng_ref, rw_ref, rb_ref, cnt0_ref,
     xo_ref, h2_ref, idx_ref, gate_ref, cnt_ref) = refs[2 * n_in:]
    i = pl.program_id(0)

    @pl.when(i == 0)
    def _():
        cnt_ref[...] = cnt0_ref[...]

    y = None
    for j in range(n_in):
        t = _dot(ins[2 * j][...], ins[2 * j + 1][...])
        y = t if y is None else y + t
    xn = x_ref[...] + g1_ref[0] * y
    xo_ref[...] = xn
    h2 = _rms(xn, ng_ref[...]) * (1.0 + sc_ref[0]) + sh_ref[0]
    _store_token_tiles(h2_ref, h2)

    h_hi, h_lo = _split2(h2)
    w_hi, w_lo = _split2(rw_ref[...])
    logits = _dot_nt(w_hi, h_hi) + _dot_nt(w_hi, h_lo) + _dot_nt(w_lo, h_hi) + rb_ref[...]

    e_iota = lax.broadcasted_iota(jnp.int32, logits.shape, 0)
    work = logits
    tops, idxs = [], []
    for _ in range(TOP_K):
        mx = work.max(axis=0, keepdims=True)
        ix = jnp.where(work == mx, e_iota, N_EXPERTS).min(axis=0, keepdims=True)
        tops.append(mx)
        idxs.append(ix)
        work = jnp.where(e_iota == ix, -jnp.inf, work)
    exps = [jnp.exp(t - tops[0]) for t in tops]
    den = exps[0] + exps[1] + exps[2] + exps[3]

    for k in range(TOP_K):
        idx_ref[k:k + 1, :] = idxs[k]
        gate_ref[k:k + 1, :] = exps[k] / den
    cnt = sum((e_iota == ix).astype(F32) for ix in idxs)
    cnt_ref[...] = cnt_ref[...] + cnt.sum(axis=1, keepdims=True)


def _out_router(ins, x, mod, mod_rows, group_rows, norm_g, router_wt, router_b, cnt0, tm):
    m, d = x.shape
    per_group = group_rows // tm
    g1_j, sc_j, sh_j = mod_rows
    in_specs, args = [], []
    for a, w in ins:
        in_specs += [pl.BlockSpec((tm, a.shape[1]), lambda i: (i, 0)), pl.BlockSpec(w.shape, lambda i: (0, 0))]
        args += [a, w]
    mod_spec = lambda j: pl.BlockSpec((1, 1, d), lambda i: ((i // per_group) * 6 + j, 0, 0))
    in_specs += [
        pl.BlockSpec((tm, d), lambda i: (i, 0)),
        mod_spec(g1_j), mod_spec(sc_j), mod_spec(sh_j),
        pl.BlockSpec((1, d), lambda i: (0, 0)),
        pl.BlockSpec(router_wt.shape, lambda i: (0, 0)),
        pl.BlockSpec(router_b.shape, lambda i: (0, 0)),
        pl.BlockSpec(cnt0.shape, lambda i: (0, 0)),
    ]
    args += [x, mod, mod, mod, norm_g, router_wt, router_b, cnt0]
    small = lambda dt: jax.ShapeDtypeStruct((TOP_K, m), dt)
    small_spec = pl.BlockSpec((TOP_K, tm), lambda i: (0, i))
    return pl.pallas_call(
        functools.partial(_out_router_kernel, n_in=len(ins), tm=tm),
        out_shape=(jax.ShapeDtypeStruct((m, d), F32), jax.ShapeDtypeStruct((m * TOKEN_ROWS, LANES), U32),
                   small(jnp.int32), small(F32),
                   jax.ShapeDtypeStruct(cnt0.shape, F32)),
        grid=(m // tm,),
        in_specs=in_specs,
        out_specs=(pl.BlockSpec((tm, d), lambda i: (i, 0)),
                   pl.BlockSpec((tm * TOKEN_ROWS, LANES), lambda i: (i, 0)),
                   small_spec, small_spec,
                   pl.BlockSpec(cnt0.shape, lambda i: (0, 0))),
        compiler_params=_cparams("arbitrary"),
        name="out_router",
    )(*args)


def _moe_kernel(blk_e_ref, nused_ref, tok0_ref, tokn_ref, dstp_ref, dstc_ref, h_hbm, w1_ref, b1_ref, w2_ref, b2_ref,
                o_hbm, x0, x1, y0, y1, gsem, ssem, *, tm):
    i = pl.program_id(0)
    n_used = nused_ref[0]
    xs, ys = (x0, x1), (y0, y1)

    def tile_at(ref, row):
        return ref.at[pl.ds(pl.multiple_of(row, TOKEN_ROWS), TOKEN_ROWS)]

    def for_tokens(fn):
        def body(b, carry):
            for j in range(DMA_UNROLL):
                fn(b * DMA_UNROLL + j)
            return carry
        lax.fori_loop(0, tm // DMA_UNROLL, body, 0)

    def gather(tok_ref, xbuf, sem):
        for_tokens(lambda r: pltpu.make_async_copy(
            tile_at(h_hbm, tok_ref[0, 0, r]), tile_at(xbuf, r * TOKEN_ROWS), sem).start())

    def scatter(dst_ref, ybuf, sem):
        for_tokens(lambda r: pltpu.make_async_copy(
            tile_at(ybuf, r * TOKEN_ROWS), tile_at(o_hbm, dst_ref[0, 0, r]), sem).start(priority=1))

    def wait_gather(xbuf, sem):
        pltpu.make_async_copy(h_hbm.at[pl.ds(0, tm * TOKEN_ROWS)], xbuf, sem).wait()

    def wait_scatter(ybuf, sem):
        pltpu.make_async_copy(ybuf, o_hbm.at[pl.ds(0, tm * TOKEN_ROWS)], sem).wait()

    @pl.when(i == 0)
    def _():
        gather(tok0_ref, x0, gsem.at[0])
        y1[...] = jnp.zeros_like(y1)

    def step(p):
        xb, xn, yb, yp = xs[p], xs[1 - p], ys[p], ys[1 - p]
        wait_gather(xb, gsem.at[p])

        @pl.when(i >= 1)
        def _():
            wait_scatter(yb, ssem.at[p])

        @pl.when(n_used > 0)
        def _():
            scatter(dstp_ref, yp, ssem.at[1 - p])
            gather(tokn_ref, xn, gsem.at[1 - p])

        x = _load_token_tiles(xb).astype(BF16)
        hid = _dot(x, w1_ref[0]) + b1_ref[0]
        f = hid.shape[1] // 2
        g = jnp.minimum(hid[:, :f], SWIGLU_LIMIT)
        lin = jnp.clip(hid[:, f:], -SWIGLU_LIMIT, SWIGLU_LIMIT)
        act = g * jax.nn.sigmoid(SWIGLU_ALPHA * g) * (lin + 1.0)
        _store_token_tiles(yb, _dot(act.astype(BF16), w2_ref[0]) + b2_ref[0])

        @pl.when(i == n_used - 1)
        def _():
            wait_scatter(yp, ssem.at[1 - p])
            scatter(dstc_ref, yb, ssem.at[p])
            wait_scatter(yb, ssem.at[p])
            wait_gather(xn, gsem.at[1 - p])

    for p in range(2):
        @pl.when(jnp.logical_and(i < n_used, i % 2 == p))
        def _(p=p):
            step(p)


def _moe(h2, row_tok, row_dst, blk_e, n_used, w1, b1, w2, b2, out_rows, tm):
    n_tiles = row_tok.shape[0]
    d = w1.shape[1]
    f2 = w1.shape[2]
    tile_rows = (tm * TOKEN_ROWS, LANES)
    smem = lambda fn: pl.BlockSpec((1, 1, tm), fn, memory_space=pltpu.SMEM)
    grid_spec = pltpu.PrefetchScalarGridSpec(
        num_scalar_prefetch=2,
        grid=(n_tiles,),
        in_specs=[
            smem(lambda i, be, nu: (0, 0, 0)),
            smem(lambda i, be, nu: (jnp.minimum(i + 1, nu[0] - 1), 0, 0)),
            smem(lambda i, be, nu: (i, 0, 0)),
            smem(lambda i, be, nu: (i + 1, 0, 0)),
            pl.BlockSpec(memory_space=pl.ANY),
            pl.BlockSpec((1, d, f2), lambda i, be, nu: (be[i], 0, 0)),
            pl.BlockSpec((1, 1, f2), lambda i, be, nu: (be[i], 0, 0)),
            pl.BlockSpec((1, f2 // 2, d), lambda i, be, nu: (be[i], 0, 0)),
            pl.BlockSpec((1, 1, d), lambda i, be, nu: (be[i], 0, 0)),
        ],
        out_specs=pl.BlockSpec(memory_space=pl.ANY),
        scratch_shapes=[pltpu.VMEM(tile_rows, U32), pltpu.VMEM(tile_rows, U32),
                        pltpu.VMEM(tile_rows, U32), pltpu.VMEM(tile_rows, U32),
                        pltpu.SemaphoreType.DMA((2,)), pltpu.SemaphoreType.DMA((2,))],
    )
    return pl.pallas_call(
        functools.partial(_moe_kernel, tm=tm),
        out_shape=jax.ShapeDtypeStruct((out_rows * TOKEN_ROWS, LANES), U32),
        grid_spec=grid_spec,
        compiler_params=_cparams("arbitrary"),
        name="moe_experts",
    )(blk_e, n_used, row_tok, row_tok, row_dst, row_dst, h2, w1, b1, w2, b2)


def _combine_kernel(*refs, tq, final):
    ys = refs[:TOP_K]
    if final:
        gate_ref, x_ref, g2_ref, fg_ref, o_ref = refs[TOP_K:]
    else:
        gate_ref, x_ref, g2_ref, o_ref = refs[TOP_K:]
    f = None
    d = x_ref.shape[1]
    for k in range(TOP_K):
        gcol = jnp.broadcast_to(gate_ref[k:k + 1, :], (LANES, tq)).T
        t = _load_token_tiles(ys[k]) * jnp.tile(gcol, (1, d // LANES))
        f = t if f is None else f + t
    xn = x_ref[...] + g2_ref[0] * f
    if final:
        xn = _rms(xn, fg_ref[...])
    o_ref[...] = xn


def _combine(ys, t_tot, t_off, gate, x, mod, g2_j, group_rows, final_g, tq):
    m, d = x.shape
    per_group = group_rows // tq
    final = final_g is not None
    in_specs = [pl.BlockSpec((tq * TOKEN_ROWS, LANES),
                             functools.partial(lambda i, k: ((k * t_tot + t_off) // tq + i, 0), k=k))
                for k in range(TOP_K)]
    in_specs += [
        pl.BlockSpec((TOP_K, tq), lambda i: (0, i)),
        pl.BlockSpec((tq, d), lambda i: (i, 0)),
        pl.BlockSpec((1, 1, d), lambda i: ((i // per_group) * 6 + g2_j, 0, 0)),
    ]
    args = [ys] * TOP_K + [gate, x, mod]
    if final:
        in_specs.append(pl.BlockSpec((1, d), lambda i: (0, 0)))
        args.append(final_g)
    return pl.pallas_call(
        functools.partial(_combine_kernel, tq=tq, final=final),
        out_shape=jax.ShapeDtypeStruct((m, d), F32),
        grid=(m // tq,),
        in_specs=in_specs,
        out_specs=pl.BlockSpec((tq, d), lambda i: (i, 0)),
        compiler_params=_cparams("parallel"),
        name="moe_combine",
    )(*args)


def _rope_tables(seq, rot_dim):
    t = np.arange(seq)
    row = (t // GRID_W).astype(np.float32)
    col = (t % GRID_W).astype(np.float32)
    axis_dim = rot_dim // 2
    quarter = axis_dim // 2
    inv = jnp.asarray(ROPE_THETA, F32) ** (-jnp.arange(0, axis_dim, 2, dtype=F32) / axis_dim)
    ang_r = jnp.asarray(row)[:, None] * inv
    ang_c = jnp.asarray(col)[:, None] * inv
    zeros = jnp.zeros_like(ang_r)
    cos = jnp.concatenate([jnp.cos(ang_r)] * 2 + [jnp.cos(ang_c)] * 2, axis=1)
    s_a = jnp.concatenate([-jnp.sin(ang_r), zeros, -jnp.sin(ang_c), zeros], axis=1)
    s_b = jnp.concatenate([zeros, jnp.sin(ang_r), zeros, jnp.sin(ang_c)], axis=1)
    pad = LANES - rot_dim
    if pad:
        cos, s_a, s_b = (jnp.pad(a, ((0, 0), (0, pad))) for a in (cos, s_a, s_b))
    del quarter
    return cos, s_a, s_b


def _route(idx, counts, tm):
    t_tot = idx.shape[1]
    n_assign = t_tot * TOP_K
    n_rows = n_assign + N_EXPERTS * tm
    n_tiles = n_rows // tm
    counts = counts.astype(jnp.int32)
    pad_need = (-counts) % tm
    flat_e = idx.T.reshape(-1)
    slot = jnp.arange(tm, dtype=jnp.int32)[None, :]
    pad_keys = jnp.where(slot < pad_need[:, None], jnp.arange(N_EXPERTS, dtype=jnp.int32)[:, None], N_EXPERTS)
    keys = jnp.concatenate([flat_e, pad_keys.reshape(-1)])
    vals = jnp.concatenate([jnp.arange(n_assign, dtype=jnp.int32), jnp.full((N_EXPERTS * tm,), -1, jnp.int32)])
    skeys, svals = lax.sort((keys, vals), num_keys=1, is_stable=True)
    real = svals >= 0
    row_tok = jnp.where(real, svals // TOP_K, 0)
    rows = jnp.arange(n_rows, dtype=jnp.int32)
    row_dst = jnp.where(real, (svals % TOP_K) * t_tot + svals // TOP_K, n_assign + rows % tm)
    tile_e = skeys[::tm]
    n_used = jnp.sum(tile_e < N_EXPERTS).astype(jnp.int32)
    last_e = tile_e[jnp.maximum(n_used - 1, 0)]
    blk_e = jnp.where(tile_e < N_EXPERTS, tile_e, last_e).astype(jnp.int32)
    row_dst = jnp.concatenate([n_assign + jnp.arange(tm, dtype=jnp.int32), row_dst])
    row_tok = (row_tok * TOKEN_ROWS).reshape(n_tiles, 1, tm)
    row_dst = (row_dst * TOKEN_ROWS).reshape(n_tiles + 1, 1, tm)
    return row_tok, row_dst, blk_e, n_used.reshape(1)


def _pick(m, pref):
    t = pref
    while m % t:
        t //= 2
    return t


def kernel(x, c, ctx, c_ctx, ada_w, ada_b, norm_mix_g, norm_ffn_g, ab_w_in, sgu_norm_g, sgu_w, sgu_b,
           gqa_q_norm_g, gqa_k_norm_g, ab_w_out, mla_w_in, mla_q_norm_g, mla_kv_norm_g, mla_w_uq, mla_w_ukv,
           mla_w_out, router_w, router_b, moe_w1, moe_b1, moe_w2, moe_b2, final_norm_g):
    bn, s, d = x.shape
    l = ctx.shape[1]
    depth = ada_w.shape[0]
    m_lat, m_ctx = bn * s, bn * l
    tm_lat = _pick(s, 512)
    tm_ctx = _pick(l, 512)
    tm_moe = 512
    tq_comb = _pick(int(np.gcd(s, m_ctx)), 512)

    n_cond = (bn + 1 + 7) // 8 * 8
    cond = jnp.concatenate([c, c_ctx[None, :], jnp.zeros((n_cond - bn - 1, d), F32)], axis=0)
    mod_all = _ada(cond, ada_w, ada_b)

    rope_gqa = _rope_tables(s, HEAD_DIM)
    rope_mla = _rope_tables(s, MLA_ROPE)
    w1_all = moe_w1.astype(BF16).reshape(depth * N_EXPERTS, d, -1)
    w2_all = moe_w2.astype(BF16).reshape(depth * N_EXPERTS, -1, d)
    b1_all = moe_b1.reshape(depth * N_EXPERTS, 1, -1)
    b2_all = moe_b2.reshape(depth * N_EXPERTS, 1, d)

    xl = x.reshape(m_lat, d)
    xc = ctx.reshape(m_ctx, d)
    for i in range(depth):
        ctx_out = i < depth - 1
        j = i // 2
        mod_lat = mod_all[i, :bn].reshape(bn * 6, 1, d)
        mod_ctx = mod_all[i, bn].reshape(6, 1, d)
        ng = norm_mix_g[i].reshape(1, d)
        if i % 2 == 0:
            w_in = ab_w_in[j].astype(BF16)
            w_out = ab_w_out[j].astype(BF16)
            sgu_g = sgu_norm_g[j].reshape(1, SGU_WIDTH)
            sgu_wb = sgu_w[j].astype(BF16)
            sgu_bfull = jnp.broadcast_to(sgu_b[j][:, :, None], (SGU_GROUPS, CHUNK, SGU_GROUP_CH))
            q_g = gqa_q_norm_g[j].reshape(1, HEAD_DIM)
            k_g = gqa_k_norm_g[j].reshape(1, HEAD_DIM)
            oa, q, k, v = _proj_ab(xl, mod_lat, (1, 0), s, ng, w_in, sgu_g, sgu_wb, sgu_bfull, q_g, k_g,
                                   rope_gqa, tm_lat)
            oac, qc, kc, vc = _proj_ab(xc, mod_ctx, (1, 0), m_ctx, ng, w_in, sgu_g, sgu_wb, sgu_bfull, q_g, k_g,
                                       None, tm_ctx)
            ob = _attention(q, [(kc, vc), (k, v)], bn, GQA_HEADS, GQA_KV_HEADS, HEAD_DIM, HEAD_DIM, ATTN_TQ)
            ins = [(oa, w_out[:SGU_WIDTH]), (ob, w_out[SGU_WIDTH:])]
            if ctx_out:
                obc = _attention(qc, [(kc, vc)], bn, GQA_HEADS, GQA_KV_HEADS, HEAD_DIM, HEAD_DIM, ATTN_TQ)
                ins_c = [(oac, w_out[:SGU_WIDTH]), (obc, w_out[SGU_WIDTH:])]
        else:
            w_in_f = mla_w_in[j]
            w_in = jnp.pad(w_in_f, ((0, 0), (0, MLA_IN_PAD - w_in_f.shape[1]))).astype(BF16)
            w_in_kv = w_in[:, MLA_Q_LORA:]
            w_out = mla_w_out[j].astype(BF16)
            q_g = mla_q_norm_g[j].reshape(1, MLA_Q_LORA)
            kv_g = mla_kv_norm_g[j].reshape(1, MLA_KV_LORA)
            w_uq = mla_w_uq[j].reshape(MLA_Q_LORA, MLA_HEADS, MLA_QK)
            w_uq = jnp.pad(w_uq, ((0, 0), (0, 0), (0, MLA_QK_PAD - MLA_QK)))
            w_uq = w_uq.reshape(MLA_Q_LORA, MLA_HEADS * MLA_QK_PAD).astype(BF16)
            w_ukv = mla_w_ukv[j].reshape(MLA_KV_LORA, MLA_HEADS, MLA_NOPE + MLA_V)
            w_uk = w_ukv[:, :, :MLA_NOPE].reshape(MLA_KV_LORA, MLA_HEADS * MLA_NOPE).astype(BF16)
            w_uv = w_ukv[:, :, MLA_NOPE:].reshape(MLA_KV_LORA, MLA_HEADS * MLA_V).astype(BF16)
            q, k, v = _proj_mla(xl, mod_lat, (1, 0), s, ng, w_in, q_g, w_uq, kv_g, w_uk, w_uv, rope_mla, True,
                                tm_lat)
            if ctx_out:
                qc, kc, vc = _proj_mla(xc, mod_ctx, (1, 0), m_ctx, ng, w_in, q_g, w_uq, kv_g, w_uk, w_uv, None,
                                       True, tm_ctx)
            else:
                kc, vc = _proj_mla(xc, mod_ctx, (1, 0), m_ctx, ng, w_in_kv, None, None, kv_g, w_uk, w_uv, None,
                                   False, tm_ctx)
            ob = _attention(q, [(kc, vc), (k, v)], bn, MLA_HEADS, MLA_HEADS, MLA_QK_PAD, MLA_V, ATTN_TQ)
            ins = [(ob, w_out)]
            if ctx_out:
                obc = _attention(qc, [(kc, vc)], bn, MLA_HEADS, MLA_HEADS, MLA_QK_PAD, MLA_V, ATTN_TQ)
                ins_c = [(obc, w_out)]

        nfg = norm_ffn_g[i].reshape(1, d)
        rwt = router_w[i].T
        rb = router_b[i].reshape(N_EXPERTS, 1)
        cnt0 = jnp.zeros((N_EXPERTS, LANES), F32)
        xl, h2, idx, gate, cnt = _out_router(ins, xl, mod_lat, (2, 4, 3), s, nfg, rwt, rb, cnt0, tm_lat)
        if ctx_out:
            xc, h2c, idx_c, gate_c, cnt = _out_router(ins_c, xc, mod_ctx, (2, 4, 3), m_ctx, nfg, rwt, rb, cnt, tm_ctx)
            h2 = jnp.concatenate([h2, h2c], axis=0)
            idx = jnp.concatenate([idx, idx_c], axis=1)
        t_tot = h2.shape[0] // TOKEN_ROWS
        n_assign = t_tot * TOP_K
        row_tok, row_dst, blk_e, n_used = _route(idx, cnt[:, 0], tm_moe)
        ys = _moe(h2, row_tok, row_dst, blk_e + i * N_EXPERTS, n_used, w1_all, b1_all, w2_all, b2_all,
                  n_assign + tm_moe, tm_moe)
        last = i == depth - 1
        fg = final_norm_g.reshape(1, d) if last else None
        xl = _combine(ys, t_tot, 0, gate, xl, mod_lat, 5, s, fg, tq_comb)
        if ctx_out:
            xc = _combine(ys, t_tot, m_lat, gate_c, xc, mod_ctx, 5, m_ctx, None, tq_comb)
    return xl.reshape(bn, s, d)
```

```python
import functools

import jax
import jax.numpy as jnp
import numpy as np
from jax import lax
from jax.experimental import pallas as pl
from jax.experimental.pallas import tpu as pltpu

F32 = jnp.float32
BF16 = jnp.bfloat16
U32 = jnp.uint32

GRID_W = 64
EPS = 1e-6
ROPE_THETA = 10000.0
CHUNK = 128
SGU_GROUPS = 4
SGU_GROUP_CH = 128
SGU_WIDTH = SGU_GROUPS * SGU_GROUP_CH
GQA_HEADS = 4
GQA_KV_HEADS = 2
HEAD_DIM = 128
GQA_Q_W = GQA_HEADS * HEAD_DIM
GQA_KV_W = GQA_KV_HEADS * HEAD_DIM
A_END = 2 * SGU_WIDTH
K_OFF = A_END + GQA_Q_W
V_OFF = K_OFF + GQA_KV_W
MLA_HEADS = 8
MLA_Q_LORA = 512
MLA_KV_LORA = 256
MLA_NOPE = 128
MLA_ROPE = 64
MLA_V = 128
MLA_QK = MLA_NOPE + MLA_ROPE
MLA_QK_PAD = 256
MLA_IN_PAD = 896
N_EXPERTS = 32
TOP_K = 4
D_EXPERT = 1024
SWIGLU_LIMIT = 7.0
SWIGLU_ALPHA = 1.702

LANES = 128
TOKEN_ROWS = 4
DMA_UNROLL = 32
ROW_SUB = 2
ATTN_TQ = 256
VMEM_LIMIT = 56 * 1024 * 1024


def _cparams(*sem):
    return pltpu.CompilerParams(dimension_semantics=sem, vmem_limit_bytes=VMEM_LIMIT)


def _dot(a, b):
    return jnp.dot(a, b, preferred_element_type=F32)


def _dot_nt(a, b):
    return lax.dot_general(a, b, (((1,), (1,)), ((), ())), preferred_element_type=F32)


def _split2(a):
    hi = a.astype(BF16)
    lo = (a - hi.astype(F32)).astype(BF16)
    return hi, lo


def _row_sub(tm):
    return ROW_SUB if tm % (ROW_SUB * CHUNK) == 0 else 1


def _rms(x, g):
    return x * lax.rsqrt(jnp.mean(x * x, axis=-1, keepdims=True) + EPS) * g


def _store_token_tiles(ref, val):
    n, d = val.shape
    h = d // 2
    lo = lax.bitcast_convert_type(val[:, :h].astype(BF16).astype(F32), U32) >> 16
    hi = lax.bitcast_convert_type(val[:, h:].astype(BF16).astype(F32), U32) & jnp.uint32(0xFFFF0000)
    packed = hi | lo
    for s in range(TOKEN_ROWS):
        ref[pl.ds(s, n, stride=TOKEN_ROWS), :] = packed[:, s * LANES:(s + 1) * LANES]


def _load_token_tiles(ref):
    n = ref.shape[0] // TOKEN_ROWS
    packed = jnp.concatenate([ref[pl.ds(s, n, stride=TOKEN_ROWS), :] for s in range(TOKEN_ROWS)], axis=1)
    lo = lax.bitcast_convert_type(packed << 16, F32)
    hi = lax.bitcast_convert_type(packed & jnp.uint32(0xFFFF0000), F32)
    return jnp.concatenate([lo, hi], axis=1)


def _rope(x, c, s_a, s_b, half):
    n = x.shape[-1]
    return x * c + pltpu.roll(x, n - half, 1) * s_a + pltpu.roll(x, half, 1) * s_b


def _ada_kernel(c_ref, w_ref, b_ref, o_ref):
    c = c_ref[...]
    sc = c * jax.nn.sigmoid(c)
    c_hi, c_lo = _split2(sc)
    w = w_ref[0]
    w_hi, w_lo = _split2(w)
    acc = _dot(c_hi, w_hi) + _dot(c_hi, w_lo) + _dot(c_lo, w_hi)
    o_ref[0] = acc + b_ref[0]


def _ada(cond, ada_w, ada_b):
    depth, d, n = ada_w.shape
    rows = cond.shape[0]
    tn = 1536
    return pl.pallas_call(
        _ada_kernel,
        out_shape=jax.ShapeDtypeStruct((depth, rows, n), F32),
        grid=(depth, n // tn),
        in_specs=[
            pl.BlockSpec((rows, d), lambda l, j: (0, 0)),
            pl.BlockSpec((1, d, tn), lambda l, j: (l, 0, j)),
            pl.BlockSpec((1, 1, tn), lambda l, j: (l, 0, j)),
        ],
        out_specs=pl.BlockSpec((1, rows, tn), lambda l, j: (l, 0, j)),
        compiler_params=_cparams("arbitrary", "arbitrary"),
        name="ada_mod",
    )(cond, ada_w, ada_b.reshape(depth, 1, n))


def _proj_ab_kernel(*refs, use_rope, tm, sub):
    if use_rope:
        (x_ref, sc_ref, sh_ref, g_ref, w_ref, sgug_ref, sguw_ref, sgub_ref, qg_ref, kg_ref,
         c_ref, sa_ref, sb_ref, oa_ref, q_ref, k_ref, v_ref) = refs
    else:
        (x_ref, sc_ref, sh_ref, g_ref, w_ref, sgug_ref, sguw_ref, sgub_ref, qg_ref, kg_ref,
         oa_ref, q_ref, k_ref, v_ref) = refs
    ts = tm // sub
    n_chunks = ts // CHUNK
    qg = qg_ref[...] * (HEAD_DIM ** -0.5)
    for b in range(sub):
        rows = slice(b * ts, (b + 1) * ts)
        x = x_ref[rows, :]
        h = _rms(x, g_ref[...]) * (1.0 + sc_ref[0]) + sh_ref[0]
        p = _dot(h.astype(BF16), w_ref[...])
        rope = (lambda t: _rope(t, c_ref[rows, :], sa_ref[rows, :], sb_ref[rows, :], HEAD_DIM // 4)) \
            if use_rope else (lambda t: t)

        for g in range(SGU_GROUPS):
            u = jax.nn.gelu(p[:, g * SGU_GROUP_CH:(g + 1) * SGU_GROUP_CH])
            v = jax.nn.gelu(p[:, SGU_WIDTH + g * SGU_GROUP_CH:SGU_WIDTH + (g + 1) * SGU_GROUP_CH])
            v = _rms(v, sgug_ref[:, g * SGU_GROUP_CH:(g + 1) * SGU_GROUP_CH]).astype(BF16)
            vcat = jnp.concatenate([v[c * CHUNK:(c + 1) * CHUNK, :] for c in range(n_chunks)], axis=1)
            mixed = _dot(sguw_ref[g], vcat)
            bias = sgub_ref[g]
            for c in range(n_chunks):
                gate = mixed[:, c * SGU_GROUP_CH:(c + 1) * SGU_GROUP_CH] + bias
                oa_ref[b * ts + c * CHUNK:b * ts + (c + 1) * CHUNK, g * SGU_GROUP_CH:(g + 1) * SGU_GROUP_CH] = (
                    u[c * CHUNK:(c + 1) * CHUNK, :] * gate).astype(oa_ref.dtype)

        for hd in range(GQA_HEADS):
            qh = rope(_rms(p[:, A_END + hd * HEAD_DIM:A_END + (hd + 1) * HEAD_DIM], qg))
            q_ref[rows, hd * HEAD_DIM:(hd + 1) * HEAD_DIM] = qh.astype(q_ref.dtype)
        for hd in range(GQA_KV_HEADS):
            kh = rope(_rms(p[:, K_OFF + hd * HEAD_DIM:K_OFF + (hd + 1) * HEAD_DIM], kg_ref[...]))
            k_ref[rows, hd * HEAD_DIM:(hd + 1) * HEAD_DIM] = kh.astype(k_ref.dtype)
        v_ref[rows, :] = p[:, V_OFF:].astype(v_ref.dtype)


def _proj_ab(x, mod, mod_rows, group_rows, norm_g, w_in, sgu_g, sgu_w, sgu_bfull, q_g, k_g, rope, tm):
    m, d = x.shape
    per_group = group_rows // tm
    sc_j, sh_j = mod_rows
    use_rope = rope is not None
    in_specs = [
        pl.BlockSpec((tm, d), lambda i: (i, 0)),
        pl.BlockSpec((1, 1, d), lambda i: ((i // per_group) * 6 + sc_j, 0, 0)),
        pl.BlockSpec((1, 1, d), lambda i: ((i // per_group) * 6 + sh_j, 0, 0)),
        pl.BlockSpec((1, d), lambda i: (0, 0)),
        pl.BlockSpec(w_in.shape, lambda i: (0, 0)),
        pl.BlockSpec((1, SGU_WIDTH), lambda i: (0, 0)),
        pl.BlockSpec(sgu_w.shape, lambda i: (0, 0, 0)),
        pl.BlockSpec(sgu_bfull.shape, lambda i: (0, 0, 0)),
        pl.BlockSpec((1, HEAD_DIM), lambda i: (0, 0)),
        pl.BlockSpec((1, HEAD_DIM), lambda i: (0, 0)),
    ]
    args = [x, mod, mod, norm_g, w_in, sgu_g, sgu_w, sgu_bfull, q_g, k_g]
    if use_rope:
        s_tiles = rope[0].shape[0] // tm
        for t in rope:
            in_specs.append(pl.BlockSpec((tm, HEAD_DIM), lambda i: (i % s_tiles, 0)))
            args.append(t)
    return pl.pallas_call(
        functools.partial(_proj_ab_kernel, use_rope=use_rope, tm=tm, sub=_row_sub(tm)),
        out_shape=(jax.ShapeDtypeStruct((m, SGU_WIDTH), BF16),
                   jax.ShapeDtypeStruct((m, GQA_Q_W), BF16),
                   jax.ShapeDtypeStruct((m, GQA_KV_W), BF16),
                   jax.ShapeDtypeStruct((m, GQA_KV_W), BF16)),
        grid=(m // tm,),
        in_specs=in_specs,
        out_specs=(pl.BlockSpec((tm, SGU_WIDTH), lambda i: (i, 0)),
                   pl.BlockSpec((tm, GQA_Q_W), lambda i: (i, 0)),
                   pl.BlockSpec((tm, GQA_KV_W), lambda i: (i, 0)),
                   pl.BlockSpec((tm, GQA_KV_W), lambda i: (i, 0))),
        compiler_params=_cparams("parallel"),
        name="proj_ab",
    )(*args)


def _proj_mla_kernel(*refs, with_q, use_rope, sub):
    it = iter(refs)
    x_ref, sc_ref, sh_ref, g_ref, w_ref = (next(it) for _ in range(5))
    if with_q:
        qg_ref, wuq_ref = next(it), next(it)
    kvg_ref, wuk_ref, wuv_ref = next(it), next(it), next(it)
    if use_rope:
        c_ref, sa_ref, sb_ref = next(it), next(it), next(it)
    if with_q:
        q_ref = next(it)
    k_ref, v_ref = next(it), next(it)

    ts = x_ref.shape[0] // sub
    off = MLA_Q_LORA if with_q else 0
    for b in range(sub):
        rows = slice(b * ts, (b + 1) * ts)
        rope = (lambda t: _rope(t, c_ref[rows, :], sa_ref[rows, :], sb_ref[rows, :], MLA_ROPE // 4)) \
            if use_rope else (lambda t: t)
        x = x_ref[rows, :]
        h = _rms(x, g_ref[...]) * (1.0 + sc_ref[0]) + sh_ref[0]
        p = _dot(h.astype(BF16), w_ref[...])
        ckv = _rms(p[:, off:off + MLA_KV_LORA], kvg_ref[...]).astype(BF16)
        kr = rope(p[:, off + MLA_KV_LORA:off + MLA_KV_LORA + LANES])
        kr = kr.astype(k_ref.dtype)
        k_nope = _dot(ckv, wuk_ref[...])
        v_ref[rows, :] = _dot(ckv, wuv_ref[...]).astype(v_ref.dtype)
        for hd in range(MLA_HEADS):
            k_ref[rows, hd * MLA_QK_PAD:hd * MLA_QK_PAD + MLA_NOPE] = (
                k_nope[:, hd * MLA_NOPE:(hd + 1) * MLA_NOPE].astype(k_ref.dtype))
            k_ref[rows, hd * MLA_QK_PAD + MLA_NOPE:(hd + 1) * MLA_QK_PAD] = kr
        if with_q:
            cq = _rms(p[:, :MLA_Q_LORA], qg_ref[...]).astype(BF16)
            q = _dot(cq, wuq_ref[...]) * (MLA_QK ** -0.5)
            for hd in range(MLA_HEADS):
                q_ref[rows, hd * MLA_QK_PAD:hd * MLA_QK_PAD + MLA_NOPE] = (
                    q[:, hd * MLA_QK_PAD:hd * MLA_QK_PAD + MLA_NOPE].astype(q_ref.dtype))
                qr = rope(q[:, hd * MLA_QK_PAD + MLA_NOPE:(hd + 1) * MLA_QK_PAD])
                q_ref[rows, hd * MLA_QK_PAD + MLA_NOPE:(hd + 1) * MLA_QK_PAD] = qr.astype(q_ref.dtype)


def _proj_mla(x, mod, mod_rows, group_rows, norm_g, w_in, q_g, w_uq, kv_g, w_uk, w_uv, rope, with_q, tm):
    m, d = x.shape
    per_group = group_rows // tm
    sc_j, sh_j = mod_rows
    use_rope = rope is not None
    const2 = lambda i: (0, 0)
    in_specs = [
        pl.BlockSpec((tm, d), lambda i: (i, 0)),
        pl.BlockSpec((1, 1, d), lambda i: ((i // per_group) * 6 + sc_j, 0, 0)),
        pl.BlockSpec((1, 1, d), lambda i: ((i // per_group) * 6 + sh_j, 0, 0)),
        pl.BlockSpec((1, d), const2),
        pl.BlockSpec(w_in.shape, const2),
    ]
    args = [x, mod, mod, norm_g, w_in]
    if with_q:
        in_specs += [pl.BlockSpec(q_g.shape, const2), pl.BlockSpec(w_uq.shape, const2)]
        args += [q_g, w_uq]
    in_specs += [pl.BlockSpec(kv_g.shape, const2), pl.BlockSpec(w_uk.shape, const2),
                 pl.BlockSpec(w_uv.shape, const2)]
    args += [kv_g, w_uk, w_uv]
    if use_rope:
        s_tiles = rope[0].shape[0] // tm
        for t in rope:
            in_specs.append(pl.BlockSpec((tm, LANES), lambda i: (i % s_tiles, 0)))
            args.append(t)
    kw = MLA_HEADS * MLA_QK_PAD
    vw = MLA_HEADS * MLA_V
    out_shape = [jax.ShapeDtypeStruct((m, kw), BF16), jax.ShapeDtypeStruct((m, vw), BF16)]
    out_specs = [pl.BlockSpec((tm, kw), lambda i: (i, 0)), pl.BlockSpec((tm, vw), lambda i: (i, 0))]
    if with_q:
        out_shape = [jax.ShapeDtypeStruct((m, kw), BF16)] + out_shape
        out_specs = [pl.BlockSpec((tm, kw), lambda i: (i, 0))] + out_specs
    return pl.pallas_call(
        functools.partial(_proj_mla_kernel, with_q=with_q, use_rope=use_rope, sub=1),
        out_shape=tuple(out_shape),
        grid=(m // tm,),
        in_specs=in_specs,
        out_specs=tuple(out_specs),
        compiler_params=_cparams("parallel"),
        name="proj_mla",
    )(*args)


def _attn_kernel(*refs, n_seg, tq, n_q):
    q_ref = refs[0]
    kv = refs[1:1 + 2 * n_seg]
    o_ref = refs[1 + 2 * n_seg]

    for c in range(n_q):
        q = q_ref[c * tq:(c + 1) * tq, :]
        ss = [_dot_nt(q, kv[2 * j][...]) for j in range(n_seg)]
        mx = ss[0].max(axis=-1, keepdims=True)
        for s in ss[1:]:
            mx = jnp.maximum(mx, s.max(axis=-1, keepdims=True))
        acc = None
        den = None
        for j, s in enumerate(ss):
            e = jnp.exp(s - mx)
            d = e.sum(axis=-1, keepdims=True)
            o = _dot(e.astype(BF16), kv[2 * j + 1][...])
            acc = o if acc is None else acc + o
            den = d if den is None else den + d
        o_ref[c * tq:(c + 1) * tq, :] = (acc / den).astype(o_ref.dtype)


def _attention(q, segs, batch, heads, kv_heads, dk, dv, tq):
    t_q = q.shape[0] // batch
    group = heads // kv_heads
    in_specs = [pl.BlockSpec((t_q, dk), lambda b, h: (b, h))]
    args = [q]
    for k, v in segs:
        t_k = k.shape[0] // batch
        in_specs.append(pl.BlockSpec((t_k, dk), lambda b, h: (b, h // group)))
        in_specs.append(pl.BlockSpec((t_k, dv), lambda b, h: (b, h // group)))
        args += [k, v]
    tq = min(tq, t_q)
    return pl.pallas_call(
        functools.partial(_attn_kernel, n_seg=len(segs), tq=tq, n_q=t_q // tq),
        out_shape=jax.ShapeDtypeStruct((q.shape[0], heads * dv), BF16),
        grid=(batch, heads),
        in_specs=in_specs,
        out_specs=pl.BlockSpec((t_q, dv), lambda b, h: (b, h)),
        compiler_params=_cparams("parallel", "parallel"),
        name="attention",
    )(*args)


def _out_router_kernel(*refs, n_in, tm, sub):
    ins = refs[:2 * n_in]
    (x_ref, g1_ref, sc_ref, sh_ref, ng_ref, rw_ref, rb_ref, cnt0_ref,
     xo_ref, h2_ref, idx_ref, gate_ref, cnt_ref) = refs[2 * n_in:]
    i = pl.program_id(0)

    @pl.when(i == 0)
    def _():
        cnt_ref[...] = cnt0_ref[...]

    ts = tm // sub
    w_hi, w_lo = _split2(rw_ref[...])
    for b in range(sub):
        rows = slice(b * ts, (b + 1) * ts)
        y = None
        for j in range(n_in):
            t = _dot(ins[2 * j][rows, :], ins[2 * j + 1][...])
            y = t if y is None else y + t
        xn = x_ref[rows, :] + g1_ref[0] * y
        xo_ref[rows, :] = xn
        h2 = _rms(xn, ng_ref[...]) * (1.0 + sc_ref[0]) + sh_ref[0]
        _store_token_tiles(h2_ref.at[pl.ds(b * ts * TOKEN_ROWS, ts * TOKEN_ROWS)], h2)

        h_hi, h_lo = _split2(h2)
        logits = _dot_nt(w_hi, h_hi) + _dot_nt(w_hi, h_lo) + _dot_nt(w_lo, h_hi) + rb_ref[...]

        e_iota = lax.broadcasted_iota(jnp.int32, logits.shape, 0)
        work = logits
        tops, idxs = [], []
        for _ in range(TOP_K):
            mx = work.max(axis=0, keepdims=True)
            ix = jnp.where(work == mx, e_iota, N_EXPERTS).min(axis=0, keepdims=True)
            tops.append(mx)
            idxs.append(ix)
            work = jnp.where(e_iota == ix, -jnp.inf, work)
        exps = [jnp.exp(t - tops[0]) for t in tops]
        den = exps[0] + exps[1] + exps[2] + exps[3]

        for k in range(TOP_K):
            idx_ref[k:k + 1, rows] = idxs[k]
            gate_ref[k:k + 1, rows] = exps[k] / den
        cnt = sum((e_iota == ix).astype(F32) for ix in idxs)
        cnt_ref[...] = cnt_ref[...] + cnt.sum(axis=1, keepdims=True)


def _out_router(ins, x, mod, mod_rows, group_rows, norm_g, router_wt, router_b, cnt0, tm):
    m, d = x.shape
    per_group = group_rows // tm
    g1_j, sc_j, sh_j = mod_rows
    in_specs, args = [], []
    for a, w in ins:
        in_specs += [pl.BlockSpec((tm, a.shape[1]), lambda i: (i, 0)), pl.BlockSpec(w.shape, lambda i: (0, 0))]
        args += [a, w]
    mod_spec = lambda j: pl.BlockSpec((1, 1, d), lambda i: ((i // per_group) * 6 + j, 0, 0))
    in_specs += [
        pl.BlockSpec((tm, d), lambda i: (i, 0)),
        mod_spec(g1_j), mod_spec(sc_j), mod_spec(sh_j),
        pl.BlockSpec((1, d), lambda i: (0, 0)),
        pl.BlockSpec(router_wt.shape, lambda i: (0, 0)),
        pl.BlockSpec(router_b.shape, lambda i: (0, 0)),
        pl.BlockSpec(cnt0.shape, lambda i: (0, 0)),
    ]
    args += [x, mod, mod, mod, norm_g, router_wt, router_b, cnt0]
    small = lambda dt: jax.ShapeDtypeStruct((TOP_K, m), dt)
    small_spec = pl.BlockSpec((TOP_K, tm), lambda i: (0, i))
    return pl.pallas_call(
        functools.partial(_out_router_kernel, n_in=len(ins), tm=tm, sub=1),
        out_shape=(jax.ShapeDtypeStruct((m, d), F32), jax.ShapeDtypeStruct((m * TOKEN_ROWS, LANES), U32),
                   small(jnp.int32), small(F32),
                   jax.ShapeDtypeStruct(cnt0.shape, F32)),
        grid=(m // tm,),
        in_specs=in_specs,
        out_specs=(pl.BlockSpec((tm, d), lambda i: (i, 0)),
                   pl.BlockSpec((tm * TOKEN_ROWS, LANES), lambda i: (i, 0)),
                   small_spec, small_spec,
                   pl.BlockSpec(cnt0.shape, lambda i: (0, 0))),
        compiler_params=_cparams("arbitrary"),
        name="out_router",
    )(*args)


def _moe_kernel(blk_e_ref, meta_ref, tok0_ref, tokn_ref, dstp_ref, dstc_ref, h_hbm, w1_ref, b1_ref, w2_ref, b2_ref,
                o_hbm, x0, x1, y0, y1, w1b, w2b, gsem, ssem, *, tm):
    i = pl.program_id(0)
    n_used = meta_ref[0]

    @pl.when(jnp.logical_and(i < n_used, meta_ref[1 + i] == 1))
    def _():
        w1b[...] = w1_ref[0].astype(BF16)
        w2b[...] = w2_ref[0].astype(BF16)
    xs, ys = (x0, x1), (y0, y1)

    def tile_at(ref, row):
        return ref.at[pl.ds(pl.multiple_of(row, TOKEN_ROWS), TOKEN_ROWS)]

    def for_tokens(fn):
        def body(b, carry):
            for j in range(DMA_UNROLL):
                fn(b * DMA_UNROLL + j)
            return carry
        lax.fori_loop(0, tm // DMA_UNROLL, body, 0)

    def gather(tok_ref, xbuf, sem):
        for_tokens(lambda r: pltpu.make_async_copy(
            tile_at(h_hbm, tok_ref[0, 0, r]), tile_at(xbuf, r * TOKEN_ROWS), sem).start())

    def scatter(dst_ref, ybuf, sem):
        for_tokens(lambda r: pltpu.make_async_copy(
            tile_at(ybuf, r * TOKEN_ROWS), tile_at(o_hbm, dst_ref[0, 0, r]), sem).start(priority=1))

    def wait_gather(xbuf, sem):
        pltpu.make_async_copy(h_hbm.at[pl.ds(0, tm * TOKEN_ROWS)], xbuf, sem).wait()

    def wait_scatter(ybuf, sem):
        pltpu.make_async_copy(ybuf, o_hbm.at[pl.ds(0, tm * TOKEN_ROWS)], sem).wait()

    @pl.when(i == 0)
    def _():
        gather(tok0_ref, x0, gsem.at[0])
        y1[...] = jnp.zeros_like(y1)

    def step(p):
        xb, xn, yb, yp = xs[p], xs[1 - p], ys[p], ys[1 - p]
        wait_gather(xb, gsem.at[p])

        @pl.when(i >= 1)
        def _():
            wait_scatter(yb, ssem.at[p])

        @pl.when(n_used > 0)
        def _():
            scatter(dstp_ref, yp, ssem.at[1 - p])
            gather(tokn_ref, xn, gsem.at[1 - p])

        x = _load_token_tiles(xb).astype(BF16)
        hid = _dot(x, w1b[...]) + b1_ref[0]
        f = hid.shape[1] // 2
        g = jnp.minimum(hid[:, :f], SWIGLU_LIMIT)
        lin = jnp.clip(hid[:, f:], -SWIGLU_LIMIT, SWIGLU_LIMIT)
        act = g * jax.nn.sigmoid(SWIGLU_ALPHA * g) * (lin + 1.0)
        _store_token_tiles(yb, _dot(act.astype(BF16), w2b[...]) + b2_ref[0])

        @pl.when(i == n_used - 1)
        def _():
            wait_scatter(yp, ssem.at[1 - p])
            scatter(dstc_ref, yb, ssem.at[p])
            wait_scatter(yb, ssem.at[p])
            wait_gather(xn, gsem.at[1 - p])

    for p in range(2):
        @pl.when(jnp.logical_and(i < n_used, i % 2 == p))
        def _(p=p):
            step(p)


def _moe(h2, row_tok, row_dst, blk_e, n_used, w1, b1, w2, b2, out_rows, tm):
    n_tiles = row_tok.shape[0]
    d = w1.shape[1]
    f2 = w1.shape[2]
    tile_rows = (tm * TOKEN_ROWS, LANES)
    smem = lambda fn: pl.BlockSpec((1, 1, tm), fn, memory_space=pltpu.SMEM)
    grid_spec = pltpu.PrefetchScalarGridSpec(
        num_scalar_prefetch=2,
        grid=(n_tiles,),
        in_specs=[
            smem(lambda i, be, nu: (0, 0, 0)),
            smem(lambda i, be, nu: (jnp.minimum(i + 1, nu[0] - 1), 0, 0)),
            smem(lambda i, be, nu: (i, 0, 0)),
            smem(lambda i, be, nu: (i + 1, 0, 0)),
            pl.BlockSpec(memory_space=pl.ANY),
            pl.BlockSpec((1, d, f2), lambda i, be, nu: (be[i], 0, 0)),
            pl.BlockSpec((1, 1, f2), lambda i, be, nu: (be[i], 0, 0)),
            pl.BlockSpec((1, f2 // 2, d), lambda i, be, nu: (be[i], 0, 0)),
            pl.BlockSpec((1, 1, d), lambda i, be, nu: (be[i], 0, 0)),
        ],
        out_specs=pl.BlockSpec(memory_space=pl.ANY),
        scratch_shapes=[pltpu.VMEM(tile_rows, U32), pltpu.VMEM(tile_rows, U32),
                        pltpu.VMEM(tile_rows, U32), pltpu.VMEM(tile_rows, U32),
                        pltpu.VMEM((d, f2), BF16), pltpu.VMEM((f2 // 2, d), BF16),
                        pltpu.SemaphoreType.DMA((2,)), pltpu.SemaphoreType.DMA((2,))],
    )
    return pl.pallas_call(
        functools.partial(_moe_kernel, tm=tm),
        out_shape=jax.ShapeDtypeStruct((out_rows * TOKEN_ROWS, LANES), U32),
        grid_spec=grid_spec,
        compiler_params=_cparams("arbitrary"),
        name="moe_experts",
    )(blk_e, n_used, row_tok, row_tok, row_dst, row_dst, h2, w1, b1, w2, b2)


def _combine_kernel(*refs, tq, final):
    ys = refs[:TOP_K]
    if final:
        gate_ref, x_ref, g2_ref, fg_ref, o_ref = refs[TOP_K:]
    else:
        gate_ref, x_ref, g2_ref, o_ref = refs[TOP_K:]
    f = None
    d = x_ref.shape[1]
    for k in range(TOP_K):
        gcol = jnp.broadcast_to(gate_ref[k:k + 1, :], (LANES, tq)).T
        t = _load_token_tiles(ys[k]) * jnp.tile(gcol, (1, d // LANES))
        f = t if f is None else f + t
    xn = x_ref[...] + g2_ref[0] * f
    if final:
        xn = _rms(xn, fg_ref[...])
    o_ref[...] = xn


def _combine(ys, t_tot, t_off, gate, x, mod, g2_j, group_rows, final_g, tq):
    m, d = x.shape
    per_group = group_rows // tq
    final = final_g is not None
    in_specs = [pl.BlockSpec((tq * TOKEN_ROWS, LANES),
                             functools.partial(lambda i, k: ((k * t_tot + t_off) // tq + i, 0), k=k))
                for k in range(TOP_K)]
    in_specs += [
        pl.BlockSpec((TOP_K, tq), lambda i: (0, i)),
        pl.BlockSpec((tq, d), lambda i: (i, 0)),
        pl.BlockSpec((1, 1, d), lambda i: ((i // per_group) * 6 + g2_j, 0, 0)),
    ]
    args = [ys] * TOP_K + [gate, x, mod]
    if final:
        in_specs.append(pl.BlockSpec((1, d), lambda i: (0, 0)))
        args.append(final_g)
    return pl.pallas_call(
        functools.partial(_combine_kernel, tq=tq, final=final),
        out_shape=jax.ShapeDtypeStruct((m, d), F32),
        grid=(m // tq,),
        in_specs=in_specs,
        out_specs=pl.BlockSpec((tq, d), lambda i: (i, 0)),
        compiler_params=_cparams("parallel"),
        name="moe_combine",
    )(*args)


def _rope_tables(seq, rot_dim):
    t = np.arange(seq)
    row = (t // GRID_W).astype(np.float32)
    col = (t % GRID_W).astype(np.float32)
    axis_dim = rot_dim // 2
    quarter = axis_dim // 2
    inv = jnp.asarray(ROPE_THETA, F32) ** (-jnp.arange(0, axis_dim, 2, dtype=F32) / axis_dim)
    ang_r = jnp.asarray(row)[:, None] * inv
    ang_c = jnp.asarray(col)[:, None] * inv
    zeros = jnp.zeros_like(ang_r)
    cos = jnp.concatenate([jnp.cos(ang_r)] * 2 + [jnp.cos(ang_c)] * 2, axis=1)
    s_a = jnp.concatenate([-jnp.sin(ang_r), zeros, -jnp.sin(ang_c), zeros], axis=1)
    s_b = jnp.concatenate([zeros, jnp.sin(ang_r), zeros, jnp.sin(ang_c)], axis=1)
    pad = LANES - rot_dim
    if pad:
        cos, s_a, s_b = (jnp.pad(a, ((0, 0), (0, pad))) for a in (cos, s_a, s_b))
    del quarter
    return cos, s_a, s_b


def _route(idx, counts, tm):
    t_tot = idx.shape[1]
    n_assign = t_tot * TOP_K
    n_rows = n_assign + N_EXPERTS * tm
    n_tiles = n_rows // tm
    counts = counts.astype(jnp.int32)
    pad_need = (-counts) % tm
    flat_e = idx.T.reshape(-1)
    slot = jnp.arange(tm, dtype=jnp.int32)[None, :]
    pad_keys = jnp.where(slot < pad_need[:, None], jnp.arange(N_EXPERTS, dtype=jnp.int32)[:, None], N_EXPERTS)
    keys = jnp.concatenate([flat_e, pad_keys.reshape(-1)])
    vals = jnp.concatenate([jnp.arange(n_assign, dtype=jnp.int32), jnp.full((N_EXPERTS * tm,), -1, jnp.int32)])
    skeys, svals = lax.sort((keys, vals), num_keys=1, is_stable=True)
    real = svals >= 0
    row_tok = jnp.where(real, svals // TOP_K, 0)
    rows = jnp.arange(n_rows, dtype=jnp.int32)
    row_dst = jnp.where(real, (svals % TOP_K) * t_tot + svals // TOP_K, n_assign + rows % tm)
    tile_e = skeys[::tm]
    n_used = jnp.sum(tile_e < N_EXPERTS).astype(jnp.int32)
    last_e = tile_e[jnp.maximum(n_used - 1, 0)]
    blk_e = jnp.where(tile_e < N_EXPERTS, tile_e, last_e).astype(jnp.int32)
    row_dst = jnp.concatenate([n_assign + jnp.arange(tm, dtype=jnp.int32), row_dst])
    row_tok = (row_tok * TOKEN_ROWS).reshape(n_tiles, 1, tm)
    row_dst = (row_dst * TOKEN_ROWS).reshape(n_tiles + 1, 1, tm)
    first = jnp.concatenate([jnp.ones((1,), jnp.int32), (blk_e[1:] != blk_e[:-1]).astype(jnp.int32)])
    meta = jnp.concatenate([n_used.reshape(1), first])
    return row_tok, row_dst, blk_e, meta


def _pick(m, pref):
    t = pref
    while m % t:
        t //= 2
    return t


def kernel(x, c, ctx, c_ctx, ada_w, ada_b, norm_mix_g, norm_ffn_g, ab_w_in, sgu_norm_g, sgu_w, sgu_b,
           gqa_q_norm_g, gqa_k_norm_g, ab_w_out, mla_w_in, mla_q_norm_g, mla_kv_norm_g, mla_w_uq, mla_w_ukv,
           mla_w_out, router_w, router_b, moe_w1, moe_b1, moe_w2, moe_b2, final_norm_g):
    bn, s, d = x.shape
    l = ctx.shape[1]
    depth = ada_w.shape[0]
    m_lat, m_ctx = bn * s, bn * l
    tm_lat = _pick(s, 512)
    tm_ctx = _pick(l, 512)
    tm_moe = 512
    tq_comb = _pick(int(np.gcd(s, m_ctx)), 512)

    n_cond = (bn + 1 + 7) // 8 * 8
    cond = jnp.concatenate([c, c_ctx[None, :], jnp.zeros((n_cond - bn - 1, d), F32)], axis=0)
    mod_all = _ada(cond, ada_w, ada_b)

    rope_gqa = _rope_tables(s, HEAD_DIM)
    rope_mla = _rope_tables(s, MLA_ROPE)
    w1_all = moe_w1.reshape(depth * N_EXPERTS, d, -1)
    w2_all = moe_w2.reshape(depth * N_EXPERTS, -1, d)
    b1_all = moe_b1.reshape(depth * N_EXPERTS, 1, -1)
    b2_all = moe_b2.reshape(depth * N_EXPERTS, 1, d)

    xl = x.reshape(m_lat, d)
    xc = ctx.reshape(m_ctx, d)
    for i in range(depth):
        ctx_out = i < depth - 1
        j = i // 2
        mod_lat = mod_all[i, :bn].reshape(bn * 6, 1, d)
        mod_ctx = mod_all[i, bn].reshape(6, 1, d)
        ng = norm_mix_g[i].reshape(1, d)
        if i % 2 == 0:
            w_in = ab_w_in[j].astype(BF16)
            w_out = ab_w_out[j].astype(BF16)
            sgu_g = sgu_norm_g[j].reshape(1, SGU_WIDTH)
            sgu_wb = sgu_w[j].astype(BF16)
            sgu_bfull = jnp.broadcast_to(sgu_b[j][:, :, None], (SGU_GROUPS, CHUNK, SGU_GROUP_CH))
            q_g = gqa_q_norm_g[j].reshape(1, HEAD_DIM)
            k_g = gqa_k_norm_g[j].reshape(1, HEAD_DIM)
            oa, q, k, v = _proj_ab(xl, mod_lat, (1, 0), s, ng, w_in, sgu_g, sgu_wb, sgu_bfull, q_g, k_g,
                                   rope_gqa, tm_lat)
            oac, qc, kc, vc = _proj_ab(xc, mod_ctx, (1, 0), m_ctx, ng, w_in, sgu_g, sgu_wb, sgu_bfull, q_g, k_g,
                                       None, tm_ctx)
            ob = _attention(q, [(kc, vc), (k, v)], bn, GQA_HEADS, GQA_KV_HEADS, HEAD_DIM, HEAD_DIM, ATTN_TQ)
            ins = [(oa, w_out[:SGU_WIDTH]), (ob, w_out[SGU_WIDTH:])]
            if ctx_out:
                obc = _attention(qc, [(kc, vc)], bn, GQA_HEADS, GQA_KV_HEADS, HEAD_DIM, HEAD_DIM, ATTN_TQ)
                ins_c = [(oac, w_out[:SGU_WIDTH]), (obc, w_out[SGU_WIDTH:])]
        else:
            w_in_f = mla_w_in[j]
            w_in = jnp.pad(w_in_f, ((0, 0), (0, MLA_IN_PAD - w_in_f.shape[1]))).astype(BF16)
            w_in_kv = w_in[:, MLA_Q_LORA:]
            w_out = mla_w_out[j].astype(BF16)
            q_g = mla_q_norm_g[j].reshape(1, MLA_Q_LORA)
            kv_g = mla_kv_norm_g[j].reshape(1, MLA_KV_LORA)
            w_uq = mla_w_uq[j].reshape(MLA_Q_LORA, MLA_HEADS, MLA_QK)
            w_uq = jnp.pad(w_uq, ((0, 0), (0, 0), (0, MLA_QK_PAD - MLA_QK)))
            w_uq = w_uq.reshape(MLA_Q_LORA, MLA_HEADS * MLA_QK_PAD).astype(BF16)
            w_ukv = mla_w_ukv[j].reshape(MLA_KV_LORA, MLA_HEADS, MLA_NOPE + MLA_V)
            w_uk = w_ukv[:, :, :MLA_NOPE].reshape(MLA_KV_LORA, MLA_HEADS * MLA_NOPE).astype(BF16)
            w_uv = w_ukv[:, :, MLA_NOPE:].reshape(MLA_KV_LORA, MLA_HEADS * MLA_V).astype(BF16)
            q, k, v = _proj_mla(xl, mod_lat, (1, 0), s, ng, w_in, q_g, w_uq, kv_g, w_uk, w_uv, rope_mla, True,
                                tm_lat)
            if ctx_out:
                qc, kc, vc = _proj_mla(xc, mod_ctx, (1, 0), m_ctx, ng, w_in, q_g, w_uq, kv_g, w_uk, w_uv, None,
                                       True, tm_ctx)
            else:
                kc, vc = _proj_mla(xc, mod_ctx, (1, 0), m_ctx, ng, w_in_kv, None, None, kv_g, w_uk, w_uv, None,
                                   False, tm_ctx)
            ob = _attention(q, [(kc, vc), (k, v)], bn, MLA_HEADS, MLA_HEADS, MLA_QK_PAD, MLA_V, ATTN_TQ)
            ins = [(ob, w_out)]
            if ctx_out:
                obc = _attention(qc, [(kc, vc)], bn, MLA_HEADS, MLA_HEADS, MLA_QK_PAD, MLA_V, ATTN_TQ)
                ins_c = [(obc, w_out)]

        nfg = norm_ffn_g[i].reshape(1, d)
        rwt = router_w[i].T
        rb = router_b[i].reshape(N_EXPERTS, 1)
        cnt0 = jnp.zeros((N_EXPERTS, LANES), F32)
        xl, h2, idx, gate, cnt = _out_router(ins, xl, mod_lat, (2, 4, 3), s, nfg, rwt, rb, cnt0, tm_lat)
        if ctx_out:
            xc, h2c, idx_c, gate_c, cnt = _out_router(ins_c, xc, mod_ctx, (2, 4, 3), m_ctx, nfg, rwt, rb, cnt, tm_ctx)
            h2 = jnp.concatenate([h2, h2c], axis=0)
            idx = jnp.concatenate([idx, idx_c], axis=1)
        t_tot = h2.shape[0] // TOKEN_ROWS
        n_assign = t_tot * TOP_K
        row_tok, row_dst, blk_e, n_used = _route(idx, cnt[:, 0], tm_moe)
        ys = _moe(h2, row_tok, row_dst, blk_e + i * N_EXPERTS, n_used, w1_all, b1_all, w2_all, b2_all,
                  n_assign + tm_moe, tm_moe)
        last = i == depth - 1
        fg = final_norm_g.reshape(1, d) if last else None
        xl = _combine(ys, t_tot, 0, gate, xl, mod_lat, 5, s, fg, tq_comb)
        if ctx_out:
            xc = _combine(ys, t_tot, m_lat, gate_c, xc, mod_ctx, 5, m_ctx, None, tq_comb)
    return xl.reshape(bn, s, d)
```

```python
import functools

import jax
import jax.numpy as jnp
import numpy as np
from jax import lax
from jax.experimental import pallas as pl
from jax.experimental.pallas import tpu as pltpu

F32 = jnp.float32
BF16 = jnp.bfloat16
U32 = jnp.uint32

GRID_W = 64
EPS = 1e-6
ROPE_THETA = 10000.0
CHUNK = 128
SGU_GROUPS = 4
SGU_GROUP_CH = 128
SGU_WIDTH = SGU_GROUPS * SGU_GROUP_CH
GQA_HEADS = 4
GQA_KV_HEADS = 2
HEAD_DIM = 128
GQA_Q_W = GQA_HEADS * HEAD_DIM
GQA_KV_W = GQA_KV_HEADS * HEAD_DIM
A_END = 2 * SGU_WIDTH
K_OFF = A_END + GQA_Q_W
V_OFF = K_OFF + GQA_KV_W
MLA_HEADS = 8
MLA_Q_LORA = 512
MLA_KV_LORA = 256
MLA_NOPE = 128
MLA_ROPE = 64
MLA_V = 128
MLA_QK = MLA_NOPE + MLA_ROPE
MLA_QK_PAD = 256
MLA_IN_PAD = 896
N_EXPERTS = 32
TOP_K = 4
D_EXPERT = 1024
SWIGLU_LIMIT = 7.0
SWIGLU_ALPHA = 1.702

LANES = 128
TOKEN_ROWS = 4
DMA_UNROLL = 64
ROW_SUB = 2
ATTN_TQ = 256
VMEM_LIMIT = 56 * 1024 * 1024


def _cparams(*sem):
    return pltpu.CompilerParams(dimension_semantics=sem, vmem_limit_bytes=VMEM_LIMIT)


def _dot(a, b):
    return jnp.dot(a, b, preferred_element_type=F32)


def _dot_nt(a, b):
    return lax.dot_general(a, b, (((1,), (1,)), ((), ())), preferred_element_type=F32)


def _split2(a):
    hi = a.astype(BF16)
    lo = (a - hi.astype(F32)).astype(BF16)
    return hi, lo


def _row_sub(tm):
    return ROW_SUB if tm % (ROW_SUB * CHUNK) == 0 else 1


def _rms(x, g):
    return x * lax.rsqrt(jnp.mean(x * x, axis=-1, keepdims=True) + EPS) * g


def _store_token_tiles(ref, val):
    n, d = val.shape
    h = d // 2
    lo = lax.bitcast_convert_type(val[:, :h].astype(BF16).astype(F32), U32) >> 16
    hi = lax.bitcast_convert_type(val[:, h:].astype(BF16).astype(F32), U32) & jnp.uint32(0xFFFF0000)
    packed = hi | lo
    for s in range(TOKEN_ROWS):
        ref[pl.ds(s, n, stride=TOKEN_ROWS), :] = packed[:, s * LANES:(s + 1) * LANES]


def _load_token_tiles(ref):
    n = ref.shape[0] // TOKEN_ROWS
    packed = jnp.concatenate([ref[pl.ds(s, n, stride=TOKEN_ROWS), :] for s in range(TOKEN_ROWS)], axis=1)
    lo = lax.bitcast_convert_type(packed << 16, F32)
    hi = lax.bitcast_convert_type(packed & jnp.uint32(0xFFFF0000), F32)
    return jnp.concatenate([lo, hi], axis=1)


def _rope(x, c, s_a, s_b, half):
    n = x.shape[-1]
    return x * c + pltpu.roll(x, n - half, 1) * s_a + pltpu.roll(x, half, 1) * s_b


def _ada_kernel(c_ref, w_ref, b_ref, o_ref):
    c = c_ref[...]
    sc = c * jax.nn.sigmoid(c)
    c_hi, c_lo = _split2(sc)
    w = w_ref[0]
    w_hi, w_lo = _split2(w)
    acc = _dot(c_hi, w_hi) + _dot(c_hi, w_lo) + _dot(c_lo, w_hi)
    o_ref[0] = acc + b_ref[0]


def _ada(cond, ada_w, ada_b):
    depth, d, n = ada_w.shape
    rows = cond.shape[0]
    tn = 1536
    return pl.pallas_call(
        _ada_kernel,
        out_shape=jax.ShapeDtypeStruct((depth, rows, n), F32),
        grid=(depth, n // tn),
        in_specs=[
            pl.BlockSpec((rows, d), lambda l, j: (0, 0)),
            pl.BlockSpec((1, d, tn), lambda l, j: (l, 0, j)),
            pl.BlockSpec((1, 1, tn), lambda l, j: (l, 0, j)),
        ],
        out_specs=pl.BlockSpec((1, rows, tn), lambda l, j: (l, 0, j)),
        compiler_params=_cparams("arbitrary", "arbitrary"),
        name="ada_mod",
    )(cond, ada_w, ada_b.reshape(depth, 1, n))


def _proj_ab_kernel(*refs, use_rope, tm, sub):
    if use_rope:
        (x_ref, sc_ref, sh_ref, g_ref, w_ref, sgug_ref, sguw_ref, sgub_ref, qg_ref, kg_ref,
         c_ref, sa_ref, sb_ref, oa_ref, q_ref, k_ref, v_ref) = refs
    else:
        (x_ref, sc_ref, sh_ref, g_ref, w_ref, sgug_ref, sguw_ref, sgub_ref, qg_ref, kg_ref,
         oa_ref, q_ref, k_ref, v_ref) = refs
    ts = tm // sub
    n_chunks = ts // CHUNK
    qg = qg_ref[...] * (HEAD_DIM ** -0.5)
    for b in range(sub):
        rows = slice(b * ts, (b + 1) * ts)
        x = x_ref[rows, :]
        h = _rms(x, g_ref[...]) * (1.0 + sc_ref[0]) + sh_ref[0]
        p = _dot(h.astype(BF16), w_ref[...])
        rope = (lambda t: _rope(t, c_ref[rows, :], sa_ref[rows, :], sb_ref[rows, :], HEAD_DIM // 4)) \
            if use_rope else (lambda t: t)

        for g in range(SGU_GROUPS):
            u = jax.nn.gelu(p[:, g * SGU_GROUP_CH:(g + 1) * SGU_GROUP_CH])
            v = jax.nn.gelu(p[:, SGU_WIDTH + g * SGU_GROUP_CH:SGU_WIDTH + (g + 1) * SGU_GROUP_CH])
            v = _rms(v, sgug_ref[:, g * SGU_GROUP_CH:(g + 1) * SGU_GROUP_CH]).astype(BF16)
            vcat = jnp.concatenate([v[c * CHUNK:(c + 1) * CHUNK, :] for c in range(n_chunks)], axis=1)
            mixed = _dot(sguw_ref[g], vcat)
            bias = sgub_ref[g]
            for c in range(n_chunks):
                gate = mixed[:, c * SGU_GROUP_CH:(c + 1) * SGU_GROUP_CH] + bias
                oa_ref[b * ts + c * CHUNK:b * ts + (c + 1) * CHUNK, g * SGU_GROUP_CH:(g + 1) * SGU_GROUP_CH] = (
                    u[c * CHUNK:(c + 1) * CHUNK, :] * gate).astype(oa_ref.dtype)

        for hd in range(GQA_HEADS):
            qh = rope(_rms(p[:, A_END + hd * HEAD_DIM:A_END + (hd + 1) * HEAD_DIM], qg))
            q_ref[rows, hd * HEAD_DIM:(hd + 1) * HEAD_DIM] = qh.astype(q_ref.dtype)
        for hd in range(GQA_KV_HEADS):
            kh = rope(_rms(p[:, K_OFF + hd * HEAD_DIM:K_OFF + (hd + 1) * HEAD_DIM], kg_ref[...]))
            k_ref[rows, hd * HEAD_DIM:(hd + 1) * HEAD_DIM] = kh.astype(k_ref.dtype)
        v_ref[rows, :] = p[:, V_OFF:].astype(v_ref.dtype)


def _proj_ab(x, mod, mod_rows, group_rows, norm_g, w_in, sgu_g, sgu_w, sgu_bfull, q_g, k_g, rope, tm):
    m, d = x.shape
    per_group = group_rows // tm
    sc_j, sh_j = mod_rows
    use_rope = rope is not None
    in_specs = [
        pl.BlockSpec((tm, d), lambda i: (i, 0)),
        pl.BlockSpec((1, 1, d), lambda i: ((i // per_group) * 6 + sc_j, 0, 0)),
        pl.BlockSpec((1, 1, d), lambda i: ((i // per_group) * 6 + sh_j, 0, 0)),
        pl.BlockSpec((1, d), lambda i: (0, 0)),
        pl.BlockSpec(w_in.shape, lambda i: (0, 0)),
        pl.BlockSpec((1, SGU_WIDTH), lambda i: (0, 0)),
        pl.BlockSpec(sgu_w.shape, lambda i: (0, 0, 0)),
        pl.BlockSpec(sgu_bfull.shape, lambda i: (0, 0, 0)),
        pl.BlockSpec((1, HEAD_DIM), lambda i: (0, 0)),
        pl.BlockSpec((1, HEAD_DIM), lambda i: (0, 0)),
    ]
    args = [x, mod, mod, norm_g, w_in, sgu_g, sgu_w, sgu_bfull, q_g, k_g]
    if use_rope:
        s_tiles = rope[0].shape[0] // tm
        for t in rope:
            in_specs.append(pl.BlockSpec((tm, HEAD_DIM), lambda i: (i % s_tiles, 0)))
            args.append(t)
    return pl.pallas_call(
        functools.partial(_proj_ab_kernel, use_rope=use_rope, tm=tm, sub=_row_sub(tm)),
        out_shape=(jax.ShapeDtypeStruct((m, SGU_WIDTH), BF16),
                   jax.ShapeDtypeStruct((m, GQA_Q_W), BF16),
                   jax.ShapeDtypeStruct((m, GQA_KV_W), BF16),
                   jax.ShapeDtypeStruct((m, GQA_KV_W), BF16)),
        grid=(m // tm,),
        in_specs=in_specs,
        out_specs=(pl.BlockSpec((tm, SGU_WIDTH), lambda i: (i, 0)),
                   pl.BlockSpec((tm, GQA_Q_W), lambda i: (i, 0)),
                   pl.BlockSpec((tm, GQA_KV_W), lambda i: (i, 0)),
                   pl.BlockSpec((tm, GQA_KV_W), lambda i: (i, 0))),
        compiler_params=_cparams("parallel"),
        name="proj_ab",
    )(*args)


def _proj_mla_kernel(*refs, with_q, use_rope, sub):
    it = iter(refs)
    x_ref, sc_ref, sh_ref, g_ref, w_ref = (next(it) for _ in range(5))
    if with_q:
        qg_ref, wuq_ref = next(it), next(it)
    kvg_ref, wuk_ref, wuv_ref = next(it), next(it), next(it)
    if use_rope:
        c_ref, sa_ref, sb_ref = next(it), next(it), next(it)
    if with_q:
        q_ref = next(it)
    k_ref, v_ref = next(it), next(it)

    ts = x_ref.shape[0] // sub
    off = MLA_Q_LORA if with_q else 0
    for b in range(sub):
        rows = slice(b * ts, (b + 1) * ts)
        rope = (lambda t: _rope(t, c_ref[rows, :], sa_ref[rows, :], sb_ref[rows, :], MLA_ROPE // 4)) \
            if use_rope else (lambda t: t)
        x = x_ref[rows, :]
        h = _rms(x, g_ref[...]) * (1.0 + sc_ref[0]) + sh_ref[0]
        p = _dot(h.astype(BF16), w_ref[...])
        ckv = _rms(p[:, off:off + MLA_KV_LORA], kvg_ref[...]).astype(BF16)
        kr = rope(p[:, off + MLA_KV_LORA:off + MLA_KV_LORA + LANES])
        kr = kr.astype(k_ref.dtype)
        k_nope = _dot(ckv, wuk_ref[...])
        v_ref[rows, :] = _dot(ckv, wuv_ref[...]).astype(v_ref.dtype)
        for hd in range(MLA_HEADS):
            k_ref[rows, hd * MLA_QK_PAD:hd * MLA_QK_PAD + MLA_NOPE] = (
                k_nope[:, hd * MLA_NOPE:(hd + 1) * MLA_NOPE].astype(k_ref.dtype))
            k_ref[rows, hd * MLA_QK_PAD + MLA_NOPE:(hd + 1) * MLA_QK_PAD] = kr
        if with_q:
            cq = _rms(p[:, :MLA_Q_LORA], qg_ref[...]).astype(BF16)
            q = _dot(cq, wuq_ref[...]) * (MLA_QK ** -0.5)
            for hd in range(MLA_HEADS):
                q_ref[rows, hd * MLA_QK_PAD:hd * MLA_QK_PAD + MLA_NOPE] = (
                    q[:, hd * MLA_QK_PAD:hd * MLA_QK_PAD + MLA_NOPE].astype(q_ref.dtype))
                qr = rope(q[:, hd * MLA_QK_PAD + MLA_NOPE:(hd + 1) * MLA_QK_PAD])
                q_ref[rows, hd * MLA_QK_PAD + MLA_NOPE:(hd + 1) * MLA_QK_PAD] = qr.astype(q_ref.dtype)


def _proj_mla(x, mod, mod_rows, group_rows, norm_g, w_in, q_g, w_uq, kv_g, w_uk, w_uv, rope, with_q, tm):
    m, d = x.shape
    per_group = group_rows // tm
    sc_j, sh_j = mod_rows
    use_rope = rope is not None
    const2 = lambda i: (0, 0)
    in_specs = [
        pl.BlockSpec((tm, d), lambda i: (i, 0)),
        pl.BlockSpec((1, 1, d), lambda i: ((i // per_group) * 6 + sc_j, 0, 0)),
        pl.BlockSpec((1, 1, d), lambda i: ((i // per_group) * 6 + sh_j, 0, 0)),
        pl.BlockSpec((1, d), const2),
        pl.BlockSpec(w_in.shape, const2),
    ]
    args = [x, mod, mod, norm_g, w_in]
    if with_q:
        in_specs += [pl.BlockSpec(q_g.shape, const2), pl.BlockSpec(w_uq.shape, const2)]
        args += [q_g, w_uq]
    in_specs += [pl.BlockSpec(kv_g.shape, const2), pl.BlockSpec(w_uk.shape, const2),
                 pl.BlockSpec(w_uv.shape, const2)]
    args += [kv_g, w_uk, w_uv]
    if use_rope:
        s_tiles = rope[0].shape[0] // tm
        for t in rope:
            in_specs.append(pl.BlockSpec((tm, LANES), lambda i: (i % s_tiles, 0)))
            args.append(t)
    kw = MLA_HEADS * MLA_QK_PAD
    vw = MLA_HEADS * MLA_V
    out_shape = [jax.ShapeDtypeStruct((m, kw), BF16), jax.ShapeDtypeStruct((m, vw), BF16)]
    out_specs = [pl.BlockSpec((tm, kw), lambda i: (i, 0)), pl.BlockSpec((tm, vw), lambda i: (i, 0))]
    if with_q:
        out_shape = [jax.ShapeDtypeStruct((m, kw), BF16)] + out_shape
        out_specs = [pl.BlockSpec((tm, kw), lambda i: (i, 0))] + out_specs
    return pl.pallas_call(
        functools.partial(_proj_mla_kernel, with_q=with_q, use_rope=use_rope, sub=1),
        out_shape=tuple(out_shape),
        grid=(m // tm,),
        in_specs=in_specs,
        out_specs=tuple(out_specs),
        compiler_params=_cparams("parallel"),
        name="proj_mla",
    )(*args)


def _attn_kernel(*refs, n_seg, tq, n_q):
    q_ref = refs[0]
    kv = refs[1:1 + 2 * n_seg]
    o_ref = refs[1 + 2 * n_seg]

    for c in range(n_q):
        q = q_ref[c * tq:(c + 1) * tq, :]
        ss = [_dot_nt(q, kv[2 * j][...]) for j in range(n_seg)]
        mx = ss[0].max(axis=-1, keepdims=True)
        for s in ss[1:]:
            mx = jnp.maximum(mx, s.max(axis=-1, keepdims=True))
        acc = None
        den = None
        for j, s in enumerate(ss):
            e = jnp.exp(s - mx)
            d = e.sum(axis=-1, keepdims=True)
            o = _dot(e.astype(BF16), kv[2 * j + 1][...])
            acc = o if acc is None else acc + o
            den = d if den is None else den + d
        o_ref[c * tq:(c + 1) * tq, :] = (acc / den).astype(o_ref.dtype)


def _attention(q, segs, batch, heads, kv_heads, dk, dv, tq):
    t_q = q.shape[0] // batch
    group = heads // kv_heads
    in_specs = [pl.BlockSpec((t_q, dk), lambda b, h: (b, h))]
    args = [q]
    for k, v in segs:
        t_k = k.shape[0] // batch
        in_specs.append(pl.BlockSpec((t_k, dk), lambda b, h: (b, h // group)))
        in_specs.append(pl.BlockSpec((t_k, dv), lambda b, h: (b, h // group)))
        args += [k, v]
    tq = min(tq, t_q)
    return pl.pallas_call(
        functools.partial(_attn_kernel, n_seg=len(segs), tq=tq, n_q=t_q // tq),
        out_shape=jax.ShapeDtypeStruct((q.shape[0], heads * dv), BF16),
        grid=(batch, heads),
        in_specs=in_specs,
        out_specs=pl.BlockSpec((t_q, dv), lambda b, h: (b, h)),
        compiler_params=_cparams("parallel", "parallel"),
        name="attention",
    )(*args)


def _out_router_kernel(*refs, n_in, tm, sub):
    ins = refs[:2 * n_in]
    (x_ref, g1_ref, sc_ref, sh_ref, ng_ref, rw_ref, rb_ref, cnt0_ref,
     xo_ref, h2_ref, idx_ref, gate_ref, cnt_ref) = refs[2 * n_in:]
    i = pl.program_id(0)

    @pl.when(i == 0)
    def _():
        cnt_ref[...] = cnt0_ref[...]

    ts = tm // sub
    w_hi, w_lo = _split2(rw_ref[...])
    for b in range(sub):
        rows = slice(b * ts, (b + 1) * ts)
        y = None
        for j in range(n_in):
            t = _dot(ins[2 * j][rows, :], ins[2 * j + 1][...])
            y = t if y is None else y + t
        xn = x_ref[rows, :] + g1_ref[0] * y
        xo_ref[rows, :] = xn
        h2 = _rms(xn, ng_ref[...]) * (1.0 + sc_ref[0]) + sh_ref[0]
        _store_token_tiles(h2_ref.at[pl.ds(b * ts * TOKEN_ROWS, ts * TOKEN_ROWS)], h2)

        h_hi, h_lo = _split2(h2)
        logits = _dot_nt(w_hi, h_hi) + _dot_nt(w_hi, h_lo) + _dot_nt(w_lo, h_hi) + rb_ref[...]

        e_iota = lax.broadcasted_iota(jnp.int32, logits.shape, 0)
        work = logits
        tops, idxs = [], []
        for _ in range(TOP_K):
            mx = work.max(axis=0, keepdims=True)
            ix = jnp.where(work == mx, e_iota, N_EXPERTS).min(axis=0, keepdims=True)
            tops.append(mx)
            idxs.append(ix)
            work = jnp.where(e_iota == ix, -jnp.inf, work)
        exps = [jnp.exp(t - tops[0]) for t in tops]
        den = exps[0] + exps[1] + exps[2] + exps[3]

        for k in range(TOP_K):
            idx_ref[k:k + 1, rows] = idxs[k]
            gate_ref[k:k + 1, rows] = exps[k] / den
        cnt = sum((e_iota == ix).astype(F32) for ix in idxs)
        cnt_ref[...] = cnt_ref[...] + cnt.sum(axis=1, keepdims=True)


def _out_router(ins, x, mod, mod_rows, group_rows, norm_g, router_wt, router_b, cnt0, tm):
    m, d = x.shape
    per_group = group_rows // tm
    g1_j, sc_j, sh_j = mod_rows
    in_specs, args = [], []
    for a, w in ins:
        in_specs += [pl.BlockSpec((tm, a.shape[1]), lambda i: (i, 0)), pl.BlockSpec(w.shape, lambda i: (0, 0))]
        args += [a, w]
    mod_spec = lambda j: pl.BlockSpec((1, 1, d), lambda i: ((i // per_group) * 6 + j, 0, 0))
    in_specs += [
        pl.BlockSpec((tm, d), lambda i: (i, 0)),
        mod_spec(g1_j), mod_spec(sc_j), mod_spec(sh_j),
        pl.BlockSpec((1, d), lambda i: (0, 0)),
        pl.BlockSpec(router_wt.shape, lambda i: (0, 0)),
        pl.BlockSpec(router_b.shape, lambda i: (0, 0)),
        pl.BlockSpec(cnt0.shape, lambda i: (0, 0)),
    ]
    args += [x, mod, mod, mod, norm_g, router_wt, router_b, cnt0]
    small = lambda dt: jax.ShapeDtypeStruct((TOP_K, m), dt)
    small_spec = pl.BlockSpec((TOP_K, tm), lambda i: (0, i))
    return pl.pallas_call(
        functools.partial(_out_router_kernel, n_in=len(ins), tm=tm, sub=1),
        out_shape=(jax.ShapeDtypeStruct((m, d), F32), jax.ShapeDtypeStruct((m * TOKEN_ROWS, LANES), U32),
                   small(jnp.int32), small(F32),
                   jax.ShapeDtypeStruct(cnt0.shape, F32)),
        grid=(m // tm,),
        in_specs=in_specs,
        out_specs=(pl.BlockSpec((tm, d), lambda i: (i, 0)),
                   pl.BlockSpec((tm * TOKEN_ROWS, LANES), lambda i: (i, 0)),
                   small_spec, small_spec,
                   pl.BlockSpec(cnt0.shape, lambda i: (0, 0))),
        compiler_params=_cparams("arbitrary"),
        name="out_router",
    )(*args)


def _moe_kernel(blk_e_ref, meta_ref, tok0_ref, tokn_ref, dstp_ref, dstc_ref, h_hbm, w1_ref, b1_ref, w2_ref, b2_ref,
                o_hbm, x0, x1, y0, y1, w1b, w2b, gsem, ssem, *, tm):
    i = pl.program_id(0)
    n_used = meta_ref[0]

    @pl.when(jnp.logical_and(i < n_used, meta_ref[1 + i] == 1))
    def _():
        w1b[...] = w1_ref[0].astype(BF16)
        w2b[...] = w2_ref[0].astype(BF16)
    xs, ys = (x0, x1), (y0, y1)

    def tile_at(ref, row):
        return ref.at[pl.ds(pl.multiple_of(row, TOKEN_ROWS), TOKEN_ROWS)]

    def for_tokens(fn):
        def body(b, carry):
            for j in range(DMA_UNROLL):
                fn(b * DMA_UNROLL + j)
            return carry
        lax.fori_loop(0, tm // DMA_UNROLL, body, 0)

    def gather(tok_ref, xbuf, sem):
        for_tokens(lambda r: pltpu.make_async_copy(
            tile_at(h_hbm, tok_ref[0, 0, r]), tile_at(xbuf, r * TOKEN_ROWS), sem).start())

    def scatter(dst_ref, ybuf, sem):
        for_tokens(lambda r: pltpu.make_async_copy(
            tile_at(ybuf, r * TOKEN_ROWS), tile_at(o_hbm, dst_ref[0, 0, r]), sem).start(priority=1))

    def wait_gather(xbuf, sem):
        pltpu.make_async_copy(h_hbm.at[pl.ds(0, tm * TOKEN_ROWS)], xbuf, sem).wait()

    def wait_scatter(ybuf, sem):
        pltpu.make_async_copy(ybuf, o_hbm.at[pl.ds(0, tm * TOKEN_ROWS)], sem).wait()

    @pl.when(i == 0)
    def _():
        gather(tok0_ref, x0, gsem.at[0])
        y1[...] = jnp.zeros_like(y1)

    def step(p):
        xb, xn, yb, yp = xs[p], xs[1 - p], ys[p], ys[1 - p]
        wait_gather(xb, gsem.at[p])

        @pl.when(i >= 1)
        def _():
            wait_scatter(yb, ssem.at[p])

        @pl.when(n_used > 0)
        def _():
            scatter(dstp_ref, yp, ssem.at[1 - p])
            gather(tokn_ref, xn, gsem.at[1 - p])

        x = _load_token_tiles(xb).astype(BF16)
        hid = _dot(x, w1b[...]) + b1_ref[0]
        f = hid.shape[1] // 2
        g = jnp.minimum(hid[:, :f], SWIGLU_LIMIT)
        lin = jnp.clip(hid[:, f:], -SWIGLU_LIMIT, SWIGLU_LIMIT)
        act = g * jax.nn.sigmoid(SWIGLU_ALPHA * g) * (lin + 1.0)
        _store_token_tiles(yb, _dot(act.astype(BF16), w2b[...]) + b2_ref[0])

        @pl.when(i == n_used - 1)
        def _():
            wait_scatter(yp, ssem.at[1 - p])
            scatter(dstc_ref, yb, ssem.at[p])
            wait_scatter(yb, ssem.at[p])
            wait_gather(xn, gsem.at[1 - p])

    for p in range(2):
        @pl.when(jnp.logical_and(i < n_used, i % 2 == p))
        def _(p=p):
            step(p)


def _moe(h2, row_tok, row_dst, blk_e, n_used, w1, b1, w2, b2, out_rows, tm):
    n_tiles = row_tok.shape[0]
    d = w1.shape[1]
    f2 = w1.shape[2]
    tile_rows = (tm * TOKEN_ROWS, LANES)
    smem = lambda fn: pl.BlockSpec((1, 1, tm), fn, memory_space=pltpu.SMEM)
    grid_spec = pltpu.PrefetchScalarGridSpec(
        num_scalar_prefetch=2,
        grid=(n_tiles,),
        in_specs=[
            smem(lambda i, be, nu: (0, 0, 0)),
            smem(lambda i, be, nu: (jnp.minimum(i + 1, nu[0] - 1), 0, 0)),
            smem(lambda i, be, nu: (i, 0, 0)),
            smem(lambda i, be, nu: (i + 1, 0, 0)),
            pl.BlockSpec(memory_space=pl.ANY),
            pl.BlockSpec((1, d, f2), lambda i, be, nu: (be[i], 0, 0)),
            pl.BlockSpec((1, 1, f2), lambda i, be, nu: (be[i], 0, 0)),
            pl.BlockSpec((1, f2 // 2, d), lambda i, be, nu: (be[i], 0, 0)),
            pl.BlockSpec((1, 1, d), lambda i, be, nu: (be[i], 0, 0)),
        ],
        out_specs=pl.BlockSpec(memory_space=pl.ANY),
        scratch_shapes=[pltpu.VMEM(tile_rows, U32), pltpu.VMEM(tile_rows, U32),
                        pltpu.VMEM(tile_rows, U32), pltpu.VMEM(tile_rows, U32),
                        pltpu.VMEM((d, f2), BF16), pltpu.VMEM((f2 // 2, d), BF16),
                        pltpu.SemaphoreType.DMA((2,)), pltpu.SemaphoreType.DMA((2,))],
    )
    return pl.pallas_call(
        functools.partial(_moe_kernel, tm=tm),
        out_shape=jax.ShapeDtypeStruct((out_rows * TOKEN_ROWS, LANES), U32),
        grid_spec=grid_spec,
        compiler_params=_cparams("arbitrary"),
        name="moe_experts",
    )(blk_e, n_used, row_tok, row_tok, row_dst, row_dst, h2, w1, b1, w2, b2)


def _combine_kernel(*refs, tq, final):
    ys = refs[:TOP_K]
    if final:
        gate_ref, x_ref, g2_ref, fg_ref, o_ref = refs[TOP_K:]
    else:
        gate_ref, x_ref, g2_ref, o_ref = refs[TOP_K:]
    f = None
    d = x_ref.shape[1]
    for k in range(TOP_K):
        gcol = jnp.broadcast_to(gate_ref[k:k + 1, :], (LANES, tq)).T
        t = _load_token_tiles(ys[k]) * jnp.tile(gcol, (1, d // LANES))
        f = t if f is None else f + t
    xn = x_ref[...] + g2_ref[0] * f
    if final:
        xn = _rms(xn, fg_ref[...])
    o_ref[...] = xn


def _combine(ys, t_tot, t_off, gate, x, mod, g2_j, group_rows, final_g, tq):
    m, d = x.shape
    per_group = group_rows // tq
    final = final_g is not None
    in_specs = [pl.BlockSpec((tq * TOKEN_ROWS, LANES),
                             functools.partial(lambda i, k: ((k * t_tot + t_off) // tq + i, 0), k=k))
                for k in range(TOP_K)]
    in_specs += [
        pl.BlockSpec((TOP_K, tq), lambda i: (0, i)),
        pl.BlockSpec((tq, d), lambda i: (i, 0)),
        pl.BlockSpec((1, 1, d), lambda i: ((i // per_group) * 6 + g2_j, 0, 0)),
    ]
    args = [ys] * TOP_K + [gate, x, mod]
    if final:
        in_specs.append(pl.BlockSpec((1, d), lambda i: (0, 0)))
        args.append(final_g)
    return pl.pallas_call(
        functools.partial(_combine_kernel, tq=tq, final=final),
        out_shape=jax.ShapeDtypeStruct((m, d), F32),
        grid=(m // tq,),
        in_specs=in_specs,
        out_specs=pl.BlockSpec((tq, d), lambda i: (i, 0)),
        compiler_params=_cparams("parallel"),
        name="moe_combine",
    )(*args)


def _rope_tables(seq, rot_dim):
    t = np.arange(seq)
    row = (t // GRID_W).astype(np.float32)
    col = (t % GRID_W).astype(np.float32)
    axis_dim = rot_dim // 2
    quarter = axis_dim // 2
    inv = jnp.asarray(ROPE_THETA, F32) ** (-jnp.arange(0, axis_dim, 2, dtype=F32) / axis_dim)
    ang_r = jnp.asarray(row)[:, None] * inv
    ang_c = jnp.asarray(col)[:, None] * inv
    zeros = jnp.zeros_like(ang_r)
    cos = jnp.concatenate([jnp.cos(ang_r)] * 2 + [jnp.cos(ang_c)] * 2, axis=1)
    s_a = jnp.concatenate([-jnp.sin(ang_r), zeros, -jnp.sin(ang_c), zeros], axis=1)
    s_b = jnp.concatenate([zeros, jnp.sin(ang_r), zeros, jnp.sin(ang_c)], axis=1)
    pad = LANES - rot_dim
    if pad:
        cos, s_a, s_b = (jnp.pad(a, ((0, 0), (0, pad))) for a in (cos, s_a, s_b))
    del quarter
    return cos, s_a, s_b


def _route(idx, counts, tm):
    t_tot = idx.shape[1]
    n_assign = t_tot * TOP_K
    n_rows = n_assign + N_EXPERTS * tm
    n_tiles = n_rows // tm
    counts = counts.astype(jnp.int32)
    pad_need = (-counts) % tm
    flat_e = idx.T.reshape(-1)
    slot = jnp.arange(tm, dtype=jnp.int32)[None, :]
    pad_keys = jnp.where(slot < pad_need[:, None], jnp.arange(N_EXPERTS, dtype=jnp.int32)[:, None], N_EXPERTS)
    keys = jnp.concatenate([flat_e, pad_keys.reshape(-1)])
    pos_bits = (n_rows - 1).bit_length()
    packed = lax.sort((keys << pos_bits) | jnp.arange(n_rows, dtype=jnp.int32))
    skeys = packed >> pos_bits
    svals = packed & ((1 << pos_bits) - 1)
    real = svals < n_assign
    row_tok = jnp.where(real, svals // TOP_K, 0)
    rows = jnp.arange(n_rows, dtype=jnp.int32)
    row_dst = jnp.where(real, (svals % TOP_K) * t_tot + svals // TOP_K, n_assign + rows % tm)
    tile_e = skeys[::tm]
    n_used = jnp.sum(tile_e < N_EXPERTS).astype(jnp.int32)
    last_e = tile_e[jnp.maximum(n_used - 1, 0)]
    blk_e = jnp.where(tile_e < N_EXPERTS, tile_e, last_e).astype(jnp.int32)
    row_dst = jnp.concatenate([n_assign + jnp.arange(tm, dtype=jnp.int32), row_dst])
    row_tok = (row_tok * TOKEN_ROWS).reshape(n_tiles, 1, tm)
    row_dst = (row_dst * TOKEN_ROWS).reshape(n_tiles + 1, 1, tm)
    first = jnp.concatenate([jnp.ones((1,), jnp.int32), (blk_e[1:] != blk_e[:-1]).astype(jnp.int32)])
    meta = jnp.concatenate([n_used.reshape(1), first])
    return row_tok, row_dst, blk_e, meta


def _pick(m, pref):
    t = pref
    while m % t:
        t //= 2
    return t


def kernel(x, c, ctx, c_ctx, ada_w, ada_b, norm_mix_g, norm_ffn_g, ab_w_in, sgu_norm_g, sgu_w, sgu_b,
           gqa_q_norm_g, gqa_k_norm_g, ab_w_out, mla_w_in, mla_q_norm_g, mla_kv_norm_g, mla_w_uq, mla_w_ukv,
           mla_w_out, router_w, router_b, moe_w1, moe_b1, moe_w2, moe_b2, final_norm_g):
    bn, s, d = x.shape
    l = ctx.shape[1]
    depth = ada_w.shape[0]
    m_lat, m_ctx = bn * s, bn * l
    tm_lat = _pick(s, 512)
    tm_ctx = _pick(l, 512)
    tm_moe = 512
    tq_comb = _pick(int(np.gcd(s, m_ctx)), 512)

    n_cond = (bn + 1 + 7) // 8 * 8
    cond = jnp.concatenate([c, c_ctx[None, :], jnp.zeros((n_cond - bn - 1, d), F32)], axis=0)
    mod_all = _ada(cond, ada_w, ada_b)

    rope_gqa = _rope_tables(s, HEAD_DIM)
    rope_mla = _rope_tables(s, MLA_ROPE)
    w1_all = moe_w1.reshape(depth * N_EXPERTS, d, -1)
    w2_all = moe_w2.reshape(depth * N_EXPERTS, -1, d)
    b1_all = moe_b1.reshape(depth * N_EXPERTS, 1, -1)
    b2_all = moe_b2.reshape(depth * N_EXPERTS, 1, d)

    xl = x.reshape(m_lat, d)
    xc = ctx.reshape(m_ctx, d)
    for i in range(depth):
        ctx_out = i < depth - 1
        j = i // 2
        mod_lat = mod_all[i, :bn].reshape(bn * 6, 1, d)
        mod_ctx = mod_all[i, bn].reshape(6, 1, d)
        ng = norm_mix_g[i].reshape(1, d)
        if i % 2 == 0:
            w_in = ab_w_in[j].astype(BF16)
            w_out = ab_w_out[j].astype(BF16)
            sgu_g = sgu_norm_g[j].reshape(1, SGU_WIDTH)
            sgu_wb = sgu_w[j].astype(BF16)
            sgu_bfull = jnp.broadcast_to(sgu_b[j][:, :, None], (SGU_GROUPS, CHUNK, SGU_GROUP_CH))
            q_g = gqa_q_norm_g[j].reshape(1, HEAD_DIM)
            k_g = gqa_k_norm_g[j].reshape(1, HEAD_DIM)
            oa, q, k, v = _proj_ab(xl, mod_lat, (1, 0), s, ng, w_in, sgu_g, sgu_wb, sgu_bfull, q_g, k_g,
                                   rope_gqa, tm_lat)
            oac, qc, kc, vc = _proj_ab(xc, mod_ctx, (1, 0), m_ctx, ng, w_in, sgu_g, sgu_wb, sgu_bfull, q_g, k_g,
                                       None, tm_ctx)
            ob = _attention(q, [(kc, vc), (k, v)], bn, GQA_HEADS, GQA_KV_HEADS, HEAD_DIM, HEAD_DIM, ATTN_TQ)
            ins = [(oa, w_out[:SGU_WIDTH]), (ob, w_out[SGU_WIDTH:])]
            if ctx_out:
                obc = _attention(qc, [(kc, vc)], bn, GQA_HEADS, GQA_KV_HEADS, HEAD_DIM, HEAD_DIM, ATTN_TQ)
                ins_c = [(oac, w_out[:SGU_WIDTH]), (obc, w_out[SGU_WIDTH:])]
        else:
            w_in_f = mla_w_in[j]
            w_in = jnp.pad(w_in_f, ((0, 0), (0, MLA_IN_PAD - w_in_f.shape[1]))).astype(BF16)
            w_in_kv = w_in[:, MLA_Q_LORA:]
            w_out = mla_w_out[j].astype(BF16)
            q_g = mla_q_norm_g[j].reshape(1, MLA_Q_LORA)
            kv_g = mla_kv_norm_g[j].reshape(1, MLA_KV_LORA)
            w_uq = mla_w_uq[j].reshape(MLA_Q_LORA, MLA_HEADS, MLA_QK)
            w_uq = jnp.pad(w_uq, ((0, 0), (0, 0), (0, MLA_QK_PAD - MLA_QK)))
            w_uq = w_uq.reshape(MLA_Q_LORA, MLA_HEADS * MLA_QK_PAD).astype(BF16)
            w_ukv = mla_w_ukv[j].reshape(MLA_KV_LORA, MLA_HEADS, MLA_NOPE + MLA_V)
            w_uk = w_ukv[:, :, :MLA_NOPE].reshape(MLA_KV_LORA, MLA_HEADS * MLA_NOPE).astype(BF16)
            w_uv = w_ukv[:, :, MLA_NOPE:].reshape(MLA_KV_LORA, MLA_HEADS * MLA_V).astype(BF16)
            q, k, v = _proj_mla(xl, mod_lat, (1, 0), s, ng, w_in, q_g, w_uq, kv_g, w_uk, w_uv, rope_mla, True,
                                tm_lat)
            if ctx_out:
                qc, kc, vc = _proj_mla(xc, mod_ctx, (1, 0), m_ctx, ng, w_in, q_g, w_uq, kv_g, w_uk, w_uv, None,
                                       True, tm_ctx)
            else:
                kc, vc = _proj_mla(xc, mod_ctx, (1, 0), m_ctx, ng, w_in_kv, None, None, kv_g, w_uk, w_uv, None,
                                   False, tm_ctx)
            ob = _attention(q, [(kc, vc), (k, v)], bn, MLA_HEADS, MLA_HEADS, MLA_QK_PAD, MLA_V, ATTN_TQ)
            ins = [(ob, w_out)]
            if ctx_out:
                obc = _attention(qc, [(kc, vc)], bn, MLA_HEADS, MLA_HEADS, MLA_QK_PAD, MLA_V, ATTN_TQ)
                ins_c = [(obc, w_out)]

        nfg = norm_ffn_g[i].reshape(1, d)
        rwt = router_w[i].T
        rb = router_b[i].reshape(N_EXPERTS, 1)
        cnt0 = jnp.zeros((N_EXPERTS, LANES), F32)
        xl, h2, idx, gate, cnt = _out_router(ins, xl, mod_lat, (2, 4, 3), s, nfg, rwt, rb, cnt0, tm_lat)
        if ctx_out:
            xc, h2c, idx_c, gate_c, cnt = _out_router(ins_c, xc, mod_ctx, (2, 4, 3), m_ctx, nfg, rwt, rb, cnt, tm_ctx)
            h2 = jnp.concatenate([h2, h2c], axis=0)
            idx = jnp.concatenate([idx, idx_c], axis=1)
        t_tot = h2.shape[0] // TOKEN_ROWS
        n_assign = t_tot * TOP_K
        row_tok, row_dst, blk_e, n_used = _route(idx, cnt[:, 0], tm_moe)
        ys = _moe(h2, row_tok, row_dst, blk_e + i * N_EXPERTS, n_used, w1_all, b1_all, w2_all, b2_all,
                  n_assign + tm_moe, tm_moe)
        last = i == depth - 1
        fg = final_norm_g.reshape(1, d) if last else None
        xl = _combine(ys, t_tot, 0, gate, xl, mod_lat, 5, s, fg, tq_comb)
        if ctx_out:
            xc = _combine(ys, t_tot, m_lat, gate_c, xc, mod_ctx, 5, m_ctx, None, tq_comb)
    return xl.reshape(bn, s, d)
```

```python
import functools

import jax
import jax.numpy as jnp
import numpy as np
from jax import lax
from jax.experimental import pallas as pl
from jax.experimental.pallas import tpu as pltpu

F32 = jnp.float32
BF16 = jnp.bfloat16
U32 = jnp.uint32

GRID_W = 64
EPS = 1e-6
ROPE_THETA = 10000.0
CHUNK = 128
SGU_GROUPS = 4
SGU_GROUP_CH = 128
SGU_WIDTH = SGU_GROUPS * SGU_GROUP_CH
GQA_HEADS = 4
GQA_KV_HEADS = 2
HEAD_DIM = 128
GQA_Q_W = GQA_HEADS * HEAD_DIM
GQA_KV_W = GQA_KV_HEADS * HEAD_DIM
A_END = 2 * SGU_WIDTH
K_OFF = A_END + GQA_Q_W
V_OFF = K_OFF + GQA_KV_W
MLA_HEADS = 8
MLA_Q_LORA = 512
MLA_KV_LORA = 256
MLA_NOPE = 128
MLA_ROPE = 64
MLA_V = 128
MLA_QK = MLA_NOPE + MLA_ROPE
MLA_QK_PAD = 256
MLA_IN_PAD = 896
N_EXPERTS = 32
TOP_K = 4
D_EXPERT = 1024
SWIGLU_LIMIT = 7.0
SWIGLU_ALPHA = 1.702

LANES = 128
TOKEN_ROWS = 4
DMA_UNROLL = 64
ROW_SUB = 4
ATTN_TQ = 256
VMEM_LIMIT = 56 * 1024 * 1024


def _cparams(*sem):
    return pltpu.CompilerParams(dimension_semantics=sem, vmem_limit_bytes=VMEM_LIMIT)


def _dot(a, b):
    return jnp.dot(a, b, preferred_element_type=F32)


def _dot_nt(a, b):
    return lax.dot_general(a, b, (((1,), (1,)), ((), ())), preferred_element_type=F32)


def _split2(a):
    hi = a.astype(BF16)
    lo = (a - hi.astype(F32)).astype(BF16)
    return hi, lo


def _row_sub(tm):
    return ROW_SUB if tm % (ROW_SUB * CHUNK) == 0 else 1


def _rms(x, g):
    return x * lax.rsqrt(jnp.mean(x * x, axis=-1, keepdims=True) + EPS) * g


def _store_token_tiles(ref, val):
    n, d = val.shape
    h = d // 2
    lo = lax.bitcast_convert_type(val[:, :h].astype(BF16).astype(F32), U32) >> 16
    hi = lax.bitcast_convert_type(val[:, h:].astype(BF16).astype(F32), U32) & jnp.uint32(0xFFFF0000)
    packed = hi | lo
    for s in range(TOKEN_ROWS):
        ref[pl.ds(s, n, stride=TOKEN_ROWS), :] = packed[:, s * LANES:(s + 1) * LANES]


def _load_token_tiles(ref):
    n = ref.shape[0] // TOKEN_ROWS
    packed = jnp.concatenate([ref[pl.ds(s, n, stride=TOKEN_ROWS), :] for s in range(TOKEN_ROWS)], axis=1)
    lo = lax.bitcast_convert_type(packed << 16, F32)
    hi = lax.bitcast_convert_type(packed & jnp.uint32(0xFFFF0000), F32)
    return jnp.concatenate([lo, hi], axis=1)


def _rope(x, c, s_a, s_b, half):
    n = x.shape[-1]
    return x * c + pltpu.roll(x, n - half, 1) * s_a + pltpu.roll(x, half, 1) * s_b


def _ada_kernel(c_ref, w_ref, b_ref, o_ref):
    c = c_ref[...]
    sc = c * jax.nn.sigmoid(c)
    c_hi, c_lo = _split2(sc)
    w = w_ref[0]
    w_hi, w_lo = _split2(w)
    acc = _dot(c_hi, w_hi) + _dot(c_hi, w_lo) + _dot(c_lo, w_hi)
    o_ref[0] = acc + b_ref[0]


def _ada(cond, ada_w, ada_b):
    depth, d, n = ada_w.shape
    rows = cond.shape[0]
    tn = 1536
    return pl.pallas_call(
        _ada_kernel,
        out_shape=jax.ShapeDtypeStruct((depth, rows, n), F32),
        grid=(depth, n // tn),
        in_specs=[
            pl.BlockSpec((rows, d), lambda l, j: (0, 0)),
            pl.BlockSpec((1, d, tn), lambda l, j: (l, 0, j)),
            pl.BlockSpec((1, 1, tn), lambda l, j: (l, 0, j)),
        ],
        out_specs=pl.BlockSpec((1, rows, tn), lambda l, j: (l, 0, j)),
        compiler_params=_cparams("arbitrary", "arbitrary"),
        name="ada_mod",
    )(cond, ada_w, ada_b.reshape(depth, 1, n))


def _proj_ab_kernel(*refs, use_rope, tm, sub):
    if use_rope:
        (x_ref, sc_ref, sh_ref, g_ref, w_ref, sgug_ref, sguw_ref, sgub_ref, qg_ref, kg_ref,
         c_ref, sa_ref, sb_ref, oa_ref, q_ref, k_ref, v_ref) = refs
    else:
        (x_ref, sc_ref, sh_ref, g_ref, w_ref, sgug_ref, sguw_ref, sgub_ref, qg_ref, kg_ref,
         oa_ref, q_ref, k_ref, v_ref) = refs
    ts = tm // sub
    n_chunks = ts // CHUNK
    qg = qg_ref[...] * (HEAD_DIM ** -0.5)
    for b in range(sub):
        rows = slice(b * ts, (b + 1) * ts)
        x = x_ref[rows, :]
        h = _rms(x, g_ref[...]) * (1.0 + sc_ref[0]) + sh_ref[0]
        p = _dot(h.astype(BF16), w_ref[...])
        rope = (lambda t: _rope(t, c_ref[rows, :], sa_ref[rows, :], sb_ref[rows, :], HEAD_DIM // 4)) \
            if use_rope else (lambda t: t)

        for g in range(SGU_GROUPS):
            u = jax.nn.gelu(p[:, g * SGU_GROUP_CH:(g + 1) * SGU_GROUP_CH])
            v = jax.nn.gelu(p[:, SGU_WIDTH + g * SGU_GROUP_CH:SGU_WIDTH + (g + 1) * SGU_GROUP_CH])
            v = _rms(v, sgug_ref[:, g * SGU_GROUP_CH:(g + 1) * SGU_GROUP_CH]).astype(BF16)
            vcat = jnp.concatenate([v[c * CHUNK:(c + 1) * CHUNK, :] for c in range(n_chunks)], axis=1)
            mixed = _dot(sguw_ref[g], vcat)
            bias = sgub_ref[g]
            for c in range(n_chunks):
                gate = mixed[:, c * SGU_GROUP_CH:(c + 1) * SGU_GROUP_CH] + bias
                oa_ref[b * ts + c * CHUNK:b * ts + (c + 1) * CHUNK, g * SGU_GROUP_CH:(g + 1) * SGU_GROUP_CH] = (
                    u[c * CHUNK:(c + 1) * CHUNK, :] * gate).astype(oa_ref.dtype)

        for hd in range(GQA_HEADS):
            qh = rope(_rms(p[:, A_END + hd * HEAD_DIM:A_END + (hd + 1) * HEAD_DIM], qg))
            q_ref[rows, hd * HEAD_DIM:(hd + 1) * HEAD_DIM] = qh.astype(q_ref.dtype)
        for hd in range(GQA_KV_HEADS):
            kh = rope(_rms(p[:, K_OFF + hd * HEAD_DIM:K_OFF + (hd + 1) * HEAD_DIM], kg_ref[...]))
            k_ref[rows, hd * HEAD_DIM:(hd + 1) * HEAD_DIM] = kh.astype(k_ref.dtype)
        v_ref[rows, :] = p[:, V_OFF:].astype(v_ref.dtype)


def _proj_ab(x, mod, mod_rows, group_rows, norm_g, w_in, sgu_g, sgu_w, sgu_bfull, q_g, k_g, rope, tm):
    m, d = x.shape
    per_group = group_rows // tm
    sc_j, sh_j = mod_rows
    use_rope = rope is not None
    in_specs = [
        pl.BlockSpec((tm, d), lambda i: (i, 0)),
        pl.BlockSpec((1, 1, d), lambda i: ((i // per_group) * 6 + sc_j, 0, 0)),
        pl.BlockSpec((1, 1, d), lambda i: ((i // per_group) * 6 + sh_j, 0, 0)),
        pl.BlockSpec((1, d), lambda i: (0, 0)),
        pl.BlockSpec(w_in.shape, lambda i: (0, 0)),
        pl.BlockSpec((1, SGU_WIDTH), lambda i: (0, 0)),
        pl.BlockSpec(sgu_w.shape, lambda i: (0, 0, 0)),
        pl.BlockSpec(sgu_bfull.shape, lambda i: (0, 0, 0)),
        pl.BlockSpec((1, HEAD_DIM), lambda i: (0, 0)),
        pl.BlockSpec((1, HEAD_DIM), lambda i: (0, 0)),
    ]
    args = [x, mod, mod, norm_g, w_in, sgu_g, sgu_w, sgu_bfull, q_g, k_g]
    if use_rope:
        s_tiles = rope[0].shape[0] // tm
        for t in rope:
            in_specs.append(pl.BlockSpec((tm, HEAD_DIM), lambda i: (i % s_tiles, 0)))
            args.append(t)
    return pl.pallas_call(
        functools.partial(_proj_ab_kernel, use_rope=use_rope, tm=tm, sub=_row_sub(tm)),
        out_shape=(jax.ShapeDtypeStruct((m, SGU_WIDTH), BF16),
                   jax.ShapeDtypeStruct((m, GQA_Q_W), BF16),
                   jax.ShapeDtypeStruct((m, GQA_KV_W), BF16),
                   jax.ShapeDtypeStruct((m, GQA_KV_W), BF16)),
        grid=(m // tm,),
        in_specs=in_specs,
        out_specs=(pl.BlockSpec((tm, SGU_WIDTH), lambda i: (i, 0)),
                   pl.BlockSpec((tm, GQA_Q_W), lambda i: (i, 0)),
                   pl.BlockSpec((tm, GQA_KV_W), lambda i: (i, 0)),
                   pl.BlockSpec((tm, GQA_KV_W), lambda i: (i, 0))),
        compiler_params=_cparams("parallel"),
        name="proj_ab",
    )(*args)


def _proj_mla_kernel(*refs, with_q, use_rope, sub):
    it = iter(refs)
    x_ref, sc_ref, sh_ref, g_ref, w_ref = (next(it) for _ in range(5))
    if with_q:
        qg_ref, wuq_ref = next(it), next(it)
    kvg_ref, wuk_ref, wuv_ref = next(it), next(it), next(it)
    if use_rope:
        c_ref, sa_ref, sb_ref = next(it), next(it), next(it)
    if with_q:
        q_ref = next(it)
    k_ref, v_ref = next(it), next(it)

    ts = x_ref.shape[0] // sub
    off = MLA_Q_LORA if with_q else 0
    for b in range(sub):
        rows = slice(b * ts, (b + 1) * ts)
        rope = (lambda t: _rope(t, c_ref[rows, :], sa_ref[rows, :], sb_ref[rows, :], MLA_ROPE // 4)) \
            if use_rope else (lambda t: t)
        x = x_ref[rows, :]
        h = _rms(x, g_ref[...]) * (1.0 + sc_ref[0]) + sh_ref[0]
        p = _dot(h.astype(BF16), w_ref[...])
        ckv = _rms(p[:, off:off + MLA_KV_LORA], kvg_ref[...]).astype(BF16)
        kr = rope(p[:, off + MLA_KV_LORA:off + MLA_KV_LORA + LANES])
        kr = kr.astype(k_ref.dtype)
        k_nope = _dot(ckv, wuk_ref[...])
        v_ref[rows, :] = _dot(ckv, wuv_ref[...]).astype(v_ref.dtype)
        for hd in range(MLA_HEADS):
            k_ref[rows, hd * MLA_QK_PAD:hd * MLA_QK_PAD + MLA_NOPE] = (
                k_nope[:, hd * MLA_NOPE:(hd + 1) * MLA_NOPE].astype(k_ref.dtype))
            k_ref[rows, hd * MLA_QK_PAD + MLA_NOPE:(hd + 1) * MLA_QK_PAD] = kr
        if with_q:
            cq = _rms(p[:, :MLA_Q_LORA], qg_ref[...]).astype(BF16)
            q = _dot(cq, wuq_ref[...]) * (MLA_QK ** -0.5)
            for hd in range(MLA_HEADS):
                q_ref[rows, hd * MLA_QK_PAD:hd * MLA_QK_PAD + MLA_NOPE] = (
                    q[:, hd * MLA_QK_PAD:hd * MLA_QK_PAD + MLA_NOPE].astype(q_ref.dtype))
                qr = rope(q[:, hd * MLA_QK_PAD + MLA_NOPE:(hd + 1) * MLA_QK_PAD])
                q_ref[rows, hd * MLA_QK_PAD + MLA_NOPE:(hd + 1) * MLA_QK_PAD] = qr.astype(q_ref.dtype)


def _proj_mla(x, mod, mod_rows, group_rows, norm_g, w_in, q_g, w_uq, kv_g, w_uk, w_uv, rope, with_q, tm):
    m, d = x.shape
    per_group = group_rows // tm
    sc_j, sh_j = mod_rows
    use_rope = rope is not None
    const2 = lambda i: (0, 0)
    in_specs = [
        pl.BlockSpec((tm, d), lambda i: (i, 0)),
        pl.BlockSpec((1, 1, d), lambda i: ((i // per_group) * 6 + sc_j, 0, 0)),
        pl.BlockSpec((1, 1, d), lambda i: ((i // per_group) * 6 + sh_j, 0, 0)),
        pl.BlockSpec((1, d), const2),
        pl.BlockSpec(w_in.shape, const2),
    ]
    args = [x, mod, mod, norm_g, w_in]
    if with_q:
        in_specs += [pl.BlockSpec(q_g.shape, const2), pl.BlockSpec(w_uq.shape, const2)]
        args += [q_g, w_uq]
    in_specs += [pl.BlockSpec(kv_g.shape, const2), pl.BlockSpec(w_uk.shape, const2),
                 pl.BlockSpec(w_uv.shape, const2)]
    args += [kv_g, w_uk, w_uv]
    if use_rope:
        s_tiles = rope[0].shape[0] // tm
        for t in rope:
            in_specs.append(pl.BlockSpec((tm, LANES), lambda i: (i % s_tiles, 0)))
            args.append(t)
    kw = MLA_HEADS * MLA_QK_PAD
    vw = MLA_HEADS * MLA_V
    out_shape = [jax.ShapeDtypeStruct((m, kw), BF16), jax.ShapeDtypeStruct((m, vw), BF16)]
    out_specs = [pl.BlockSpec((tm, kw), lambda i: (i, 0)), pl.BlockSpec((tm, vw), lambda i: (i, 0))]
    if with_q:
        out_shape = [jax.ShapeDtypeStruct((m, kw), BF16)] + out_shape
        out_specs = [pl.BlockSpec((tm, kw), lambda i: (i, 0))] + out_specs
    return pl.pallas_call(
        functools.partial(_proj_mla_kernel, with_q=with_q, use_rope=use_rope, sub=1),
        out_shape=tuple(out_shape),
        grid=(m // tm,),
        in_specs=in_specs,
        out_specs=tuple(out_specs),
        compiler_params=_cparams("parallel"),
        name="proj_mla",
    )(*args)


def _attn_kernel(*refs, n_seg, tq, n_q):
    q_ref = refs[0]
    kv = refs[1:1 + 2 * n_seg]
    o_ref = refs[1 + 2 * n_seg]

    for c in range(n_q):
        q = q_ref[c * tq:(c + 1) * tq, :]
        ss = [_dot_nt(q, kv[2 * j][...]) for j in range(n_seg)]
        mx = ss[0].max(axis=-1, keepdims=True)
        for s in ss[1:]:
            mx = jnp.maximum(mx, s.max(axis=-1, keepdims=True))
        acc = None
        den = None
        for j, s in enumerate(ss):
            e = jnp.exp(s - mx)
            d = e.sum(axis=-1, keepdims=True)
            o = _dot(e.astype(BF16), kv[2 * j + 1][...])
            acc = o if acc is None else acc + o
            den = d if den is None else den + d
        o_ref[c * tq:(c + 1) * tq, :] = (acc / den).astype(o_ref.dtype)


def _attention(q, segs, batch, heads, kv_heads, dk, dv, tq):
    t_q = q.shape[0] // batch
    group = heads // kv_heads
    in_specs = [pl.BlockSpec((t_q, dk), lambda b, h: (b, h))]
    args = [q]
    for k, v in segs:
        t_k = k.shape[0] // batch
        in_specs.append(pl.BlockSpec((t_k, dk), lambda b, h: (b, h // group)))
        in_specs.append(pl.BlockSpec((t_k, dv), lambda b, h: (b, h // group)))
        args += [k, v]
    tq = min(tq, t_q)
    return pl.pallas_call(
        functools.partial(_attn_kernel, n_seg=len(segs), tq=tq, n_q=t_q // tq),
        out_shape=jax.ShapeDtypeStruct((q.shape[0], heads * dv), BF16),
        grid=(batch, heads),
        in_specs=in_specs,
        out_specs=pl.BlockSpec((t_q, dv), lambda b, h: (b, h)),
        compiler_params=_cparams("parallel", "parallel"),
        name="attention",
    )(*args)


def _out_router_kernel(*refs, n_in, tm, sub):
    ins = refs[:2 * n_in]
    (x_ref, g1_ref, sc_ref, sh_ref, ng_ref, rw_ref, rb_ref, cnt0_ref,
     xo_ref, h2_ref, idx_ref, gate_ref, cnt_ref) = refs[2 * n_in:]
    i = pl.program_id(0)

    @pl.when(i == 0)
    def _():
        cnt_ref[...] = cnt0_ref[...]

    ts = tm // sub
    w_hi, w_lo = _split2(rw_ref[...])
    for b in range(sub):
        rows = slice(b * ts, (b + 1) * ts)
        y = None
        for j in range(n_in):
            t = _dot(ins[2 * j][rows, :], ins[2 * j + 1][...])
            y = t if y is None else y + t
        xn = x_ref[rows, :] + g1_ref[0] * y
        xo_ref[rows, :] = xn
        h2 = _rms(xn, ng_ref[...]) * (1.0 + sc_ref[0]) + sh_ref[0]
        _store_token_tiles(h2_ref.at[pl.ds(b * ts * TOKEN_ROWS, ts * TOKEN_ROWS)], h2)

        h_hi, h_lo = _split2(h2)
        logits = _dot_nt(w_hi, h_hi) + _dot_nt(w_hi, h_lo) + _dot_nt(w_lo, h_hi) + rb_ref[...]

        e_iota = lax.broadcasted_iota(jnp.int32, logits.shape, 0)
        work = logits
        tops, idxs = [], []
        for _ in range(TOP_K):
            mx = work.max(axis=0, keepdims=True)
            ix = jnp.where(work == mx, e_iota, N_EXPERTS).min(axis=0, keepdims=True)
            tops.append(mx)
            idxs.append(ix)
            work = jnp.where(e_iota == ix, -jnp.inf, work)
        exps = [jnp.exp(t - tops[0]) for t in tops]
        den = exps[0] + exps[1] + exps[2] + exps[3]

        for k in range(TOP_K):
            idx_ref[k:k + 1, rows] = idxs[k]
            gate_ref[k:k + 1, rows] = exps[k] / den
        cnt = sum((e_iota == ix).astype(F32) for ix in idxs)
        cnt_ref[...] = cnt_ref[...] + cnt.sum(axis=1, keepdims=True)


def _out_router(ins, x, mod, mod_rows, group_rows, norm_g, router_wt, router_b, cnt0, tm):
    m, d = x.shape
    per_group = group_rows // tm
    g1_j, sc_j, sh_j = mod_rows
    in_specs, args = [], []
    for a, w in ins:
        in_specs += [pl.BlockSpec((tm, a.shape[1]), lambda i: (i, 0)), pl.BlockSpec(w.shape, lambda i: (0, 0))]
        args += [a, w]
    mod_spec = lambda j: pl.BlockSpec((1, 1, d), lambda i: ((i // per_group) * 6 + j, 0, 0))
    in_specs += [
        pl.BlockSpec((tm, d), lambda i: (i, 0)),
        mod_spec(g1_j), mod_spec(sc_j), mod_spec(sh_j),
        pl.BlockSpec((1, d), lambda i: (0, 0)),
        pl.BlockSpec(router_wt.shape, lambda i: (0, 0)),
        pl.BlockSpec(router_b.shape, lambda i: (0, 0)),
        pl.BlockSpec(cnt0.shape, lambda i: (0, 0)),
    ]
    args += [x, mod, mod, mod, norm_g, router_wt, router_b, cnt0]
    small = lambda dt: jax.ShapeDtypeStruct((TOP_K, m), dt)
    small_spec = pl.BlockSpec((TOP_K, tm), lambda i: (0, i))
    return pl.pallas_call(
        functools.partial(_out_router_kernel, n_in=len(ins), tm=tm, sub=1),
        out_shape=(jax.ShapeDtypeStruct((m, d), F32), jax.ShapeDtypeStruct((m * TOKEN_ROWS, LANES), U32),
                   small(jnp.int32), small(F32),
                   jax.ShapeDtypeStruct(cnt0.shape, F32)),
        grid=(m // tm,),
        in_specs=in_specs,
        out_specs=(pl.BlockSpec((tm, d), lambda i: (i, 0)),
                   pl.BlockSpec((tm * TOKEN_ROWS, LANES), lambda i: (i, 0)),
                   small_spec, small_spec,
                   pl.BlockSpec(cnt0.shape, lambda i: (0, 0))),
        compiler_params=_cparams("arbitrary"),
        name="out_router",
    )(*args)


def _moe_kernel(blk_e_ref, meta_ref, tok0_ref, tokn_ref, dstp_ref, dstc_ref, h_hbm, w1_ref, b1_ref, w2_ref, b2_ref,
                o_hbm, x0, x1, y0, y1, w1b, w2b, gsem, ssem, *, tm):
    i = pl.program_id(0)
    n_used = meta_ref[0]

    @pl.when(jnp.logical_and(i < n_used, meta_ref[1 + i] == 1))
    def _():
        w1b[...] = w1_ref[0].astype(BF16)
        w2b[...] = w2_ref[0].astype(BF16)
    xs, ys = (x0, x1), (y0, y1)

    def tile_at(ref, row):
        return ref.at[pl.ds(pl.multiple_of(row, TOKEN_ROWS), TOKEN_ROWS)]

    def for_tokens(fn):
        def body(b, carry):
            for j in range(DMA_UNROLL):
                fn(b * DMA_UNROLL + j)
            return carry
        lax.fori_loop(0, tm // DMA_UNROLL, body, 0)

    def gather(tok_ref, xbuf, sem):
        for_tokens(lambda r: pltpu.make_async_copy(
            tile_at(h_hbm, tok_ref[0, 0, r]), tile_at(xbuf, r * TOKEN_ROWS), sem).start())

    def scatter(dst_ref, ybuf, sem):
        for_tokens(lambda r: pltpu.make_async_copy(
            tile_at(ybuf, r * TOKEN_ROWS), tile_at(o_hbm, dst_ref[0, 0, r]), sem).start(priority=1))

    def wait_gather(xbuf, sem):
        pltpu.make_async_copy(h_hbm.at[pl.ds(0, tm * TOKEN_ROWS)], xbuf, sem).wait()

    def wait_scatter(ybuf, sem):
        pltpu.make_async_copy(ybuf, o_hbm.at[pl.ds(0, tm * TOKEN_ROWS)], sem).wait()

    @pl.when(i == 0)
    def _():
        gather(tok0_ref, x0, gsem.at[0])
        y1[...] = jnp.zeros_like(y1)

    def step(p):
        xb, xn, yb, yp = xs[p], xs[1 - p], ys[p], ys[1 - p]
        wait_gather(xb, gsem.at[p])

        @pl.when(i >= 1)
        def _():
            wait_scatter(yb, ssem.at[p])

        @pl.when(n_used > 0)
        def _():
            scatter(dstp_ref, yp, ssem.at[1 - p])
            gather(tokn_ref, xn, gsem.at[1 - p])

        x = _load_token_tiles(xb).astype(BF16)
        hid = _dot(x, w1b[...]) + b1_ref[0]
        f = hid.shape[1] // 2
        g = jnp.minimum(hid[:, :f], SWIGLU_LIMIT)
        lin = jnp.clip(hid[:, f:], -SWIGLU_LIMIT, SWIGLU_LIMIT)
        act = g * jax.nn.sigmoid(SWIGLU_ALPHA * g) * (lin + 1.0)
        _store_token_tiles(yb, _dot(act.astype(BF16), w2b[...]) + b2_ref[0])

        @pl.when(i == n_used - 1)
        def _():
            wait_scatter(yp, ssem.at[1 - p])
            scatter(dstc_ref, yb, ssem.at[p])
            wait_scatter(yb, ssem.at[p])
            wait_gather(xn, gsem.at[1 - p])

    for p in range(2):
        @pl.when(jnp.logical_and(i < n_used, i % 2 == p))
        def _(p=p):
            step(p)


def _moe(h2, row_tok, row_dst, blk_e, n_used, w1, b1, w2, b2, out_rows, tm):
    n_tiles = row_tok.shape[0]
    d = w1.shape[1]
    f2 = w1.shape[2]
    tile_rows = (tm * TOKEN_ROWS, LANES)
    smem = lambda fn: pl.BlockSpec((1, 1, tm), fn, memory_space=pltpu.SMEM)
    grid_spec = pltpu.PrefetchScalarGridSpec(
        num_scalar_prefetch=2,
        grid=(n_tiles,),
        in_specs=[
            smem(lambda i, be, nu: (0, 0, 0)),
            smem(lambda i, be, nu: (jnp.minimum(i + 1, nu[0] - 1), 0, 0)),
            smem(lambda i, be, nu: (i, 0, 0)),
            smem(lambda i, be, nu: (i + 1, 0, 0)),
            pl.BlockSpec(memory_space=pl.ANY),
            pl.BlockSpec((1, d, f2), lambda i, be, nu: (be[i], 0, 0)),
            pl.BlockSpec((1, 1, f2), lambda i, be, nu: (be[i], 0, 0)),
            pl.BlockSpec((1, f2 // 2, d), lambda i, be, nu: (be[i], 0, 0)),
            pl.BlockSpec((1, 1, d), lambda i, be, nu: (be[i], 0, 0)),
        ],
        out_specs=pl.BlockSpec(memory_space=pl.ANY),
        scratch_shapes=[pltpu.VMEM(tile_rows, U32), pltpu.VMEM(tile_rows, U32),
                        pltpu.VMEM(tile_rows, U32), pltpu.VMEM(tile_rows, U32),
                        pltpu.VMEM((d, f2), BF16), pltpu.VMEM((f2 // 2, d), BF16),
                        pltpu.SemaphoreType.DMA((2,)), pltpu.SemaphoreType.DMA((2,))],
    )
    return pl.pallas_call(
        functools.partial(_moe_kernel, tm=tm),
        out_shape=jax.ShapeDtypeStruct((out_rows * TOKEN_ROWS, LANES), U32),
        grid_spec=grid_spec,
        compiler_params=_cparams("arbitrary"),
        name="moe_experts",
    )(blk_e, n_used, row_tok, row_tok, row_dst, row_dst, h2, w1, b1, w2, b2)


def _combine_kernel(*refs, tq, final):
    ys = refs[:TOP_K]
    if final:
        gate_ref, x_ref, g2_ref, fg_ref, o_ref = refs[TOP_K:]
    else:
        gate_ref, x_ref, g2_ref, o_ref = refs[TOP_K:]
    f = None
    d = x_ref.shape[1]
    for k in range(TOP_K):
        gcol = jnp.broadcast_to(gate_ref[k:k + 1, :], (LANES, tq)).T
        t = _load_token_tiles(ys[k]) * jnp.tile(gcol, (1, d // LANES))
        f = t if f is None else f + t
    xn = x_ref[...] + g2_ref[0] * f
    if final:
        xn = _rms(xn, fg_ref[...])
    o_ref[...] = xn


def _combine(ys, t_tot, t_off, gate, x, mod, g2_j, group_rows, final_g, tq):
    m, d = x.shape
    per_group = group_rows // tq
    final = final_g is not None
    in_specs = [pl.BlockSpec((tq * TOKEN_ROWS, LANES),
                             functools.partial(lambda i, k: ((k * t_tot + t_off) // tq + i, 0), k=k))
                for k in range(TOP_K)]
    in_specs += [
        pl.BlockSpec((TOP_K, tq), lambda i: (0, i)),
        pl.BlockSpec((tq, d), lambda i: (i, 0)),
        pl.BlockSpec((1, 1, d), lambda i: ((i // per_group) * 6 + g2_j, 0, 0)),
    ]
    args = [ys] * TOP_K + [gate, x, mod]
    if final:
        in_specs.append(pl.BlockSpec((1, d), lambda i: (0, 0)))
        args.append(final_g)
    return pl.pallas_call(
        functools.partial(_combine_kernel, tq=tq, final=final),
        out_shape=jax.ShapeDtypeStruct((m, d), F32),
        grid=(m // tq,),
        in_specs=in_specs,
        out_specs=pl.BlockSpec((tq, d), lambda i: (i, 0)),
        compiler_params=_cparams("parallel"),
        name="moe_combine",
    )(*args)


def _rope_tables(seq, rot_dim):
    t = np.arange(seq)
    row = (t // GRID_W).astype(np.float32)
    col = (t % GRID_W).astype(np.float32)
    axis_dim = rot_dim // 2
    quarter = axis_dim // 2
    inv = jnp.asarray(ROPE_THETA, F32) ** (-jnp.arange(0, axis_dim, 2, dtype=F32) / axis_dim)
    ang_r = jnp.asarray(row)[:, None] * inv
    ang_c = jnp.asarray(col)[:, None] * inv
    zeros = jnp.zeros_like(ang_r)
    cos = jnp.concatenate([jnp.cos(ang_r)] * 2 + [jnp.cos(ang_c)] * 2, axis=1)
    s_a = jnp.concatenate([-jnp.sin(ang_r), zeros, -jnp.sin(ang_c), zeros], axis=1)
    s_b = jnp.concatenate([zeros, jnp.sin(ang_r), zeros, jnp.sin(ang_c)], axis=1)
    pad = LANES - rot_dim
    if pad:
        cos, s_a, s_b = (jnp.pad(a, ((0, 0), (0, pad))) for a in (cos, s_a, s_b))
    del quarter
    return cos, s_a, s_b


def _route(idx, counts, tm):
    t_tot = idx.shape[1]
    n_assign = t_tot * TOP_K
    n_rows = n_assign + N_EXPERTS * tm
    n_tiles = n_rows // tm
    counts = counts.astype(jnp.int32)
    pad_need = (-counts) % tm
    flat_e = idx.T.reshape(-1)
    slot = jnp.arange(tm, dtype=jnp.int32)[None, :]
    pad_keys = jnp.where(slot < pad_need[:, None], jnp.arange(N_EXPERTS, dtype=jnp.int32)[:, None], N_EXPERTS)
    keys = jnp.concatenate([flat_e, pad_keys.reshape(-1)])
    pos_bits = (n_rows - 1).bit_length()
    packed = lax.sort((keys << pos_bits) | jnp.arange(n_rows, dtype=jnp.int32))
    skeys = packed >> pos_bits
    svals = packed & ((1 << pos_bits) - 1)
    real = svals < n_assign
    row_tok = jnp.where(real, svals // TOP_K, 0)
    rows = jnp.arange(n_rows, dtype=jnp.int32)
    row_dst = jnp.where(real, (svals % TOP_K) * t_tot + svals // TOP_K, n_assign + rows % tm)
    tile_e = skeys[::tm]
    n_used = jnp.sum(tile_e < N_EXPERTS).astype(jnp.int32)
    last_e = tile_e[jnp.maximum(n_used - 1, 0)]
    blk_e = jnp.where(tile_e < N_EXPERTS, tile_e, last_e).astype(jnp.int32)
    row_dst = jnp.concatenate([n_assign + jnp.arange(tm, dtype=jnp.int32), row_dst])
    row_tok = (row_tok * TOKEN_ROWS).reshape(n_tiles, 1, tm)
    row_dst = (row_dst * TOKEN_ROWS).reshape(n_tiles + 1, 1, tm)
    first = jnp.concatenate([jnp.ones((1,), jnp.int32), (blk_e[1:] != blk_e[:-1]).astype(jnp.int32)])
    meta = jnp.concatenate([n_used.reshape(1), first])
    return row_tok, row_dst, blk_e, meta


def _pick(m, pref):
    t = pref
    while m % t:
        t //= 2
    return t


def kernel(x, c, ctx, c_ctx, ada_w, ada_b, norm_mix_g, norm_ffn_g, ab_w_in, sgu_norm_g, sgu_w, sgu_b,
           gqa_q_norm_g, gqa_k_norm_g, ab_w_out, mla_w_in, mla_q_norm_g, mla_kv_norm_g, mla_w_uq, mla_w_ukv,
           mla_w_out, router_w, router_b, moe_w1, moe_b1, moe_w2, moe_b2, final_norm_g):
    bn, s, d = x.shape
    l = ctx.shape[1]
    depth = ada_w.shape[0]
    m_lat, m_ctx = bn * s, bn * l
    tm_lat = _pick(s, 1024)
    tm_ctx = _pick(m_ctx, 1024)
    tm_moe = 512
    tq_comb = _pick(int(np.gcd(s, m_ctx)), 512)

    n_cond = (bn + 1 + 7) // 8 * 8
    cond = jnp.concatenate([c, c_ctx[None, :], jnp.zeros((n_cond - bn - 1, d), F32)], axis=0)
    mod_all = _ada(cond, ada_w, ada_b)

    rope_gqa = _rope_tables(s, HEAD_DIM)
    rope_mla = _rope_tables(s, MLA_ROPE)
    w1_all = moe_w1.reshape(depth * N_EXPERTS, d, -1)
    w2_all = moe_w2.reshape(depth * N_EXPERTS, -1, d)
    b1_all = moe_b1.reshape(depth * N_EXPERTS, 1, -1)
    b2_all = moe_b2.reshape(depth * N_EXPERTS, 1, d)

    xl = x.reshape(m_lat, d)
    xc = ctx.reshape(m_ctx, d)
    for i in range(depth):
        ctx_out = i < depth - 1
        j = i // 2
        mod_lat = mod_all[i, :bn].reshape(bn * 6, 1, d)
        mod_ctx = mod_all[i, bn].reshape(6, 1, d)
        ng = norm_mix_g[i].reshape(1, d)
        if i % 2 == 0:
            w_in = ab_w_in[j].astype(BF16)
            w_out = ab_w_out[j].astype(BF16)
            sgu_g = sgu_norm_g[j].reshape(1, SGU_WIDTH)
            sgu_wb = sgu_w[j].astype(BF16)
            sgu_bfull = jnp.broadcast_to(sgu_b[j][:, :, None], (SGU_GROUPS, CHUNK, SGU_GROUP_CH))
            q_g = gqa_q_norm_g[j].reshape(1, HEAD_DIM)
            k_g = gqa_k_norm_g[j].reshape(1, HEAD_DIM)
            oa, q, k, v = _proj_ab(xl, mod_lat, (1, 0), s, ng, w_in, sgu_g, sgu_wb, sgu_bfull, q_g, k_g,
                                   rope_gqa, tm_lat)
            oac, qc, kc, vc = _proj_ab(xc, mod_ctx, (1, 0), m_ctx, ng, w_in, sgu_g, sgu_wb, sgu_bfull, q_g, k_g,
                                       None, tm_ctx)
            ob = _attention(q, [(kc, vc), (k, v)], bn, GQA_HEADS, GQA_KV_HEADS, HEAD_DIM, HEAD_DIM, ATTN_TQ)
            ins = [(oa, w_out[:SGU_WIDTH]), (ob, w_out[SGU_WIDTH:])]
            if ctx_out:
                obc = _attention(qc, [(kc, vc)], bn, GQA_HEADS, GQA_KV_HEADS, HEAD_DIM, HEAD_DIM, ATTN_TQ)
                ins_c = [(oac, w_out[:SGU_WIDTH]), (obc, w_out[SGU_WIDTH:])]
        else:
            w_in_f = mla_w_in[j]
            w_in = jnp.pad(w_in_f, ((0, 0), (0, MLA_IN_PAD - w_in_f.shape[1]))).astype(BF16)
            w_in_kv = w_in[:, MLA_Q_LORA:]
            w_out = mla_w_out[j].astype(BF16)
            q_g = mla_q_norm_g[j].reshape(1, MLA_Q_LORA)
            kv_g = mla_kv_norm_g[j].reshape(1, MLA_KV_LORA)
            w_uq = mla_w_uq[j].reshape(MLA_Q_LORA, MLA_HEADS, MLA_QK)
            w_uq = jnp.pad(w_uq, ((0, 0), (0, 0), (0, MLA_QK_PAD - MLA_QK)))
            w_uq = w_uq.reshape(MLA_Q_LORA, MLA_HEADS * MLA_QK_PAD).astype(BF16)
            w_ukv = mla_w_ukv[j].reshape(MLA_KV_LORA, MLA_HEADS, MLA_NOPE + MLA_V)
            w_uk = w_ukv[:, :, :MLA_NOPE].reshape(MLA_KV_LORA, MLA_HEADS * MLA_NOPE).astype(BF16)
            w_uv = w_ukv[:, :, MLA_NOPE:].reshape(MLA_KV_LORA, MLA_HEADS * MLA_V).astype(BF16)
            q, k, v = _proj_mla(xl, mod_lat, (1, 0), s, ng, w_in, q_g, w_uq, kv_g, w_uk, w_uv, rope_mla, True,
                                tm_lat)
            if ctx_out:
                qc, kc, vc = _proj_mla(xc, mod_ctx, (1, 0), m_ctx, ng, w_in, q_g, w_uq, kv_g, w_uk, w_uv, None,
                                       True, tm_ctx)
            else:
                kc, vc = _proj_mla(xc, mod_ctx, (1, 0), m_ctx, ng, w_in_kv, None, None, kv_g, w_uk, w_uv, None,
                                   False, tm_ctx)
            ob = _attention(q, [(kc, vc), (k, v)], bn, MLA_HEADS, MLA_HEADS, MLA_QK_PAD, MLA_V, ATTN_TQ)
            ins = [(ob, w_out)]
            if ctx_out:
                obc = _attention(qc, [(kc, vc)], bn, MLA_HEADS, MLA_HEADS, MLA_QK_PAD, MLA_V, ATTN_TQ)
                ins_c = [(obc, w_out)]

        nfg = norm_ffn_g[i].reshape(1, d)
        rwt = router_w[i].T
        rb = router_b[i].reshape(N_EXPERTS, 1)
        cnt0 = jnp.zeros((N_EXPERTS, LANES), F32)
        xl, h2, idx, gate, cnt = _out_router(ins, xl, mod_lat, (2, 4, 3), s, nfg, rwt, rb, cnt0, tm_lat)
        if ctx_out:
            xc, h2c, idx_c, gate_c, cnt = _out_router(ins_c, xc, mod_ctx, (2, 4, 3), m_ctx, nfg, rwt, rb, cnt, tm_ctx)
            h2 = jnp.concatenate([h2, h2c], axis=0)
            idx = jnp.concatenate([idx, idx_c], axis=1)
        t_tot = h2.shape[0] // TOKEN_ROWS
        n_assign = t_tot * TOP_K
        row_tok, row_dst, blk_e, n_used = _route(idx, cnt[:, 0], tm_moe)
        ys = _moe(h2, row_tok, row_dst, blk_e + i * N_EXPERTS, n_used, w1_all, b1_all, w2_all, b2_all,
                  n_assign + tm_moe, tm_moe)
        last = i == depth - 1
        fg = final_norm_g.reshape(1, d) if last else None
        xl = _combine(ys, t_tot, 0, gate, xl, mod_lat, 5, s, fg, tq_comb)
        if ctx_out:
            xc = _combine(ys, t_tot, m_lat, gate_c, xc, mod_ctx, 5, m_ctx, None, tq_comb)
    return xl.reshape(bn, s, d)
```

```python
import functools

import jax
import jax.numpy as jnp
import numpy as np
from jax import lax
from jax.experimental import pallas as pl
from jax.experimental.pallas import tpu as pltpu

F32 = jnp.float32
BF16 = jnp.bfloat16
U32 = jnp.uint32

GRID_W = 64
EPS = 1e-6
ROPE_THETA = 10000.0
CHUNK = 128
SGU_GROUPS = 4
SGU_GROUP_CH = 128
SGU_WIDTH = SGU_GROUPS * SGU_GROUP_CH
GQA_HEADS = 4
GQA_KV_HEADS = 2
HEAD_DIM = 128
GQA_Q_W = GQA_HEADS * HEAD_DIM
GQA_KV_W = GQA_KV_HEADS * HEAD_DIM
A_END = 2 * SGU_WIDTH
K_OFF = A_END + GQA_Q_W
V_OFF = K_OFF + GQA_KV_W
MLA_HEADS = 8
MLA_Q_LORA = 512
MLA_KV_LORA = 256
MLA_NOPE = 128
MLA_ROPE = 64
MLA_V = 128
MLA_QK = MLA_NOPE + MLA_ROPE
MLA_QK_PAD = 256
MLA_IN_PAD = 896
N_EXPERTS = 32
TOP_K = 4
D_EXPERT = 1024
SWIGLU_LIMIT = 7.0
SWIGLU_ALPHA = 1.702

LANES = 128
TOKEN_ROWS = 4
DMA_UNROLL = 64
ROW_SUB = 4
ATTN_TQ = 256
VMEM_LIMIT = 56 * 1024 * 1024


def _cparams(*sem):
    return pltpu.CompilerParams(dimension_semantics=sem, vmem_limit_bytes=VMEM_LIMIT)


def _dot(a, b):
    return jnp.dot(a, b, preferred_element_type=F32)


def _dot_nt(a, b):
    return lax.dot_general(a, b, (((1,), (1,)), ((), ())), preferred_element_type=F32)


def _split2(a):
    hi = a.astype(BF16)
    lo = (a - hi.astype(F32)).astype(BF16)
    return hi, lo


def _row_sub(tm):
    return ROW_SUB if tm % (ROW_SUB * CHUNK) == 0 else 1


def _rms(x, g):
    return x * lax.rsqrt(jnp.mean(x * x, axis=-1, keepdims=True) + EPS) * g


def _store_token_tiles(ref, val):
    n, d = val.shape
    h = d // 2
    lo = lax.bitcast_convert_type(val[:, :h].astype(BF16).astype(F32), U32) >> 16
    hi = lax.bitcast_convert_type(val[:, h:].astype(BF16).astype(F32), U32) & jnp.uint32(0xFFFF0000)
    packed = hi | lo
    for s in range(TOKEN_ROWS):
        ref[pl.ds(s, n, stride=TOKEN_ROWS), :] = packed[:, s * LANES:(s + 1) * LANES]


def _load_token_tiles(ref):
    n = ref.shape[0] // TOKEN_ROWS
    packed = jnp.concatenate([ref[pl.ds(s, n, stride=TOKEN_ROWS), :] for s in range(TOKEN_ROWS)], axis=1)
    lo = lax.bitcast_convert_type(packed << 16, F32)
    hi = lax.bitcast_convert_type(packed & jnp.uint32(0xFFFF0000), F32)
    return jnp.concatenate([lo, hi], axis=1)


def _rope(x, c, s_a, s_b, half):
    n = x.shape[-1]
    return x * c + pltpu.roll(x, n - half, 1) * s_a + pltpu.roll(x, half, 1) * s_b


def _ada_kernel(c_ref, w_ref, b_ref, o_ref):
    c = c_ref[...]
    sc = c * jax.nn.sigmoid(c)
    c_hi, c_lo = _split2(sc)
    w = w_ref[0]
    w_hi, w_lo = _split2(w)
    acc = _dot(c_hi, w_hi) + _dot(c_hi, w_lo) + _dot(c_lo, w_hi)
    o_ref[0] = acc + b_ref[0]


def _ada(cond, ada_w, ada_b):
    depth, d, n = ada_w.shape
    rows = cond.shape[0]
    tn = 1536
    return pl.pallas_call(
        _ada_kernel,
        out_shape=jax.ShapeDtypeStruct((depth, rows, n), F32),
        grid=(depth, n // tn),
        in_specs=[
            pl.BlockSpec((rows, d), lambda l, j: (0, 0)),
            pl.BlockSpec((1, d, tn), lambda l, j: (l, 0, j)),
            pl.BlockSpec((1, 1, tn), lambda l, j: (l, 0, j)),
        ],
        out_specs=pl.BlockSpec((1, rows, tn), lambda l, j: (l, 0, j)),
        compiler_params=_cparams("arbitrary", "arbitrary"),
        name="ada_mod",
    )(cond, ada_w, ada_b.reshape(depth, 1, n))


def _proj_ab_kernel(*refs, use_rope, tm, sub):
    if use_rope:
        (x_ref, sc_ref, sh_ref, g_ref, w_ref, sgug_ref, sguw_ref, sgub_ref, qg_ref, kg_ref,
         c_ref, sa_ref, sb_ref, oa_ref, q_ref, k_ref, v_ref) = refs
    else:
        (x_ref, sc_ref, sh_ref, g_ref, w_ref, sgug_ref, sguw_ref, sgub_ref, qg_ref, kg_ref,
         oa_ref, q_ref, k_ref, v_ref) = refs
    ts = tm // sub
    n_chunks = ts // CHUNK
    qg = qg_ref[...] * (HEAD_DIM ** -0.5)
    for b in range(sub):
        rows = slice(b * ts, (b + 1) * ts)
        x = x_ref[rows, :]
        h = _rms(x, g_ref[...]) * (1.0 + sc_ref[0]) + sh_ref[0]
        p = _dot(h.astype(BF16), w_ref[...])
        rope = (lambda t: _rope(t, c_ref[rows, :], sa_ref[rows, :], sb_ref[rows, :], HEAD_DIM // 4)) \
            if use_rope else (lambda t: t)

        for g in range(SGU_GROUPS):
            u = jax.nn.gelu(p[:, g * SGU_GROUP_CH:(g + 1) * SGU_GROUP_CH])
            v = jax.nn.gelu(p[:, SGU_WIDTH + g * SGU_GROUP_CH:SGU_WIDTH + (g + 1) * SGU_GROUP_CH])
            v = _rms(v, sgug_ref[:, g * SGU_GROUP_CH:(g + 1) * SGU_GROUP_CH]).astype(BF16)
            vcat = jnp.concatenate([v[c * CHUNK:(c + 1) * CHUNK, :] for c in range(n_chunks)], axis=1)
            mixed = _dot(sguw_ref[g], vcat)
            bias = sgub_ref[g]
            for c in range(n_chunks):
                gate = mixed[:, c * SGU_GROUP_CH:(c + 1) * SGU_GROUP_CH] + bias
                oa_ref[b * ts + c * CHUNK:b * ts + (c + 1) * CHUNK, g * SGU_GROUP_CH:(g + 1) * SGU_GROUP_CH] = (
                    u[c * CHUNK:(c + 1) * CHUNK, :] * gate).astype(oa_ref.dtype)

        for hd in range(GQA_HEADS):
            qh = rope(_rms(p[:, A_END + hd * HEAD_DIM:A_END + (hd + 1) * HEAD_DIM], qg))
            q_ref[rows, hd * HEAD_DIM:(hd + 1) * HEAD_DIM] = qh.astype(q_ref.dtype)
        for hd in range(GQA_KV_HEADS):
            kh = rope(_rms(p[:, K_OFF + hd * HEAD_DIM:K_OFF + (hd + 1) * HEAD_DIM], kg_ref[...]))
            k_ref[rows, hd * HEAD_DIM:(hd + 1) * HEAD_DIM] = kh.astype(k_ref.dtype)
        v_ref[rows, :] = p[:, V_OFF:].astype(v_ref.dtype)


def _proj_ab(x, mod, mod_rows, group_rows, norm_g, w_in, sgu_g, sgu_w, sgu_bfull, q_g, k_g, rope, tm):
    m, d = x.shape
    per_group = group_rows // tm
    sc_j, sh_j = mod_rows
    use_rope = rope is not None
    in_specs = [
        pl.BlockSpec((tm, d), lambda i: (i, 0)),
        pl.BlockSpec((1, 1, d), lambda i: ((i // per_group) * 6 + sc_j, 0, 0)),
        pl.BlockSpec((1, 1, d), lambda i: ((i // per_group) * 6 + sh_j, 0, 0)),
        pl.BlockSpec((1, d), lambda i: (0, 0)),
        pl.BlockSpec(w_in.shape, lambda i: (0, 0)),
        pl.BlockSpec((1, SGU_WIDTH), lambda i: (0, 0)),
        pl.BlockSpec(sgu_w.shape, lambda i: (0, 0, 0)),
        pl.BlockSpec(sgu_bfull.shape, lambda i: (0, 0, 0)),
        pl.BlockSpec((1, HEAD_DIM), lambda i: (0, 0)),
        pl.BlockSpec((1, HEAD_DIM), lambda i: (0, 0)),
    ]
    args = [x, mod, mod, norm_g, w_in, sgu_g, sgu_w, sgu_bfull, q_g, k_g]
    if use_rope:
        s_tiles = rope[0].shape[0] // tm
        for t in rope:
            in_specs.append(pl.BlockSpec((tm, HEAD_DIM), lambda i: (i % s_tiles, 0)))
            args.append(t)
    return pl.pallas_call(
        functools.partial(_proj_ab_kernel, use_rope=use_rope, tm=tm, sub=_row_sub(tm)),
        out_shape=(jax.ShapeDtypeStruct((m, SGU_WIDTH), BF16),
                   jax.ShapeDtypeStruct((m, GQA_Q_W), BF16),
                   jax.ShapeDtypeStruct((m, GQA_KV_W), BF16),
                   jax.ShapeDtypeStruct((m, GQA_KV_W), BF16)),
        grid=(m // tm,),
        in_specs=in_specs,
        out_specs=(pl.BlockSpec((tm, SGU_WIDTH), lambda i: (i, 0)),
                   pl.BlockSpec((tm, GQA_Q_W), lambda i: (i, 0)),
                   pl.BlockSpec((tm, GQA_KV_W), lambda i: (i, 0)),
                   pl.BlockSpec((tm, GQA_KV_W), lambda i: (i, 0))),
        compiler_params=_cparams("parallel"),
        name="proj_ab",
    )(*args)


def _proj_mla_kernel(*refs, with_q, use_rope, sub):
    it = iter(refs)
    x_ref, sc_ref, sh_ref, g_ref, w_ref = (next(it) for _ in range(5))
    if with_q:
        qg_ref, wuq_ref = next(it), next(it)
    kvg_ref, wuk_ref, wuv_ref = next(it), next(it), next(it)
    if use_rope:
        c_ref, sa_ref, sb_ref = next(it), next(it), next(it)
    if with_q:
        q_ref = next(it)
    k_ref, v_ref = next(it), next(it)

    ts = x_ref.shape[0] // sub
    off = MLA_Q_LORA if with_q else 0
    for b in range(sub):
        rows = slice(b * ts, (b + 1) * ts)
        rope = (lambda t: _rope(t, c_ref[rows, :], sa_ref[rows, :], sb_ref[rows, :], MLA_ROPE // 4)) \
            if use_rope else (lambda t: t)
        x = x_ref[rows, :]
        h = _rms(x, g_ref[...]) * (1.0 + sc_ref[0]) + sh_ref[0]
        p = _dot(h.astype(BF16), w_ref[...])
        ckv = _rms(p[:, off:off + MLA_KV_LORA], kvg_ref[...]).astype(BF16)
        kr = rope(p[:, off + MLA_KV_LORA:off + MLA_KV_LORA + LANES])
        kr = kr.astype(k_ref.dtype)
        k_nope = _dot(ckv, wuk_ref[...])
        v_ref[rows, :] = _dot(ckv, wuv_ref[...]).astype(v_ref.dtype)
        for hd in range(MLA_HEADS):
            k_ref[rows, hd * MLA_QK_PAD:hd * MLA_QK_PAD + MLA_NOPE] = (
                k_nope[:, hd * MLA_NOPE:(hd + 1) * MLA_NOPE].astype(k_ref.dtype))
            k_ref[rows, hd * MLA_QK_PAD + MLA_NOPE:(hd + 1) * MLA_QK_PAD] = kr
        if with_q:
            cq = _rms(p[:, :MLA_Q_LORA], qg_ref[...]).astype(BF16)
            q = _dot(cq, wuq_ref[...]) * (MLA_QK ** -0.5)
            for hd in range(MLA_HEADS):
                q_ref[rows, hd * MLA_QK_PAD:hd * MLA_QK_PAD + MLA_NOPE] = (
                    q[:, hd * MLA_QK_PAD:hd * MLA_QK_PAD + MLA_NOPE].astype(q_ref.dtype))
                qr = rope(q[:, hd * MLA_QK_PAD + MLA_NOPE:(hd + 1) * MLA_QK_PAD])
                q_ref[rows, hd * MLA_QK_PAD + MLA_NOPE:(hd + 1) * MLA_QK_PAD] = qr.astype(q_ref.dtype)


def _proj_mla(x, mod, mod_rows, group_rows, norm_g, w_in, q_g, w_uq, kv_g, w_uk, w_uv, rope, with_q, tm):
    m, d = x.shape
    per_group = group_rows // tm
    sc_j, sh_j = mod_rows
    use_rope = rope is not None
    const2 = lambda i: (0, 0)
    in_specs = [
        pl.BlockSpec((tm, d), lambda i: (i, 0)),
        pl.BlockSpec((1, 1, d), lambda i: ((i // per_group) * 6 + sc_j, 0, 0)),
        pl.BlockSpec((1, 1, d), lambda i: ((i // per_group) * 6 + sh_j, 0, 0)),
        pl.BlockSpec((1, d), const2),
        pl.BlockSpec(w_in.shape, const2),
    ]
    args = [x, mod, mod, norm_g, w_in]
    if with_q:
        in_specs += [pl.BlockSpec(q_g.shape, const2), pl.BlockSpec(w_uq.shape, const2)]
        args += [q_g, w_uq]
    in_specs += [pl.BlockSpec(kv_g.shape, const2), pl.BlockSpec(w_uk.shape, const2),
                 pl.BlockSpec(w_uv.shape, const2)]
    args += [kv_g, w_uk, w_uv]
    if use_rope:
        s_tiles = rope[0].shape[0] // tm
        for t in rope:
            in_specs.append(pl.BlockSpec((tm, LANES), lambda i: (i % s_tiles, 0)))
            args.append(t)
    kw = MLA_HEADS * MLA_QK_PAD
    vw = MLA_HEADS * MLA_V
    out_shape = [jax.ShapeDtypeStruct((m, kw), BF16), jax.ShapeDtypeStruct((m, vw), BF16)]
    out_specs = [pl.BlockSpec((tm, kw), lambda i: (i, 0)), pl.BlockSpec((tm, vw), lambda i: (i, 0))]
    if with_q:
        out_shape = [jax.ShapeDtypeStruct((m, kw), BF16)] + out_shape
        out_specs = [pl.BlockSpec((tm, kw), lambda i: (i, 0))] + out_specs
    return pl.pallas_call(
        functools.partial(_proj_mla_kernel, with_q=with_q, use_rope=use_rope, sub=1),
        out_shape=tuple(out_shape),
        grid=(m // tm,),
        in_specs=in_specs,
        out_specs=tuple(out_specs),
        compiler_params=_cparams("parallel"),
        name="proj_mla",
    )(*args)


def _attn_kernel(*refs, n_seg, tq, n_q):
    q_ref = refs[0]
    kv = refs[1:1 + 2 * n_seg]
    o_ref = refs[1 + 2 * n_seg]

    for c in range(n_q):
        q = q_ref[c * tq:(c + 1) * tq, :]
        ss = [_dot_nt(q, kv[2 * j][...]) for j in range(n_seg)]
        mx = ss[0].max(axis=-1, keepdims=True)
        for s in ss[1:]:
            mx = jnp.maximum(mx, s.max(axis=-1, keepdims=True))
        acc = None
        den = None
        for j, s in enumerate(ss):
            e = jnp.exp(s - mx)
            d = e.sum(axis=-1, keepdims=True)
            o = _dot(e.astype(BF16), kv[2 * j + 1][...])
            acc = o if acc is None else acc + o
            den = d if den is None else den + d
        o_ref[c * tq:(c + 1) * tq, :] = (acc / den).astype(o_ref.dtype)


def _attention(q, segs, batch, heads, kv_heads, dk, dv, tq):
    t_q = q.shape[0] // batch
    group = heads // kv_heads
    in_specs = [pl.BlockSpec((t_q, dk), lambda b, h: (b, h))]
    args = [q]
    for k, v in segs:
        t_k = k.shape[0] // batch
        in_specs.append(pl.BlockSpec((t_k, dk), lambda b, h: (b, h // group)))
        in_specs.append(pl.BlockSpec((t_k, dv), lambda b, h: (b, h // group)))
        args += [k, v]
    tq = min(tq, t_q)
    return pl.pallas_call(
        functools.partial(_attn_kernel, n_seg=len(segs), tq=tq, n_q=t_q // tq),
        out_shape=jax.ShapeDtypeStruct((q.shape[0], heads * dv), BF16),
        grid=(batch, heads),
        in_specs=in_specs,
        out_specs=pl.BlockSpec((t_q, dv), lambda b, h: (b, h)),
        compiler_params=_cparams("parallel", "parallel"),
        name="attention",
    )(*args)


def _out_router_kernel(*refs, n_in, tm, sub):
    ins = refs[:2 * n_in]
    (x_ref, g1_ref, sc_ref, sh_ref, ng_ref, rw_ref, rb_ref, cnt0_ref,
     xo_ref, h2_ref, idx_ref, gate_ref, cnt_ref) = refs[2 * n_in:]
    i = pl.program_id(0)

    @pl.when(i == 0)
    def _():
        cnt_ref[...] = cnt0_ref[...]

    ts = tm // sub
    w_hi, w_lo = _split2(rw_ref[...])
    for b in range(sub):
        rows = slice(b * ts, (b + 1) * ts)
        y = None
        for j in range(n_in):
            t = _dot(ins[2 * j][rows, :], ins[2 * j + 1][...])
            y = t if y is None else y + t
        xn = x_ref[rows, :] + g1_ref[0] * y
        xo_ref[rows, :] = xn
        h2 = _rms(xn, ng_ref[...]) * (1.0 + sc_ref[0]) + sh_ref[0]
        _store_token_tiles(h2_ref.at[pl.ds(b * ts * TOKEN_ROWS, ts * TOKEN_ROWS)], h2)

        h_hi, h_lo = _split2(h2)
        logits = _dot_nt(w_hi, h_hi) + _dot_nt(w_hi, h_lo) + _dot_nt(w_lo, h_hi) + rb_ref[...]

        e_iota = lax.broadcasted_iota(jnp.int32, logits.shape, 0)
        work = logits
        tops, idxs = [], []
        for _ in range(TOP_K):
            mx = work.max(axis=0, keepdims=True)
            ix = jnp.where(work == mx, e_iota, N_EXPERTS).min(axis=0, keepdims=True)
            tops.append(mx)
            idxs.append(ix)
            work = jnp.where(e_iota == ix, -jnp.inf, work)
        exps = [jnp.exp(t - tops[0]) for t in tops]
        den = exps[0] + exps[1] + exps[2] + exps[3]

        for k in range(TOP_K):
            idx_ref[k:k + 1, rows] = idxs[k]
            gate_ref[k:k + 1, rows] = exps[k] / den
        cnt = sum((e_iota == ix).astype(F32) for ix in idxs)
        cnt_ref[...] = cnt_ref[...] + cnt.sum(axis=1, keepdims=True)


def _out_router(ins, x, mod, mod_rows, group_rows, norm_g, router_wt, router_b, cnt0, tm):
    m, d = x.shape
    per_group = group_rows // tm
    g1_j, sc_j, sh_j = mod_rows
    in_specs, args = [], []
    for a, w in ins:
        in_specs += [pl.BlockSpec((tm, a.shape[1]), lambda i: (i, 0)), pl.BlockSpec(w.shape, lambda i: (0, 0))]
        args += [a, w]
    mod_spec = lambda j: pl.BlockSpec((1, 1, d), lambda i: ((i // per_group) * 6 + j, 0, 0))
    in_specs += [
        pl.BlockSpec((tm, d), lambda i: (i, 0)),
        mod_spec(g1_j), mod_spec(sc_j), mod_spec(sh_j),
        pl.BlockSpec((1, d), lambda i: (0, 0)),
        pl.BlockSpec(router_wt.shape, lambda i: (0, 0)),
        pl.BlockSpec(router_b.shape, lambda i: (0, 0)),
        pl.BlockSpec(cnt0.shape, lambda i: (0, 0)),
    ]
    args += [x, mod, mod, mod, norm_g, router_wt, router_b, cnt0]
    small = lambda dt: jax.ShapeDtypeStruct((TOP_K, m), dt)
    small_spec = pl.BlockSpec((TOP_K, tm), lambda i: (0, i))
    return pl.pallas_call(
        functools.partial(_out_router_kernel, n_in=len(ins), tm=tm, sub=1),
        out_shape=(jax.ShapeDtypeStruct((m, d), F32), jax.ShapeDtypeStruct((m * TOKEN_ROWS, LANES), U32),
                   small(jnp.int32), small(F32),
                   jax.ShapeDtypeStruct(cnt0.shape, F32)),
        grid=(m // tm,),
        in_specs=in_specs,
        out_specs=(pl.BlockSpec((tm, d), lambda i: (i, 0)),
                   pl.BlockSpec((tm * TOKEN_ROWS, LANES), lambda i: (i, 0)),
                   small_spec, small_spec,
                   pl.BlockSpec(cnt0.shape, lambda i: (0, 0))),
        compiler_params=_cparams("arbitrary"),
        name="out_router",
    )(*args)


def _moe_kernel(blk_e_ref, meta_ref, tok0_ref, tokn_ref, dstp_ref, dstc_ref, h_hbm, w1_ref, b1_ref, w2_ref, b2_ref,
                o_hbm, x0, x1, y0, y1, w1b, w2b, gsem, ssem, *, tm):
    i = pl.program_id(0)
    n_used = meta_ref[0]

    @pl.when(jnp.logical_and(i < n_used, meta_ref[1 + i] == 1))
    def _():
        w1b[...] = w1_ref[0].astype(BF16)
        w2b[...] = w2_ref[0].astype(BF16)
    xs, ys = (x0, x1), (y0, y1)

    def tile_at(ref, row):
        return ref.at[pl.ds(pl.multiple_of(row, TOKEN_ROWS), TOKEN_ROWS)]

    def for_tokens(fn):
        def body(b, carry):
            for j in range(DMA_UNROLL):
                fn(b * DMA_UNROLL + j, j % 2)
            return carry
        lax.fori_loop(0, tm // DMA_UNROLL, body, 0)

    def gather(tok_ref, xbuf, sem):
        for_tokens(lambda r, prio: pltpu.make_async_copy(
            tile_at(h_hbm, tok_ref[0, 0, r]), tile_at(xbuf, r * TOKEN_ROWS), sem).start(priority=prio))

    def scatter(dst_ref, ybuf, sem):
        for_tokens(lambda r, prio: pltpu.make_async_copy(
            tile_at(ybuf, r * TOKEN_ROWS), tile_at(o_hbm, dst_ref[0, 0, r]), sem).start(priority=prio))

    def wait_gather(xbuf, sem):
        pltpu.make_async_copy(h_hbm.at[pl.ds(0, tm * TOKEN_ROWS)], xbuf, sem).wait()

    def wait_scatter(ybuf, sem):
        pltpu.make_async_copy(ybuf, o_hbm.at[pl.ds(0, tm * TOKEN_ROWS)], sem).wait()

    @pl.when(i == 0)
    def _():
        gather(tok0_ref, x0, gsem.at[0])
        y1[...] = jnp.zeros_like(y1)

    def step(p):
        xb, xn, yb, yp = xs[p], xs[1 - p], ys[p], ys[1 - p]
        wait_gather(xb, gsem.at[p])

        @pl.when(i >= 1)
        def _():
            wait_scatter(yb, ssem.at[p])

        @pl.when(n_used > 0)
        def _():
            scatter(dstp_ref, yp, ssem.at[1 - p])
            gather(tokn_ref, xn, gsem.at[1 - p])

        x = _load_token_tiles(xb).astype(BF16)
        hid = _dot(x, w1b[...]) + b1_ref[0]
        f = hid.shape[1] // 2
        g = jnp.minimum(hid[:, :f], SWIGLU_LIMIT)
        lin = jnp.clip(hid[:, f:], -SWIGLU_LIMIT, SWIGLU_LIMIT)
        act = g * jax.nn.sigmoid(SWIGLU_ALPHA * g) * (lin + 1.0)
        _store_token_tiles(yb, _dot(act.astype(BF16), w2b[...]) + b2_ref[0])

        @pl.when(i == n_used - 1)
        def _():
            wait_scatter(yp, ssem.at[1 - p])
            scatter(dstc_ref, yb, ssem.at[p])
            wait_scatter(yb, ssem.at[p])
            wait_gather(xn, gsem.at[1 - p])

    for p in range(2):
        @pl.when(jnp.logical_and(i < n_used, i % 2 == p))
        def _(p=p):
            step(p)


def _moe(h2, row_tok, row_dst, blk_e, n_used, w1, b1, w2, b2, out_rows, tm):
    n_tiles = row_tok.shape[0]
    d = w1.shape[1]
    f2 = w1.shape[2]
    tile_rows = (tm * TOKEN_ROWS, LANES)
    smem = lambda fn: pl.BlockSpec((1, 1, tm), fn, memory_space=pltpu.SMEM)
    grid_spec = pltpu.PrefetchScalarGridSpec(
        num_scalar_prefetch=2,
        grid=(n_tiles,),
        in_specs=[
            smem(lambda i, be, nu: (0, 0, 0)),
            smem(lambda i, be, nu: (jnp.minimum(i + 1, nu[0] - 1), 0, 0)),
            smem(lambda i, be, nu: (i, 0, 0)),
            smem(lambda i, be, nu: (i + 1, 0, 0)),
            pl.BlockSpec(memory_space=pl.ANY),
            pl.BlockSpec((1, d, f2), lambda i, be, nu: (be[i], 0, 0)),
            pl.BlockSpec((1, 1, f2), lambda i, be, nu: (be[i], 0, 0)),
            pl.BlockSpec((1, f2 // 2, d), lambda i, be, nu: (be[i], 0, 0)),
            pl.BlockSpec((1, 1, d), lambda i, be, nu: (be[i], 0, 0)),
        ],
        out_specs=pl.BlockSpec(memory_space=pl.ANY),
        scratch_shapes=[pltpu.VMEM(tile_rows, U32), pltpu.VMEM(tile_rows, U32),
                        pltpu.VMEM(tile_rows, U32), pltpu.VMEM(tile_rows, U32),
                        pltpu.VMEM((d, f2), BF16), pltpu.VMEM((f2 // 2, d), BF16),
                        pltpu.SemaphoreType.DMA((2,)), pltpu.SemaphoreType.DMA((2,))],
    )
    return pl.pallas_call(
        functools.partial(_moe_kernel, tm=tm),
        out_shape=jax.ShapeDtypeStruct((out_rows * TOKEN_ROWS, LANES), U32),
        grid_spec=grid_spec,
        compiler_params=_cparams("arbitrary"),
        name="moe_experts",
    )(blk_e, n_used, row_tok, row_tok, row_dst, row_dst, h2, w1, b1, w2, b2)


def _combine_kernel(*refs, tq, final):
    ys = refs[:TOP_K]
    if final:
        gate_ref, x_ref, g2_ref, fg_ref, o_ref = refs[TOP_K:]
    else:
        gate_ref, x_ref, g2_ref, o_ref = refs[TOP_K:]
    f = None
    d = x_ref.shape[1]
    for k in range(TOP_K):
        gcol = jnp.broadcast_to(gate_ref[k:k + 1, :], (LANES, tq)).T
        t = _load_token_tiles(ys[k]) * jnp.tile(gcol, (1, d // LANES))
        f = t if f is None else f + t
    xn = x_ref[...] + g2_ref[0] * f
    if final:
        xn = _rms(xn, fg_ref[...])
    o_ref[...] = xn


def _combine(ys, t_tot, t_off, gate, x, mod, g2_j, group_rows, final_g, tq):
    m, d = x.shape
    per_group = group_rows // tq
    final = final_g is not None
    in_specs = [pl.BlockSpec((tq * TOKEN_ROWS, LANES),
                             functools.partial(lambda i, k: ((k * t_tot + t_off) // tq + i, 0), k=k))
                for k in range(TOP_K)]
    in_specs += [
        pl.BlockSpec((TOP_K, tq), lambda i: (0, i)),
        pl.BlockSpec((tq, d), lambda i: (i, 0)),
        pl.BlockSpec((1, 1, d), lambda i: ((i // per_group) * 6 + g2_j, 0, 0)),
    ]
    args = [ys] * TOP_K + [gate, x, mod]
    if final:
        in_specs.append(pl.BlockSpec((1, d), lambda i: (0, 0)))
        args.append(final_g)
    return pl.pallas_call(
        functools.partial(_combine_kernel, tq=tq, final=final),
        out_shape=jax.ShapeDtypeStruct((m, d), F32),
        grid=(m // tq,),
        in_specs=in_specs,
        out_specs=pl.BlockSpec((tq, d), lambda i: (i, 0)),
        compiler_params=_cparams("parallel"),
        name="moe_combine",
    )(*args)


def _rope_tables(seq, rot_dim):
    t = np.arange(seq)
    row = (t // GRID_W).astype(np.float32)
    col = (t % GRID_W).astype(np.float32)
    axis_dim = rot_dim // 2
    quarter = axis_dim // 2
    inv = jnp.asarray(ROPE_THETA, F32) ** (-jnp.arange(0, axis_dim, 2, dtype=F32) / axis_dim)
    ang_r = jnp.asarray(row)[:, None] * inv
    ang_c = jnp.asarray(col)[:, None] * inv
    zeros = jnp.zeros_like(ang_r)
    cos = jnp.concatenate([jnp.cos(ang_r)] * 2 + [jnp.cos(ang_c)] * 2, axis=1)
    s_a = jnp.concatenate([-jnp.sin(ang_r), zeros, -jnp.sin(ang_c), zeros], axis=1)
    s_b = jnp.concatenate([zeros, jnp.sin(ang_r), zeros, jnp.sin(ang_c)], axis=1)
    pad = LANES - rot_dim
    if pad:
        cos, s_a, s_b = (jnp.pad(a, ((0, 0), (0, pad))) for a in (cos, s_a, s_b))
    del quarter
    return cos, s_a, s_b


def _route(idx, counts, tm):
    t_tot = idx.shape[1]
    n_assign = t_tot * TOP_K
    n_rows = n_assign + N_EXPERTS * tm
    n_tiles = n_rows // tm
    counts = counts.astype(jnp.int32)
    pad_need = (-counts) % tm
    flat_e = idx.T.reshape(-1)
    slot = jnp.arange(tm, dtype=jnp.int32)[None, :]
    pad_keys = jnp.where(slot < pad_need[:, None], jnp.arange(N_EXPERTS, dtype=jnp.int32)[:, None], N_EXPERTS)
    keys = jnp.concatenate([flat_e, pad_keys.reshape(-1)])
    pos_bits = (n_rows - 1).bit_length()
    packed = lax.sort((keys << pos_bits) | jnp.arange(n_rows, dtype=jnp.int32))
    skeys = packed >> pos_bits
    svals = packed & ((1 << pos_bits) - 1)
    real = svals < n_assign
    row_tok = jnp.where(real, svals // TOP_K, 0)
    rows = jnp.arange(n_rows, dtype=jnp.int32)
    row_dst = jnp.where(real, (svals % TOP_K) * t_tot + svals // TOP_K, n_assign + rows % tm)
    tile_e = skeys[::tm]
    n_used = jnp.sum(tile_e < N_EXPERTS).astype(jnp.int32)
    last_e = tile_e[jnp.maximum(n_used - 1, 0)]
    blk_e = jnp.where(tile_e < N_EXPERTS, tile_e, last_e).astype(jnp.int32)
    row_dst = jnp.concatenate([n_assign + jnp.arange(tm, dtype=jnp.int32), row_dst])
    row_tok = (row_tok * TOKEN_ROWS).reshape(n_tiles, 1, tm)
    row_dst = (row_dst * TOKEN_ROWS).reshape(n_tiles + 1, 1, tm)
    first = jnp.concatenate([jnp.ones((1,), jnp.int32), (blk_e[1:] != blk_e[:-1]).astype(jnp.int32)])
    meta = jnp.concatenate([n_used.reshape(1), first])
    return row_tok, row_dst, blk_e, meta


def _pick(m, pref):
    t = pref
    while m % t:
        t //= 2
    return t


def kernel(x, c, ctx, c_ctx, ada_w, ada_b, norm_mix_g, norm_ffn_g, ab_w_in, sgu_norm_g, sgu_w, sgu_b,
           gqa_q_norm_g, gqa_k_norm_g, ab_w_out, mla_w_in, mla_q_norm_g, mla_kv_norm_g, mla_w_uq, mla_w_ukv,
           mla_w_out, router_w, router_b, moe_w1, moe_b1, moe_w2, moe_b2, final_norm_g):
    bn, s, d = x.shape
    l = ctx.shape[1]
    depth = ada_w.shape[0]
    m_lat, m_ctx = bn * s, bn * l
    tm_lat = _pick(s, 1024)
    tm_ctx = _pick(m_ctx, 1024)
    tm_moe = 512
    tq_comb = _pick(int(np.gcd(s, m_ctx)), 512)

    n_cond = (bn + 1 + 7) // 8 * 8
    cond = jnp.concatenate([c, c_ctx[None, :], jnp.zeros((n_cond - bn - 1, d), F32)], axis=0)
    mod_all = _ada(cond, ada_w, ada_b)

    rope_gqa = _rope_tables(s, HEAD_DIM)
    rope_mla = _rope_tables(s, MLA_ROPE)
    w1_all = moe_w1.reshape(depth * N_EXPERTS, d, -1)
    w2_all = moe_w2.reshape(depth * N_EXPERTS, -1, d)
    b1_all = moe_b1.reshape(depth * N_EXPERTS, 1, -1)
    b2_all = moe_b2.reshape(depth * N_EXPERTS, 1, d)

    xl = x.reshape(m_lat, d)
    xc = ctx.reshape(m_ctx, d)
    for i in range(depth):
        ctx_out = i < depth - 1
        j = i // 2
        mod_lat = mod_all[i, :bn].reshape(bn * 6, 1, d)
        mod_ctx = mod_all[i, bn].reshape(6, 1, d)
        ng = norm_mix_g[i].reshape(1, d)
        if i % 2 == 0:
            w_in = ab_w_in[j].astype(BF16)
            w_out = ab_w_out[j].astype(BF16)
            sgu_g = sgu_norm_g[j].reshape(1, SGU_WIDTH)
            sgu_wb = sgu_w[j].astype(BF16)
            sgu_bfull = jnp.broadcast_to(sgu_b[j][:, :, None], (SGU_GROUPS, CHUNK, SGU_GROUP_CH))
            q_g = gqa_q_norm_g[j].reshape(1, HEAD_DIM)
            k_g = gqa_k_norm_g[j].reshape(1, HEAD_DIM)
            oa, q, k, v = _proj_ab(xl, mod_lat, (1, 0), s, ng, w_in, sgu_g, sgu_wb, sgu_bfull, q_g, k_g,
                                   rope_gqa, tm_lat)
            oac, qc, kc, vc = _proj_ab(xc, mod_ctx, (1, 0), m_ctx, ng, w_in, sgu_g, sgu_wb, sgu_bfull, q_g, k_g,
                                       None, tm_ctx)
            ob = _attention(q, [(kc, vc), (k, v)], bn, GQA_HEADS, GQA_KV_HEADS, HEAD_DIM, HEAD_DIM, ATTN_TQ)
            ins = [(oa, w_out[:SGU_WIDTH]), (ob, w_out[SGU_WIDTH:])]
            if ctx_out:
                obc = _attention(qc, [(kc, vc)], bn, GQA_HEADS, GQA_KV_HEADS, HEAD_DIM, HEAD_DIM, ATTN_TQ)
                ins_c = [(oac, w_out[:SGU_WIDTH]), (obc, w_out[SGU_WIDTH:])]
        else:
            w_in_f = mla_w_in[j]
            w_in = jnp.pad(w_in_f, ((0, 0), (0, MLA_IN_PAD - w_in_f.shape[1]))).astype(BF16)
            w_in_kv = w_in[:, MLA_Q_LORA:]
            w_out = mla_w_out[j].astype(BF16)
            q_g = mla_q_norm_g[j].reshape(1, MLA_Q_LORA)
            kv_g = mla_kv_norm_g[j].reshape(1, MLA_KV_LORA)
            w_uq = mla_w_uq[j].reshape(MLA_Q_LORA, MLA_HEADS, MLA_QK)
            w_uq = jnp.pad(w_uq, ((0, 0), (0, 0), (0, MLA_QK_PAD - MLA_QK)))
            w_uq = w_uq.reshape(MLA_Q_LORA, MLA_HEADS * MLA_QK_PAD).astype(BF16)
            w_ukv = mla_w_ukv[j].reshape(MLA_KV_LORA, MLA_HEADS, MLA_NOPE + MLA_V)
            w_uk = w_ukv[:, :, :MLA_NOPE].reshape(MLA_KV_LORA, MLA_HEADS * MLA_NOPE).astype(BF16)
            w_uv = w_ukv[:, :, MLA_NOPE:].reshape(MLA_KV_LORA, MLA_HEADS * MLA_V).astype(BF16)
            q, k, v = _proj_mla(xl, mod_lat, (1, 0), s, ng, w_in, q_g, w_uq, kv_g, w_uk, w_uv, rope_mla, True,
                                tm_lat)
            if ctx_out:
                qc, kc, vc = _proj_mla(xc, mod_ctx, (1, 0), m_ctx, ng, w_in, q_g, w_uq, kv_g, w_uk, w_uv, None,
                                       True, tm_ctx)
            else:
                kc, vc = _proj_mla(xc, mod_ctx, (1, 0), m_ctx, ng, w_in_kv, None, None, kv_g, w_uk, w_uv, None,
                                   False, tm_ctx)
            ob = _attention(q, [(kc, vc), (k, v)], bn, MLA_HEADS, MLA_HEADS, MLA_QK_PAD, MLA_V, ATTN_TQ)
            ins = [(ob, w_out)]
            if ctx_out:
                obc = _attention(qc, [(kc, vc)], bn, MLA_HEADS, MLA_HEADS, MLA_QK_PAD, MLA_V, ATTN_TQ)
                ins_c = [(obc, w_out)]

        nfg = norm_ffn_g[i].reshape(1, d)
        rwt = router_w[i].T
        rb = router_b[i].reshape(N_EXPERTS, 1)
        cnt0 = jnp.zeros((N_EXPERTS, LANES), F32)
        xl, h2, idx, gate, cnt = _out_router(ins, xl, mod_lat, (2, 4, 3), s, nfg, rwt, rb, cnt0, tm_lat)
        if ctx_out:
            xc, h2c, idx_c, gate_c, cnt = _out_router(ins_c, xc, mod_ctx, (2, 4, 3), m_ctx, nfg, rwt, rb, cnt, tm_ctx)
            h2 = jnp.concatenate([h2, h2c], axis=0)
            idx = jnp.concatenate([idx, idx_c], axis=1)
        t_tot = h2.shape[0] // TOKEN_ROWS
        n_assign = t_tot * TOP_K
        row_tok, row_dst, blk_e, n_used = _route(idx, cnt[:, 0], tm_moe)
        ys = _moe(h2, row_tok, row_dst, blk_e + i * N_EXPERTS, n_used, w1_all, b1_all, w2_all, b2_all,
                  n_assign + tm_moe, tm_moe)
        last = i == depth - 1
        fg = final_norm_g.reshape(1, d) if last else None
        xl = _combine(ys, t_tot, 0, gate, xl, mod_lat, 5, s, fg, tq_comb)
        if ctx_out:
            xc = _combine(ys, t_tot, m_lat, gate_c, xc, mod_ctx, 5, m_ctx, None, tq_comb)
    return xl.reshape(bn, s, d)
```

```python
import functools

import jax
import jax.numpy as jnp
import numpy as np
from jax import lax
from jax.experimental import pallas as pl
from jax.experimental.pallas import tpu as pltpu

F32 = jnp.float32
BF16 = jnp.bfloat16
U32 = jnp.uint32

GRID_W = 64
EPS = 1e-6
ROPE_THETA = 10000.0
CHUNK = 128
SGU_GROUPS = 4
SGU_GROUP_CH = 128
SGU_WIDTH = SGU_GROUPS * SGU_GROUP_CH
GQA_HEADS = 4
GQA_KV_HEADS = 2
HEAD_DIM = 128
GQA_Q_W = GQA_HEADS * HEAD_DIM
GQA_KV_W = GQA_KV_HEADS * HEAD_DIM
A_END = 2 * SGU_WIDTH
K_OFF = A_END + GQA_Q_W
V_OFF = K_OFF + GQA_KV_W
MLA_HEADS = 8
MLA_Q_LORA = 512
MLA_KV_LORA = 256
MLA_NOPE = 128
MLA_ROPE = 64
MLA_V = 128
MLA_QK = MLA_NOPE + MLA_ROPE
MLA_QK_PAD = 256
MLA_IN_PAD = 896
N_EXPERTS = 32
TOP_K = 4
D_EXPERT = 1024
SWIGLU_LIMIT = 7.0
SWIGLU_ALPHA = 1.702

LANES = 128
TOKEN_ROWS = 4
DMA_UNROLL = 64
ROW_SUB = 4
ATTN_TQ = 256
VMEM_LIMIT = 56 * 1024 * 1024


def _cparams(*sem):
    return pltpu.CompilerParams(dimension_semantics=sem, vmem_limit_bytes=VMEM_LIMIT)


def _dot(a, b):
    return jnp.dot(a, b, preferred_element_type=F32)


def _dot_nt(a, b):
    return lax.dot_general(a, b, (((1,), (1,)), ((), ())), preferred_element_type=F32)


def _split2(a):
    hi = a.astype(BF16)
    lo = (a - hi.astype(F32)).astype(BF16)
    return hi, lo


def _row_sub(tm):
    return ROW_SUB if tm % (ROW_SUB * CHUNK) == 0 else 1


def _rms(x, g):
    return x * lax.rsqrt(jnp.mean(x * x, axis=-1, keepdims=True) + EPS) * g


def _store_token_tiles(ref, val):
    n, d = val.shape
    h = d // 2
    lo = lax.bitcast_convert_type(val[:, :h].astype(BF16).astype(F32), U32) >> 16
    hi = lax.bitcast_convert_type(val[:, h:].astype(BF16).astype(F32), U32) & jnp.uint32(0xFFFF0000)
    packed = hi | lo
    for s in range(TOKEN_ROWS):
        ref[pl.ds(s, n, stride=TOKEN_ROWS), :] = packed[:, s * LANES:(s + 1) * LANES]


def _load_token_tiles(ref):
    n = ref.shape[0] // TOKEN_ROWS
    packed = jnp.concatenate([ref[pl.ds(s, n, stride=TOKEN_ROWS), :] for s in range(TOKEN_ROWS)], axis=1)
    lo = lax.bitcast_convert_type(packed << 16, F32)
    hi = lax.bitcast_convert_type(packed & jnp.uint32(0xFFFF0000), F32)
    return jnp.concatenate([lo, hi], axis=1)


def _rope(x, c, s_a, s_b, half):
    n = x.shape[-1]
    return x * c + pltpu.roll(x, n - half, 1) * s_a + pltpu.roll(x, half, 1) * s_b


def _ada_kernel(c_ref, w_ref, b_ref, o_ref):
    c = c_ref[...]
    sc = c * jax.nn.sigmoid(c)
    c_hi, c_lo = _split2(sc)
    w = w_ref[0]
    w_hi, w_lo = _split2(w)
    acc = _dot(c_hi, w_hi) + _dot(c_hi, w_lo) + _dot(c_lo, w_hi)
    o_ref[0] = acc + b_ref[0]


def _ada(cond, ada_w, ada_b):
    depth, d, n = ada_w.shape
    rows = cond.shape[0]
    tn = 1536
    return pl.pallas_call(
        _ada_kernel,
        out_shape=jax.ShapeDtypeStruct((depth, rows, n), F32),
        grid=(depth, n // tn),
        in_specs=[
            pl.BlockSpec((rows, d), lambda l, j: (0, 0)),
            pl.BlockSpec((1, d, tn), lambda l, j: (l, 0, j)),
            pl.BlockSpec((1, 1, tn), lambda l, j: (l, 0, j)),
        ],
        out_specs=pl.BlockSpec((1, rows, tn), lambda l, j: (l, 0, j)),
        compiler_params=_cparams("arbitrary", "arbitrary"),
        name="ada_mod",
    )(cond, ada_w, ada_b.reshape(depth, 1, n))


def _proj_ab_kernel(*refs, use_rope, tm, sub):
    if use_rope:
        (x_ref, sc_ref, sh_ref, g_ref, w_ref, sgug_ref, sguw_ref, sgub_ref, qg_ref, kg_ref,
         c_ref, sa_ref, sb_ref, oa_ref, q_ref, k_ref, v_ref) = refs
    else:
        (x_ref, sc_ref, sh_ref, g_ref, w_ref, sgug_ref, sguw_ref, sgub_ref, qg_ref, kg_ref,
         oa_ref, q_ref, k_ref, v_ref) = refs
    ts = tm // sub
    n_chunks = ts // CHUNK
    qg = qg_ref[...] * (HEAD_DIM ** -0.5)
    for b in range(sub):
        rows = slice(b * ts, (b + 1) * ts)
        x = x_ref[rows, :]
        h = _rms(x, g_ref[...]) * (1.0 + sc_ref[0]) + sh_ref[0]
        p = _dot(h.astype(BF16), w_ref[...])
        rope = (lambda t: _rope(t, c_ref[rows, :], sa_ref[rows, :], sb_ref[rows, :], HEAD_DIM // 4)) \
            if use_rope else (lambda t: t)

        for g in range(SGU_GROUPS):
            u = jax.nn.gelu(p[:, g * SGU_GROUP_CH:(g + 1) * SGU_GROUP_CH])
            v = jax.nn.gelu(p[:, SGU_WIDTH + g * SGU_GROUP_CH:SGU_WIDTH + (g + 1) * SGU_GROUP_CH])
            v = _rms(v, sgug_ref[:, g * SGU_GROUP_CH:(g + 1) * SGU_GROUP_CH]).astype(BF16)
            vcat = jnp.concatenate([v[c * CHUNK:(c + 1) * CHUNK, :] for c in range(n_chunks)], axis=1)
            mixed = _dot(sguw_ref[g], vcat)
            bias = sgub_ref[g]
            for c in range(n_chunks):
                gate = mixed[:, c * SGU_GROUP_CH:(c + 1) * SGU_GROUP_CH] + bias
                oa_ref[b * ts + c * CHUNK:b * ts + (c + 1) * CHUNK, g * SGU_GROUP_CH:(g + 1) * SGU_GROUP_CH] = (
                    u[c * CHUNK:(c + 1) * CHUNK, :] * gate).astype(oa_ref.dtype)

        for hd in range(GQA_HEADS):
            qh = rope(_rms(p[:, A_END + hd * HEAD_DIM:A_END + (hd + 1) * HEAD_DIM], qg))
            q_ref[rows, hd * HEAD_DIM:(hd + 1) * HEAD_DIM] = qh.astype(q_ref.dtype)
        for hd in range(GQA_KV_HEADS):
            kh = rope(_rms(p[:, K_OFF + hd * HEAD_DIM:K_OFF + (hd + 1) * HEAD_DIM], kg_ref[...]))
            k_ref[rows, hd * HEAD_DIM:(hd + 1) * HEAD_DIM] = kh.astype(k_ref.dtype)
        v_ref[rows, :] = p[:, V_OFF:].astype(v_ref.dtype)


def _proj_ab(x, mod, mod_rows, group_rows, norm_g, w_in, sgu_g, sgu_w, sgu_bfull, q_g, k_g, rope, tm):
    m, d = x.shape
    per_group = group_rows // tm
    sc_j, sh_j = mod_rows
    use_rope = rope is not None
    in_specs = [
        pl.BlockSpec((tm, d), lambda i: (i, 0)),
        pl.BlockSpec((1, 1, d), lambda i: ((i // per_group) * 6 + sc_j, 0, 0)),
        pl.BlockSpec((1, 1, d), lambda i: ((i // per_group) * 6 + sh_j, 0, 0)),
        pl.BlockSpec((1, d), lambda i: (0, 0)),
        pl.BlockSpec(w_in.shape, lambda i: (0, 0)),
        pl.BlockSpec((1, SGU_WIDTH), lambda i: (0, 0)),
        pl.BlockSpec(sgu_w.shape, lambda i: (0, 0, 0)),
        pl.BlockSpec(sgu_bfull.shape, lambda i: (0, 0, 0)),
        pl.BlockSpec((1, HEAD_DIM), lambda i: (0, 0)),
        pl.BlockSpec((1, HEAD_DIM), lambda i: (0, 0)),
    ]
    args = [x, mod, mod, norm_g, w_in, sgu_g, sgu_w, sgu_bfull, q_g, k_g]
    if use_rope:
        s_tiles = rope[0].shape[0] // tm
        for t in rope:
            in_specs.append(pl.BlockSpec((tm, HEAD_DIM), lambda i: (i % s_tiles, 0)))
            args.append(t)
    return pl.pallas_call(
        functools.partial(_proj_ab_kernel, use_rope=use_rope, tm=tm, sub=_row_sub(tm)),
        out_shape=(jax.ShapeDtypeStruct((m, SGU_WIDTH), BF16),
                   jax.ShapeDtypeStruct((m, GQA_Q_W), BF16),
                   jax.ShapeDtypeStruct((m, GQA_KV_W), BF16),
                   jax.ShapeDtypeStruct((m, GQA_KV_W), BF16)),
        grid=(m // tm,),
        in_specs=in_specs,
        out_specs=(pl.BlockSpec((tm, SGU_WIDTH), lambda i: (i, 0)),
                   pl.BlockSpec((tm, GQA_Q_W), lambda i: (i, 0)),
                   pl.BlockSpec((tm, GQA_KV_W), lambda i: (i, 0)),
                   pl.BlockSpec((tm, GQA_KV_W), lambda i: (i, 0))),
        compiler_params=_cparams("parallel"),
        name="proj_ab",
    )(*args)


def _proj_mla_kernel(*refs, with_q, use_rope, sub):
    it = iter(refs)
    x_ref, sc_ref, sh_ref, g_ref, w_ref = (next(it) for _ in range(5))
    if with_q:
        qg_ref, wuq_ref = next(it), next(it)
    kvg_ref, wuk_ref, wuv_ref = next(it), next(it), next(it)
    if use_rope:
        c_ref, sa_ref, sb_ref = next(it), next(it), next(it)
    if with_q:
        q_ref = next(it)
    k_ref, v_ref = next(it), next(it)

    ts = x_ref.shape[0] // sub
    off = MLA_Q_LORA if with_q else 0
    for b in range(sub):
        rows = slice(b * ts, (b + 1) * ts)
        rope = (lambda t: _rope(t, c_ref[rows, :], sa_ref[rows, :], sb_ref[rows, :], MLA_ROPE // 4)) \
            if use_rope else (lambda t: t)
        x = x_ref[rows, :]
        h = _rms(x, g_ref[...]) * (1.0 + sc_ref[0]) + sh_ref[0]
        p = _dot(h.astype(BF16), w_ref[...])
        ckv = _rms(p[:, off:off + MLA_KV_LORA], kvg_ref[...]).astype(BF16)
        kr = rope(p[:, off + MLA_KV_LORA:off + MLA_KV_LORA + LANES])
        kr = kr.astype(k_ref.dtype)
        k_nope = _dot(ckv, wuk_ref[...])
        v_ref[rows, :] = _dot(ckv, wuv_ref[...]).astype(v_ref.dtype)
        for hd in range(MLA_HEADS):
            k_ref[rows, hd * MLA_QK_PAD:hd * MLA_QK_PAD + MLA_NOPE] = (
                k_nope[:, hd * MLA_NOPE:(hd + 1) * MLA_NOPE].astype(k_ref.dtype))
            k_ref[rows, hd * MLA_QK_PAD + MLA_NOPE:(hd + 1) * MLA_QK_PAD] = kr
        if with_q:
            cq = _rms(p[:, :MLA_Q_LORA], qg_ref[...]).astype(BF16)
            q = _dot(cq, wuq_ref[...]) * (MLA_QK ** -0.5)
            for hd in range(MLA_HEADS):
                q_ref[rows, hd * MLA_QK_PAD:hd * MLA_QK_PAD + MLA_NOPE] = (
                    q[:, hd * MLA_QK_PAD:hd * MLA_QK_PAD + MLA_NOPE].astype(q_ref.dtype))
                qr = rope(q[:, hd * MLA_QK_PAD + MLA_NOPE:(hd + 1) * MLA_QK_PAD])
                q_ref[rows, hd * MLA_QK_PAD + MLA_NOPE:(hd + 1) * MLA_QK_PAD] = qr.astype(q_ref.dtype)


def _proj_mla(x, mod, mod_rows, group_rows, norm_g, w_in, q_g, w_uq, kv_g, w_uk, w_uv, rope, with_q, tm):
    m, d = x.shape
    per_group = group_rows // tm
    sc_j, sh_j = mod_rows
    use_rope = rope is not None
    const2 = lambda i: (0, 0)
    in_specs = [
        pl.BlockSpec((tm, d), lambda i: (i, 0)),
        pl.BlockSpec((1, 1, d), lambda i: ((i // per_group) * 6 + sc_j, 0, 0)),
        pl.BlockSpec((1, 1, d), lambda i: ((i // per_group) * 6 + sh_j, 0, 0)),
        pl.BlockSpec((1, d), const2),
        pl.BlockSpec(w_in.shape, const2),
    ]
    args = [x, mod, mod, norm_g, w_in]
    if with_q:
        in_specs += [pl.BlockSpec(q_g.shape, const2), pl.BlockSpec(w_uq.shape, const2)]
        args += [q_g, w_uq]
    in_specs += [pl.BlockSpec(kv_g.shape, const2), pl.BlockSpec(w_uk.shape, const2),
                 pl.BlockSpec(w_uv.shape, const2)]
    args += [kv_g, w_uk, w_uv]
    if use_rope:
        s_tiles = rope[0].shape[0] // tm
        for t in rope:
            in_specs.append(pl.BlockSpec((tm, LANES), lambda i: (i % s_tiles, 0)))
            args.append(t)
    kw = MLA_HEADS * MLA_QK_PAD
    vw = MLA_HEADS * MLA_V
    out_shape = [jax.ShapeDtypeStruct((m, kw), BF16), jax.ShapeDtypeStruct((m, vw), BF16)]
    out_specs = [pl.BlockSpec((tm, kw), lambda i: (i, 0)), pl.BlockSpec((tm, vw), lambda i: (i, 0))]
    if with_q:
        out_shape = [jax.ShapeDtypeStruct((m, kw), BF16)] + out_shape
        out_specs = [pl.BlockSpec((tm, kw), lambda i: (i, 0))] + out_specs
    return pl.pallas_call(
        functools.partial(_proj_mla_kernel, with_q=with_q, use_rope=use_rope, sub=1),
        out_shape=tuple(out_shape),
        grid=(m // tm,),
        in_specs=in_specs,
        out_specs=tuple(out_specs),
        compiler_params=_cparams("parallel"),
        name="proj_mla",
    )(*args)


def _attn_kernel(*refs, n_seg, tq, n_q, group, dk, dv):
    q_ref = refs[0]
    kv = refs[1:1 + 2 * n_seg]
    o_ref = refs[1 + 2 * n_seg]

    for c in range(n_q):
        rows = slice(c * tq, (c + 1) * tq)
        q = jnp.concatenate([q_ref[rows, g * dk:(g + 1) * dk] for g in range(group)], axis=0)
        ss = [_dot_nt(q, kv[2 * j][...]) for j in range(n_seg)]
        mx = ss[0].max(axis=-1, keepdims=True)
        for s in ss[1:]:
            mx = jnp.maximum(mx, s.max(axis=-1, keepdims=True))
        acc = None
        den = None
        for j, s in enumerate(ss):
            e = jnp.exp(s - mx)
            d = e.sum(axis=-1, keepdims=True)
            o = _dot(e.astype(BF16), kv[2 * j + 1][...])
            acc = o if acc is None else acc + o
            den = d if den is None else den + d
        o = (acc / den).astype(o_ref.dtype)
        for g in range(group):
            o_ref[rows, g * dv:(g + 1) * dv] = o[g * tq:(g + 1) * tq, :]


def _attention(q, segs, batch, heads, kv_heads, dk, dv, tq):
    t_q = q.shape[0] // batch
    group = heads // kv_heads
    in_specs = [pl.BlockSpec((t_q, group * dk), lambda b, h: (b, h))]
    args = [q]
    for k, v in segs:
        t_k = k.shape[0] // batch
        in_specs.append(pl.BlockSpec((t_k, dk), lambda b, h: (b, h)))
        in_specs.append(pl.BlockSpec((t_k, dv), lambda b, h: (b, h)))
        args += [k, v]
    tq = min(tq, t_q)
    return pl.pallas_call(
        functools.partial(_attn_kernel, n_seg=len(segs), tq=tq, n_q=t_q // tq, group=group, dk=dk, dv=dv),
        out_shape=jax.ShapeDtypeStruct((q.shape[0], heads * dv), BF16),
        grid=(batch, kv_heads),
        in_specs=in_specs,
        out_specs=pl.BlockSpec((t_q, group * dv), lambda b, h: (b, h)),
        compiler_params=_cparams("parallel", "parallel"),
        name="attention",
    )(*args)


def _out_router_kernel(*refs, n_in, tm, sub):
    ins = refs[:2 * n_in]
    (x_ref, g1_ref, sc_ref, sh_ref, ng_ref, rw_ref, rb_ref, cnt0_ref,
     xo_ref, h2_ref, idx_ref, gate_ref, cnt_ref) = refs[2 * n_in:]
    i = pl.program_id(0)

    @pl.when(i == 0)
    def _():
        cnt_ref[...] = cnt0_ref[...]

    ts = tm // sub
    w_hi, w_lo = _split2(rw_ref[...])
    for b in range(sub):
        rows = slice(b * ts, (b + 1) * ts)
        y = None
        for j in range(n_in):
            t = _dot(ins[2 * j][rows, :], ins[2 * j + 1][...])
            y = t if y is None else y + t
        xn = x_ref[rows, :] + g1_ref[0] * y
        xo_ref[rows, :] = xn
        h2 = _rms(xn, ng_ref[...]) * (1.0 + sc_ref[0]) + sh_ref[0]
        _store_token_tiles(h2_ref.at[pl.ds(b * ts * TOKEN_ROWS, ts * TOKEN_ROWS)], h2)

        h_hi, h_lo = _split2(h2)
        logits = _dot_nt(w_hi, h_hi) + _dot_nt(w_hi, h_lo) + _dot_nt(w_lo, h_hi) + rb_ref[...]

        e_iota = lax.broadcasted_iota(jnp.int32, logits.shape, 0)
        work = logits
        tops, idxs = [], []
        for _ in range(TOP_K):
            mx = work.max(axis=0, keepdims=True)
            ix = jnp.where(work == mx, e_iota, N_EXPERTS).min(axis=0, keepdims=True)
            tops.append(mx)
            idxs.append(ix)
            work = jnp.where(e_iota == ix, -jnp.inf, work)
        exps = [jnp.exp(t - tops[0]) for t in tops]
        den = exps[0] + exps[1] + exps[2] + exps[3]

        for k in range(TOP_K):
            idx_ref[k:k + 1, rows] = idxs[k]
            gate_ref[k:k + 1, rows] = exps[k] / den
        cnt = sum((e_iota == ix).astype(F32) for ix in idxs)
        cnt_ref[...] = cnt_ref[...] + cnt.sum(axis=1, keepdims=True)


def _out_router(ins, x, mod, mod_rows, group_rows, norm_g, router_wt, router_b, cnt0, tm):
    m, d = x.shape
    per_group = group_rows // tm
    g1_j, sc_j, sh_j = mod_rows
    in_specs, args = [], []
    for a, w in ins:
        in_specs += [pl.BlockSpec((tm, a.shape[1]), lambda i: (i, 0)), pl.BlockSpec(w.shape, lambda i: (0, 0))]
        args += [a, w]
    mod_spec = lambda j: pl.BlockSpec((1, 1, d), lambda i: ((i // per_group) * 6 + j, 0, 0))
    in_specs += [
        pl.BlockSpec((tm, d), lambda i: (i, 0)),
        mod_spec(g1_j), mod_spec(sc_j), mod_spec(sh_j),
        pl.BlockSpec((1, d), lambda i: (0, 0)),
        pl.BlockSpec(router_wt.shape, lambda i: (0, 0)),
        pl.BlockSpec(router_b.shape, lambda i: (0, 0)),
        pl.BlockSpec(cnt0.shape, lambda i: (0, 0)),
    ]
    args += [x, mod, mod, mod, norm_g, router_wt, router_b, cnt0]
    small = lambda dt: jax.ShapeDtypeStruct((TOP_K, m), dt)
    small_spec = pl.BlockSpec((TOP_K, tm), lambda i: (0, i))
    return pl.pallas_call(
        functools.partial(_out_router_kernel, n_in=len(ins), tm=tm, sub=1),
        out_shape=(jax.ShapeDtypeStruct((m, d), F32), jax.ShapeDtypeStruct((m * TOKEN_ROWS, LANES), U32),
                   small(jnp.int32), small(F32),
                   jax.ShapeDtypeStruct(cnt0.shape, F32)),
        grid=(m // tm,),
        in_specs=in_specs,
        out_specs=(pl.BlockSpec((tm, d), lambda i: (i, 0)),
                   pl.BlockSpec((tm * TOKEN_ROWS, LANES), lambda i: (i, 0)),
                   small_spec, small_spec,
                   pl.BlockSpec(cnt0.shape, lambda i: (0, 0))),
        compiler_params=_cparams("arbitrary"),
        name="out_router",
    )(*args)


def _moe_kernel(blk_e_ref, meta_ref, tok0_ref, tokn_ref, dstp_ref, dstc_ref, h_hbm, w1_ref, b1_ref, w2_ref, b2_ref,
                o_hbm, x0, x1, y0, y1, w1b, w2b, gsem, ssem, *, tm):
    i = pl.program_id(0)
    n_used = meta_ref[0]

    @pl.when(jnp.logical_and(i < n_used, meta_ref[1 + i] == 1))
    def _():
        w1b[...] = w1_ref[0].astype(BF16)
        w2b[...] = w2_ref[0].astype(BF16)
    xs, ys = (x0, x1), (y0, y1)

    def tile_at(ref, row):
        return ref.at[pl.ds(pl.multiple_of(row, TOKEN_ROWS), TOKEN_ROWS)]

    def for_tokens(fn):
        def body(b, carry):
            for j in range(DMA_UNROLL):
                fn(b * DMA_UNROLL + j, j % 2)
            return carry
        lax.fori_loop(0, tm // DMA_UNROLL, body, 0)

    def gather(tok_ref, xbuf, sem):
        for_tokens(lambda r, prio: pltpu.make_async_copy(
            tile_at(h_hbm, tok_ref[0, 0, r]), tile_at(xbuf, r * TOKEN_ROWS), sem).start(priority=prio))

    def scatter(dst_ref, ybuf, sem):
        for_tokens(lambda r, prio: pltpu.make_async_copy(
            tile_at(ybuf, r * TOKEN_ROWS), tile_at(o_hbm, dst_ref[0, 0, r]), sem).start(priority=prio))

    def wait_gather(xbuf, sem):
        pltpu.make_async_copy(h_hbm.at[pl.ds(0, tm * TOKEN_ROWS)], xbuf, sem).wait()

    def wait_scatter(ybuf, sem):
        pltpu.make_async_copy(ybuf, o_hbm.at[pl.ds(0, tm * TOKEN_ROWS)], sem).wait()

    @pl.when(i == 0)
    def _():
        gather(tok0_ref, x0, gsem.at[0])
        y1[...] = jnp.zeros_like(y1)

    def step(p):
        xb, xn, yb, yp = xs[p], xs[1 - p], ys[p], ys[1 - p]
        wait_gather(xb, gsem.at[p])

        @pl.when(i >= 1)
        def _():
            wait_scatter(yb, ssem.at[p])

        @pl.when(n_used > 0)
        def _():
            scatter(dstp_ref, yp, ssem.at[1 - p])
            gather(tokn_ref, xn, gsem.at[1 - p])

        x = _load_token_tiles(xb).astype(BF16)
        hid = _dot(x, w1b[...]) + b1_ref[0]
        f = hid.shape[1] // 2
        g = jnp.minimum(hid[:, :f], SWIGLU_LIMIT)
        lin = jnp.clip(hid[:, f:], -SWIGLU_LIMIT, SWIGLU_LIMIT)
        act = g * jax.nn.sigmoid(SWIGLU_ALPHA * g) * (lin + 1.0)
        _store_token_tiles(yb, _dot(act.astype(BF16), w2b[...]) + b2_ref[0])

        @pl.when(i == n_used - 1)
        def _():
            wait_scatter(yp, ssem.at[1 - p])
            scatter(dstc_ref, yb, ssem.at[p])
            wait_scatter(yb, ssem.at[p])
            wait_gather(xn, gsem.at[1 - p])

    for p in range(2):
        @pl.when(jnp.logical_and(i < n_used, i % 2 == p))
        def _(p=p):
            step(p)


def _moe(h2, row_tok, row_dst, blk_e, n_used, w1, b1, w2, b2, out_rows, tm):
    n_tiles = row_tok.shape[0]
    d = w1.shape[1]
    f2 = w1.shape[2]
    tile_rows = (tm * TOKEN_ROWS, LANES)
    smem = lambda fn: pl.BlockSpec((1, 1, tm), fn, memory_space=pltpu.SMEM)
    grid_spec = pltpu.PrefetchScalarGridSpec(
        num_scalar_prefetch=2,
        grid=(n_tiles,),
        in_specs=[
            smem(lambda i, be, nu: (0, 0, 0)),
            smem(lambda i, be, nu: (jnp.minimum(i + 1, nu[0] - 1), 0, 0)),
            smem(lambda i, be, nu: (i, 0, 0)),
            smem(lambda i, be, nu: (i + 1, 0, 0)),
            pl.BlockSpec(memory_space=pl.ANY),
            pl.BlockSpec((1, d, f2), lambda i, be, nu: (be[i], 0, 0)),
            pl.BlockSpec((1, 1, f2), lambda i, be, nu: (be[i], 0, 0)),
            pl.BlockSpec((1, f2 // 2, d), lambda i, be, nu: (be[i], 0, 0)),
            pl.BlockSpec((1, 1, d), lambda i, be, nu: (be[i], 0, 0)),
        ],
        out_specs=pl.BlockSpec(memory_space=pl.ANY),
        scratch_shapes=[pltpu.VMEM(tile_rows, U32), pltpu.VMEM(tile_rows, U32),
                        pltpu.VMEM(tile_rows, U32), pltpu.VMEM(tile_rows, U32),
                        pltpu.VMEM((d, f2), BF16), pltpu.VMEM((f2 // 2, d), BF16),
                        pltpu.SemaphoreType.DMA((2,)), pltpu.SemaphoreType.DMA((2,))],
    )
    return pl.pallas_call(
        functools.partial(_moe_kernel, tm=tm),
        out_shape=jax.ShapeDtypeStruct((out_rows * TOKEN_ROWS, LANES), U32),
        grid_spec=grid_spec,
        compiler_params=_cparams("arbitrary"),
        name="moe_experts",
    )(blk_e, n_used, row_tok, row_tok, row_dst, row_dst, h2, w1, b1, w2, b2)


def _combine_kernel(*refs, tq, final):
    ys = refs[:TOP_K]
    if final:
        gate_ref, x_ref, g2_ref, fg_ref, o_ref = refs[TOP_K:]
    else:
        gate_ref, x_ref, g2_ref, o_ref = refs[TOP_K:]
    f = None
    d = x_ref.shape[1]
    for k in range(TOP_K):
        gcol = jnp.broadcast_to(gate_ref[k:k + 1, :], (LANES, tq)).T
        t = _load_token_tiles(ys[k]) * jnp.tile(gcol, (1, d // LANES))
        f = t if f is None else f + t
    xn = x_ref[...] + g2_ref[0] * f
    if final:
        xn = _rms(xn, fg_ref[...])
    o_ref[...] = xn


def _combine(ys, t_tot, t_off, gate, x, mod, g2_j, group_rows, final_g, tq):
    m, d = x.shape
    per_group = group_rows // tq
    final = final_g is not None
    in_specs = [pl.BlockSpec((tq * TOKEN_ROWS, LANES),
                             functools.partial(lambda i, k: ((k * t_tot + t_off) // tq + i, 0), k=k))
                for k in range(TOP_K)]
    in_specs += [
        pl.BlockSpec((TOP_K, tq), lambda i: (0, i)),
        pl.BlockSpec((tq, d), lambda i: (i, 0)),
        pl.BlockSpec((1, 1, d), lambda i: ((i // per_group) * 6 + g2_j, 0, 0)),
    ]
    args = [ys] * TOP_K + [gate, x, mod]
    if final:
        in_specs.append(pl.BlockSpec((1, d), lambda i: (0, 0)))
        args.append(final_g)
    return pl.pallas_call(
        functools.partial(_combine_kernel, tq=tq, final=final),
        out_shape=jax.ShapeDtypeStruct((m, d), F32),
        grid=(m // tq,),
        in_specs=in_specs,
        out_specs=pl.BlockSpec((tq, d), lambda i: (i, 0)),
        compiler_params=_cparams("parallel"),
        name="moe_combine",
    )(*args)


def _rope_tables(seq, rot_dim):
    t = np.arange(seq)
    row = (t // GRID_W).astype(np.float32)
    col = (t % GRID_W).astype(np.float32)
    axis_dim = rot_dim // 2
    quarter = axis_dim // 2
    inv = jnp.asarray(ROPE_THETA, F32) ** (-jnp.arange(0, axis_dim, 2, dtype=F32) / axis_dim)
    ang_r = jnp.asarray(row)[:, None] * inv
    ang_c = jnp.asarray(col)[:, None] * inv
    zeros = jnp.zeros_like(ang_r)
    cos = jnp.concatenate([jnp.cos(ang_r)] * 2 + [jnp.cos(ang_c)] * 2, axis=1)
    s_a = jnp.concatenate([-jnp.sin(ang_r), zeros, -jnp.sin(ang_c), zeros], axis=1)
    s_b = jnp.concatenate([zeros, jnp.sin(ang_r), zeros, jnp.sin(ang_c)], axis=1)
    pad = LANES - rot_dim
    if pad:
        cos, s_a, s_b = (jnp.pad(a, ((0, 0), (0, pad))) for a in (cos, s_a, s_b))
    del quarter
    return cos, s_a, s_b


def _route(idx, counts, tm):
    t_tot = idx.shape[1]
    n_assign = t_tot * TOP_K
    n_rows = n_assign + N_EXPERTS * tm
    n_tiles = n_rows // tm
    counts = counts.astype(jnp.int32)
    pad_need = (-counts) % tm
    flat_e = idx.T.reshape(-1)
    slot = jnp.arange(tm, dtype=jnp.int32)[None, :]
    pad_keys = jnp.where(slot < pad_need[:, None], jnp.arange(N_EXPERTS, dtype=jnp.int32)[:, None], N_EXPERTS)
    keys = jnp.concatenate([flat_e, pad_keys.reshape(-1)])
    pos_bits = (n_rows - 1).bit_length()
    packed = lax.sort((keys << pos_bits) | jnp.arange(n_rows, dtype=jnp.int32))
    skeys = packed >> pos_bits
    svals = packed & ((1 << pos_bits) - 1)
    real = svals < n_assign
    row_tok = jnp.where(real, svals // TOP_K, 0)
    rows = jnp.arange(n_rows, dtype=jnp.int32)
    row_dst = jnp.where(real, (svals % TOP_K) * t_tot + svals // TOP_K, n_assign + rows % tm)
    tile_e = skeys[::tm]
    n_used = jnp.sum(tile_e < N_EXPERTS).astype(jnp.int32)
    last_e = tile_e[jnp.maximum(n_used - 1, 0)]
    blk_e = jnp.where(tile_e < N_EXPERTS, tile_e, last_e).astype(jnp.int32)
    row_dst = jnp.concatenate([n_assign + jnp.arange(tm, dtype=jnp.int32), row_dst])
    row_tok = (row_tok * TOKEN_ROWS).reshape(n_tiles, 1, tm)
    row_dst = (row_dst * TOKEN_ROWS).reshape(n_tiles + 1, 1, tm)
    first = jnp.concatenate([jnp.ones((1,), jnp.int32), (blk_e[1:] != blk_e[:-1]).astype(jnp.int32)])
    meta = jnp.concatenate([n_used.reshape(1), first])
    return row_tok, row_dst, blk_e, meta


def _pick(m, pref):
    t = pref
    while m % t:
        t //= 2
    return t


def kernel(x, c, ctx, c_ctx, ada_w, ada_b, norm_mix_g, norm_ffn_g, ab_w_in, sgu_norm_g, sgu_w, sgu_b,
           gqa_q_norm_g, gqa_k_norm_g, ab_w_out, mla_w_in, mla_q_norm_g, mla_kv_norm_g, mla_w_uq, mla_w_ukv,
           mla_w_out, router_w, router_b, moe_w1, moe_b1, moe_w2, moe_b2, final_norm_g):
    bn, s, d = x.shape
    l = ctx.shape[1]
    depth = ada_w.shape[0]
    m_lat, m_ctx = bn * s, bn * l
    tm_lat = _pick(s, 1024)
    tm_ctx = _pick(m_ctx, 1024)
    tm_moe = 512
    tq_comb = _pick(int(np.gcd(s, m_ctx)), 512)

    n_cond = (bn + 1 + 7) // 8 * 8
    cond = jnp.concatenate([c, c_ctx[None, :], jnp.zeros((n_cond - bn - 1, d), F32)], axis=0)
    mod_all = _ada(cond, ada_w, ada_b)

    rope_gqa = _rope_tables(s, HEAD_DIM)
    rope_mla = _rope_tables(s, MLA_ROPE)
    w1_all = moe_w1.reshape(depth * N_EXPERTS, d, -1)
    w2_all = moe_w2.reshape(depth * N_EXPERTS, -1, d)
    b1_all = moe_b1.reshape(depth * N_EXPERTS, 1, -1)
    b2_all = moe_b2.reshape(depth * N_EXPERTS, 1, d)

    xl = x.reshape(m_lat, d)
    xc = ctx.reshape(m_ctx, d)
    for i in range(depth):
        ctx_out = i < depth - 1
        j = i // 2
        mod_lat = mod_all[i, :bn].reshape(bn * 6, 1, d)
        mod_ctx = mod_all[i, bn].reshape(6, 1, d)
        ng = norm_mix_g[i].reshape(1, d)
        if i % 2 == 0:
            w_in = ab_w_in[j].astype(BF16)
            w_out = ab_w_out[j].astype(BF16)
            sgu_g = sgu_norm_g[j].reshape(1, SGU_WIDTH)
            sgu_wb = sgu_w[j].astype(BF16)
            sgu_bfull = jnp.broadcast_to(sgu_b[j][:, :, None], (SGU_GROUPS, CHUNK, SGU_GROUP_CH))
            q_g = gqa_q_norm_g[j].reshape(1, HEAD_DIM)
            k_g = gqa_k_norm_g[j].reshape(1, HEAD_DIM)
            oa, q, k, v = _proj_ab(xl, mod_lat, (1, 0), s, ng, w_in, sgu_g, sgu_wb, sgu_bfull, q_g, k_g,
                                   rope_gqa, tm_lat)
            oac, qc, kc, vc = _proj_ab(xc, mod_ctx, (1, 0), m_ctx, ng, w_in, sgu_g, sgu_wb, sgu_bfull, q_g, k_g,
                                       None, tm_ctx)
            ob = _attention(q, [(kc, vc), (k, v)], bn, GQA_HEADS, GQA_KV_HEADS, HEAD_DIM, HEAD_DIM, ATTN_TQ)
            ins = [(oa, w_out[:SGU_WIDTH]), (ob, w_out[SGU_WIDTH:])]
            if ctx_out:
                obc = _attention(qc, [(kc, vc)], bn, GQA_HEADS, GQA_KV_HEADS, HEAD_DIM, HEAD_DIM, ATTN_TQ)
                ins_c = [(oac, w_out[:SGU_WIDTH]), (obc, w_out[SGU_WIDTH:])]
        else:
            w_in_f = mla_w_in[j]
            w_in = jnp.pad(w_in_f, ((0, 0), (0, MLA_IN_PAD - w_in_f.shape[1]))).astype(BF16)
            w_in_kv = w_in[:, MLA_Q_LORA:]
            w_out = mla_w_out[j].astype(BF16)
            q_g = mla_q_norm_g[j].reshape(1, MLA_Q_LORA)
            kv_g = mla_kv_norm_g[j].reshape(1, MLA_KV_LORA)
            w_uq = mla_w_uq[j].reshape(MLA_Q_LORA, MLA_HEADS, MLA_QK)
            w_uq = jnp.pad(w_uq, ((0, 0), (0, 0), (0, MLA_QK_PAD - MLA_QK)))
            w_uq = w_uq.reshape(MLA_Q_LORA, MLA_HEADS * MLA_QK_PAD).astype(BF16)
            w_ukv = mla_w_ukv[j].reshape(MLA_KV_LORA, MLA_HEADS, MLA_NOPE + MLA_V)
            w_uk = w_ukv[:, :, :MLA_NOPE].reshape(MLA_KV_LORA, MLA_HEADS * MLA_NOPE).astype(BF16)
            w_uv = w_ukv[:, :, MLA_NOPE:].reshape(MLA_KV_LORA, MLA_HEADS * MLA_V).astype(BF16)
            q, k, v = _proj_mla(xl, mod_lat, (1, 0), s, ng, w_in, q_g, w_uq, kv_g, w_uk, w_uv, rope_mla, True,
                                tm_lat)
            if ctx_out:
                qc, kc, vc = _proj_mla(xc, mod_ctx, (1, 0), m_ctx, ng, w_in, q_g, w_uq, kv_g, w_uk, w_uv, None,
                                       True, tm_ctx)
            else:
                kc, vc = _proj_mla(xc, mod_ctx, (1, 0), m_ctx, ng, w_in_kv, None, None, kv_g, w_uk, w_uv, None,
                                   False, tm_ctx)
            ob = _attention(q, [(kc, vc), (k, v)], bn, MLA_HEADS, MLA_HEADS, MLA_QK_PAD, MLA_V, ATTN_TQ)
            ins = [(ob, w_out)]
            if ctx_out:
                obc = _attention(qc, [(kc, vc)], bn, MLA_HEADS, MLA_HEADS, MLA_QK_PAD, MLA_V, ATTN_TQ)
                ins_c = [(obc, w_out)]

        nfg = norm_ffn_g[i].reshape(1, d)
        rwt = router_w[i].T
        rb = router_b[i].reshape(N_EXPERTS, 1)
        cnt0 = jnp.zeros((N_EXPERTS, LANES), F32)
        xl, h2, idx, gate, cnt = _out_router(ins, xl, mod_lat, (2, 4, 3), s, nfg, rwt, rb, cnt0, tm_lat)
        if ctx_out:
            xc, h2c, idx_c, gate_c, cnt = _out_router(ins_c, xc, mod_ctx, (2, 4, 3), m_ctx, nfg, rwt, rb, cnt, tm_ctx)
            h2 = jnp.concatenate([h2, h2c], axis=0)
            idx = jnp.concatenate([idx, idx_c], axis=1)
        t_tot = h2.shape[0] // TOKEN_ROWS
        n_assign = t_tot * TOP_K
        row_tok, row_dst, blk_e, n_used = _route(idx, cnt[:, 0], tm_moe)
        ys = _moe(h2, row_tok, row_dst, blk_e + i * N_EXPERTS, n_used, w1_all, b1_all, w2_all, b2_all,
                  n_assign + tm_moe, tm_moe)
        last = i == depth - 1
        fg = final_norm_g.reshape(1, d) if last else None
        xl = _combine(ys, t_tot, 0, gate, xl, mod_lat, 5, s, fg, tq_comb)
        if ctx_out:
            xc = _combine(ys, t_tot, m_lat, gate_c, xc, mod_ctx, 5, m_ctx, None, tq_comb)
    return xl.reshape(bn, s, d)
```

```python
import functools

import jax
import jax.numpy as jnp
import numpy as np
from jax import lax
from jax.experimental import pallas as pl
from jax.experimental.pallas import tpu as pltpu

F32 = jnp.float32
BF16 = jnp.bfloat16
U32 = jnp.uint32

GRID_W = 64
EPS = 1e-6
ROPE_THETA = 10000.0
CHUNK = 128
SGU_GROUPS = 4
SGU_GROUP_CH = 128
SGU_WIDTH = SGU_GROUPS * SGU_GROUP_CH
GQA_HEADS = 4
GQA_KV_HEADS = 2
HEAD_DIM = 128
GQA_Q_W = GQA_HEADS * HEAD_DIM
GQA_KV_W = GQA_KV_HEADS * HEAD_DIM
A_END = 2 * SGU_WIDTH
K_OFF = A_END + GQA_Q_W
V_OFF = K_OFF + GQA_KV_W
MLA_HEADS = 8
MLA_Q_LORA = 512
MLA_KV_LORA = 256
MLA_NOPE = 128
MLA_ROPE = 64
MLA_V = 128
MLA_QK = MLA_NOPE + MLA_ROPE
MLA_QK_PAD = 256
MLA_IN_PAD = 896
N_EXPERTS = 32
TOP_K = 4
D_EXPERT = 1024
SWIGLU_LIMIT = 7.0
SWIGLU_ALPHA = 1.702

LANES = 128
TOKEN_ROWS = 4
DMA_UNROLL = 64
ROW_SUB = 4
ATTN_TQ = 256
PROJ_TM = 1024
MOE_TM = 512
ADA_TN = 1536
VMEM_LIMIT = 56 * 1024 * 1024


def _cparams(*sem):
    return pltpu.CompilerParams(dimension_semantics=sem, vmem_limit_bytes=VMEM_LIMIT)


def _dot(a, b):
    return jnp.dot(a, b, preferred_element_type=F32)


def _dot_nt(a, b):
    return lax.dot_general(a, b, (((1,), (1,)), ((), ())), preferred_element_type=F32)


def _split2(a):
    hi = a.astype(BF16)
    lo = (a - hi.astype(F32)).astype(BF16)
    return hi, lo


def _row_sub(tm):
    return ROW_SUB if tm % (ROW_SUB * CHUNK) == 0 else 1


def _rms(x, g):
    return x * lax.rsqrt(jnp.mean(x * x, axis=-1, keepdims=True) + EPS) * g


def _store_token_tiles(ref, val):
    n, d = val.shape
    h = d // 2
    lo = lax.bitcast_convert_type(val[:, :h].astype(BF16).astype(F32), U32) >> 16
    hi = lax.bitcast_convert_type(val[:, h:].astype(BF16).astype(F32), U32) & jnp.uint32(0xFFFF0000)
    packed = hi | lo
    for s in range(TOKEN_ROWS):
        ref[pl.ds(s, n, stride=TOKEN_ROWS), :] = packed[:, s * LANES:(s + 1) * LANES]


def _load_token_tiles(ref):
    n = ref.shape[0] // TOKEN_ROWS
    packed = jnp.concatenate([ref[pl.ds(s, n, stride=TOKEN_ROWS), :] for s in range(TOKEN_ROWS)], axis=1)
    lo = lax.bitcast_convert_type(packed << 16, F32)
    hi = lax.bitcast_convert_type(packed & jnp.uint32(0xFFFF0000), F32)
    return jnp.concatenate([lo, hi], axis=1)


def _rope(x, c, s_a, s_b, half):
    n = x.shape[-1]
    return x * c + pltpu.roll(x, n - half, 1) * s_a + pltpu.roll(x, half, 1) * s_b


def _ada_kernel(c_ref, w_ref, b_ref, o_ref):
    c = c_ref[...]
    sc = c * jax.nn.sigmoid(c)
    c_hi, c_lo = _split2(sc)
    w = w_ref[0]
    w_hi, w_lo = _split2(w)
    acc = _dot(c_hi, w_hi) + _dot(c_hi, w_lo) + _dot(c_lo, w_hi)
    o_ref[0] = acc + b_ref[0]


def _ada(cond, ada_w, ada_b):
    depth, d, n = ada_w.shape
    rows = cond.shape[0]
    tn = _pick(n, ADA_TN)
    return pl.pallas_call(
        _ada_kernel,
        out_shape=jax.ShapeDtypeStruct((depth, rows, n), F32),
        grid=(depth, n // tn),
        in_specs=[
            pl.BlockSpec((rows, d), lambda l, j: (0, 0)),
            pl.BlockSpec((1, d, tn), lambda l, j: (l, 0, j)),
            pl.BlockSpec((1, 1, tn), lambda l, j: (l, 0, j)),
        ],
        out_specs=pl.BlockSpec((1, rows, tn), lambda l, j: (l, 0, j)),
        compiler_params=_cparams("arbitrary", "arbitrary"),
        name="ada_mod",
    )(cond, ada_w, ada_b.reshape(depth, 1, n))


def _proj_ab_kernel(*refs, use_rope, tm, sub):
    if use_rope:
        (x_ref, sc_ref, sh_ref, g_ref, w_ref, sgug_ref, sguw_ref, sgub_ref, qg_ref, kg_ref,
         c_ref, sa_ref, sb_ref, oa_ref, q_ref, k_ref, v_ref) = refs
    else:
        (x_ref, sc_ref, sh_ref, g_ref, w_ref, sgug_ref, sguw_ref, sgub_ref, qg_ref, kg_ref,
         oa_ref, q_ref, k_ref, v_ref) = refs
    ts = tm // sub
    n_chunks = ts // CHUNK
    qg = qg_ref[...] * (HEAD_DIM ** -0.5)
    for b in range(sub):
        rows = slice(b * ts, (b + 1) * ts)
        x = x_ref[rows, :]
        h = _rms(x, g_ref[...]) * (1.0 + sc_ref[0]) + sh_ref[0]
        p = _dot(h.astype(BF16), w_ref[...])
        rope = (lambda t: _rope(t, c_ref[rows, :], sa_ref[rows, :], sb_ref[rows, :], HEAD_DIM // 4)) \
            if use_rope else (lambda t: t)

        for g in range(SGU_GROUPS):
            u = jax.nn.gelu(p[:, g * SGU_GROUP_CH:(g + 1) * SGU_GROUP_CH])
            v = jax.nn.gelu(p[:, SGU_WIDTH + g * SGU_GROUP_CH:SGU_WIDTH + (g + 1) * SGU_GROUP_CH])
            v = _rms(v, sgug_ref[:, g * SGU_GROUP_CH:(g + 1) * SGU_GROUP_CH]).astype(BF16)
            vcat = jnp.concatenate([v[c * CHUNK:(c + 1) * CHUNK, :] for c in range(n_chunks)], axis=1)
            mixed = _dot(sguw_ref[g], vcat)
            bias = sgub_ref[g]
            for c in range(n_chunks):
                gate = mixed[:, c * SGU_GROUP_CH:(c + 1) * SGU_GROUP_CH] + bias
                oa_ref[b * ts + c * CHUNK:b * ts + (c + 1) * CHUNK, g * SGU_GROUP_CH:(g + 1) * SGU_GROUP_CH] = (
                    u[c * CHUNK:(c + 1) * CHUNK, :] * gate).astype(oa_ref.dtype)

        for hd in range(GQA_HEADS):
            qh = rope(_rms(p[:, A_END + hd * HEAD_DIM:A_END + (hd + 1) * HEAD_DIM], qg))
            q_ref[rows, hd * HEAD_DIM:(hd + 1) * HEAD_DIM] = qh.astype(q_ref.dtype)
        for hd in range(GQA_KV_HEADS):
            kh = rope(_rms(p[:, K_OFF + hd * HEAD_DIM:K_OFF + (hd + 1) * HEAD_DIM], kg_ref[...]))
            k_ref[rows, hd * HEAD_DIM:(hd + 1) * HEAD_DIM] = kh.astype(k_ref.dtype)
        v_ref[rows, :] = p[:, V_OFF:].astype(v_ref.dtype)


def _proj_ab(x, mod, mod_rows, group_rows, norm_g, w_in, sgu_g, sgu_w, sgu_bfull, q_g, k_g, rope, tm):
    m, d = x.shape
    per_group = group_rows // tm
    sc_j, sh_j = mod_rows
    use_rope = rope is not None
    in_specs = [
        pl.BlockSpec((tm, d), lambda i: (i, 0)),
        pl.BlockSpec((1, 1, d), lambda i: ((i // per_group) * 6 + sc_j, 0, 0)),
        pl.BlockSpec((1, 1, d), lambda i: ((i // per_group) * 6 + sh_j, 0, 0)),
        pl.BlockSpec((1, d), lambda i: (0, 0)),
        pl.BlockSpec(w_in.shape, lambda i: (0, 0)),
        pl.BlockSpec((1, SGU_WIDTH), lambda i: (0, 0)),
        pl.BlockSpec(sgu_w.shape, lambda i: (0, 0, 0)),
        pl.BlockSpec(sgu_bfull.shape, lambda i: (0, 0, 0)),
        pl.BlockSpec((1, HEAD_DIM), lambda i: (0, 0)),
        pl.BlockSpec((1, HEAD_DIM), lambda i: (0, 0)),
    ]
    args = [x, mod, mod, norm_g, w_in, sgu_g, sgu_w, sgu_bfull, q_g, k_g]
    if use_rope:
        s_tiles = rope[0].shape[0] // tm
        for t in rope:
            in_specs.append(pl.BlockSpec((tm, HEAD_DIM), lambda i: (i % s_tiles, 0)))
            args.append(t)
    return pl.pallas_call(
        functools.partial(_proj_ab_kernel, use_rope=use_rope, tm=tm, sub=_row_sub(tm)),
        out_shape=(jax.ShapeDtypeStruct((m, SGU_WIDTH), BF16),
                   jax.ShapeDtypeStruct((m, GQA_Q_W), BF16),
                   jax.ShapeDtypeStruct((m, GQA_KV_W), BF16),
                   jax.ShapeDtypeStruct((m, GQA_KV_W), BF16)),
        grid=(m // tm,),
        in_specs=in_specs,
        out_specs=(pl.BlockSpec((tm, SGU_WIDTH), lambda i: (i, 0)),
                   pl.BlockSpec((tm, GQA_Q_W), lambda i: (i, 0)),
                   pl.BlockSpec((tm, GQA_KV_W), lambda i: (i, 0)),
                   pl.BlockSpec((tm, GQA_KV_W), lambda i: (i, 0))),
        compiler_params=_cparams("parallel"),
        name="proj_ab",
    )(*args)


def _proj_mla_kernel(*refs, with_q, use_rope, sub):
    it = iter(refs)
    x_ref, sc_ref, sh_ref, g_ref, w_ref = (next(it) for _ in range(5))
    if with_q:
        qg_ref, wuq_ref = next(it), next(it)
    kvg_ref, wuk_ref, wuv_ref = next(it), next(it), next(it)
    if use_rope:
        c_ref, sa_ref, sb_ref = next(it), next(it), next(it)
    if with_q:
        q_ref = next(it)
    k_ref, v_ref = next(it), next(it)

    ts = x_ref.shape[0] // sub
    off = MLA_Q_LORA if with_q else 0
    for b in range(sub):
        rows = slice(b * ts, (b + 1) * ts)
        rope = (lambda t: _rope(t, c_ref[rows, :], sa_ref[rows, :], sb_ref[rows, :], MLA_ROPE // 4)) \
            if use_rope else (lambda t: t)
        x = x_ref[rows, :]
        h = _rms(x, g_ref[...]) * (1.0 + sc_ref[0]) + sh_ref[0]
        p = _dot(h.astype(BF16), w_ref[...])
        ckv = _rms(p[:, off:off + MLA_KV_LORA], kvg_ref[...]).astype(BF16)
        kr = rope(p[:, off + MLA_KV_LORA:off + MLA_KV_LORA + LANES])
        kr = kr.astype(k_ref.dtype)
        k_nope = _dot(ckv, wuk_ref[...])
        v_ref[rows, :] = _dot(ckv, wuv_ref[...]).astype(v_ref.dtype)
        for hd in range(MLA_HEADS):
            k_ref[rows, hd * MLA_QK_PAD:hd * MLA_QK_PAD + MLA_NOPE] = (
                k_nope[:, hd * MLA_NOPE:(hd + 1) * MLA_NOPE].astype(k_ref.dtype))
            k_ref[rows, hd * MLA_QK_PAD + MLA_NOPE:(hd + 1) * MLA_QK_PAD] = kr
        if with_q:
            cq = _rms(p[:, :MLA_Q_LORA], qg_ref[...]).astype(BF16)
            q = _dot(cq, wuq_ref[...]) * (MLA_QK ** -0.5)
            for hd in range(MLA_HEADS):
                q_ref[rows, hd * MLA_QK_PAD:hd * MLA_QK_PAD + MLA_NOPE] = (
                    q[:, hd * MLA_QK_PAD:hd * MLA_QK_PAD + MLA_NOPE].astype(q_ref.dtype))
                qr = rope(q[:, hd * MLA_QK_PAD + MLA_NOPE:(hd + 1) * MLA_QK_PAD])
                q_ref[rows, hd * MLA_QK_PAD + MLA_NOPE:(hd + 1) * MLA_QK_PAD] = qr.astype(q_ref.dtype)


def _proj_mla(x, mod, mod_rows, group_rows, norm_g, w_in, q_g, w_uq, kv_g, w_uk, w_uv, rope, with_q, tm):
    m, d = x.shape
    per_group = group_rows // tm
    sc_j, sh_j = mod_rows
    use_rope = rope is not None
    const2 = lambda i: (0, 0)
    in_specs = [
        pl.BlockSpec((tm, d), lambda i: (i, 0)),
        pl.BlockSpec((1, 1, d), lambda i: ((i // per_group) * 6 + sc_j, 0, 0)),
        pl.BlockSpec((1, 1, d), lambda i: ((i // per_group) * 6 + sh_j, 0, 0)),
        pl.BlockSpec((1, d), const2),
        pl.BlockSpec(w_in.shape, const2),
    ]
    args = [x, mod, mod, norm_g, w_in]
    if with_q:
        in_specs += [pl.BlockSpec(q_g.shape, const2), pl.BlockSpec(w_uq.shape, const2)]
        args += [q_g, w_uq]
    in_specs += [pl.BlockSpec(kv_g.shape, const2), pl.BlockSpec(w_uk.shape, const2),
                 pl.BlockSpec(w_uv.shape, const2)]
    args += [kv_g, w_uk, w_uv]
    if use_rope:
        s_tiles = rope[0].shape[0] // tm
        for t in rope:
            in_specs.append(pl.BlockSpec((tm, LANES), lambda i: (i % s_tiles, 0)))
            args.append(t)
    kw = MLA_HEADS * MLA_QK_PAD
    vw = MLA_HEADS * MLA_V
    out_shape = [jax.ShapeDtypeStruct((m, kw), BF16), jax.ShapeDtypeStruct((m, vw), BF16)]
    out_specs = [pl.BlockSpec((tm, kw), lambda i: (i, 0)), pl.BlockSpec((tm, vw), lambda i: (i, 0))]
    if with_q:
        out_shape = [jax.ShapeDtypeStruct((m, kw), BF16)] + out_shape
        out_specs = [pl.BlockSpec((tm, kw), lambda i: (i, 0))] + out_specs
    return pl.pallas_call(
        functools.partial(_proj_mla_kernel, with_q=with_q, use_rope=use_rope, sub=1),
        out_shape=tuple(out_shape),
        grid=(m // tm,),
        in_specs=in_specs,
        out_specs=tuple(out_specs),
        compiler_params=_cparams("parallel"),
        name="proj_mla",
    )(*args)


def _attn_kernel(*refs, n_seg, tq, n_q, group, dk, dv):
    q_ref = refs[0]
    kv = refs[1:1 + 2 * n_seg]
    o_ref = refs[1 + 2 * n_seg]

    for c in range(n_q):
        rows = slice(c * tq, (c + 1) * tq)
        q = jnp.concatenate([q_ref[rows, g * dk:(g + 1) * dk] for g in range(group)], axis=0)
        ss = [_dot_nt(q, kv[2 * j][...]) for j in range(n_seg)]
        mx = ss[0].max(axis=-1, keepdims=True)
        for s in ss[1:]:
            mx = jnp.maximum(mx, s.max(axis=-1, keepdims=True))
        acc = None
        den = None
        for j, s in enumerate(ss):
            e = jnp.exp(s - mx)
            d = e.sum(axis=-1, keepdims=True)
            o = _dot(e.astype(BF16), kv[2 * j + 1][...])
            acc = o if acc is None else acc + o
            den = d if den is None else den + d
        o = (acc / den).astype(o_ref.dtype)
        for g in range(group):
            o_ref[rows, g * dv:(g + 1) * dv] = o[g * tq:(g + 1) * tq, :]


def _attention(q, segs, batch, heads, kv_heads, dk, dv, tq):
    t_q = q.shape[0] // batch
    group = heads // kv_heads
    in_specs = [pl.BlockSpec((t_q, group * dk), lambda b, h: (b, h))]
    args = [q]
    for k, v in segs:
        t_k = k.shape[0] // batch
        in_specs.append(pl.BlockSpec((t_k, dk), lambda b, h: (b, h)))
        in_specs.append(pl.BlockSpec((t_k, dv), lambda b, h: (b, h)))
        args += [k, v]
    tq = min(tq, t_q)
    return pl.pallas_call(
        functools.partial(_attn_kernel, n_seg=len(segs), tq=tq, n_q=t_q // tq, group=group, dk=dk, dv=dv),
        out_shape=jax.ShapeDtypeStruct((q.shape[0], heads * dv), BF16),
        grid=(batch, kv_heads),
        in_specs=in_specs,
        out_specs=pl.BlockSpec((t_q, group * dv), lambda b, h: (b, h)),
        compiler_params=_cparams("parallel", "parallel"),
        name="attention",
    )(*args)


def _out_router_kernel(*refs, n_in, tm, sub):
    ins = refs[:2 * n_in]
    (x_ref, g1_ref, sc_ref, sh_ref, ng_ref, rw_ref, rb_ref, cnt0_ref,
     xo_ref, h2_ref, idx_ref, gate_ref, cnt_ref) = refs[2 * n_in:]
    i = pl.program_id(0)

    @pl.when(i == 0)
    def _():
        cnt_ref[...] = cnt0_ref[...]

    ts = tm // sub
    w_hi, w_lo = _split2(rw_ref[...])
    for b in range(sub):
        rows = slice(b * ts, (b + 1) * ts)
        y = None
        for j in range(n_in):
            t = _dot(ins[2 * j][rows, :], ins[2 * j + 1][...])
            y = t if y is None else y + t
        xn = x_ref[rows, :] + g1_ref[0] * y
        xo_ref[rows, :] = xn
        h2 = _rms(xn, ng_ref[...]) * (1.0 + sc_ref[0]) + sh_ref[0]
        _store_token_tiles(h2_ref.at[pl.ds(b * ts * TOKEN_ROWS, ts * TOKEN_ROWS)], h2)

        h_hi, h_lo = _split2(h2)
        logits = _dot_nt(w_hi, h_hi) + _dot_nt(w_hi, h_lo) + _dot_nt(w_lo, h_hi) + rb_ref[...]

        e_iota = lax.broadcasted_iota(jnp.int32, logits.shape, 0)
        work = logits
        tops, idxs = [], []
        for _ in range(TOP_K):
            mx = work.max(axis=0, keepdims=True)
            ix = jnp.where(work == mx, e_iota, N_EXPERTS).min(axis=0, keepdims=True)
            tops.append(mx)
            idxs.append(ix)
            work = jnp.where(e_iota == ix, -jnp.inf, work)
        exps = [jnp.exp(t - tops[0]) for t in tops]
        den = exps[0] + exps[1] + exps[2] + exps[3]

        for k in range(TOP_K):
            idx_ref[k:k + 1, rows] = idxs[k]
            gate_ref[k:k + 1, rows] = exps[k] / den
        cnt = sum((e_iota == ix).astype(F32) for ix in idxs)
        cnt_ref[...] = cnt_ref[...] + cnt.sum(axis=1, keepdims=True)


def _out_router(ins, x, mod, mod_rows, group_rows, norm_g, router_wt, router_b, cnt0, tm):
    m, d = x.shape
    per_group = group_rows // tm
    g1_j, sc_j, sh_j = mod_rows
    in_specs, args = [], []
    for a, w in ins:
        in_specs += [pl.BlockSpec((tm, a.shape[1]), lambda i: (i, 0)), pl.BlockSpec(w.shape, lambda i: (0, 0))]
        args += [a, w]
    mod_spec = lambda j: pl.BlockSpec((1, 1, d), lambda i: ((i // per_group) * 6 + j, 0, 0))
    in_specs += [
        pl.BlockSpec((tm, d), lambda i: (i, 0)),
        mod_spec(g1_j), mod_spec(sc_j), mod_spec(sh_j),
        pl.BlockSpec((1, d), lambda i: (0, 0)),
        pl.BlockSpec(router_wt.shape, lambda i: (0, 0)),
        pl.BlockSpec(router_b.shape, lambda i: (0, 0)),
        pl.BlockSpec(cnt0.shape, lambda i: (0, 0)),
    ]
    args += [x, mod, mod, mod, norm_g, router_wt, router_b, cnt0]
    small = lambda dt: jax.ShapeDtypeStruct((TOP_K, m), dt)
    small_spec = pl.BlockSpec((TOP_K, tm), lambda i: (0, i))
    return pl.pallas_call(
        functools.partial(_out_router_kernel, n_in=len(ins), tm=tm, sub=1),
        out_shape=(jax.ShapeDtypeStruct((m, d), F32), jax.ShapeDtypeStruct((m * TOKEN_ROWS, LANES), U32),
                   small(jnp.int32), small(F32),
                   jax.ShapeDtypeStruct(cnt0.shape, F32)),
        grid=(m // tm,),
        in_specs=in_specs,
        out_specs=(pl.BlockSpec((tm, d), lambda i: (i, 0)),
                   pl.BlockSpec((tm * TOKEN_ROWS, LANES), lambda i: (i, 0)),
                   small_spec, small_spec,
                   pl.BlockSpec(cnt0.shape, lambda i: (0, 0))),
        compiler_params=_cparams("arbitrary"),
        name="out_router",
    )(*args)


def _moe_kernel(blk_e_ref, meta_ref, tok0_ref, tokn_ref, dstp_ref, dstc_ref, h_hbm, w1_ref, b1_ref, w2_ref, b2_ref,
                o_hbm, x0, x1, y0, y1, w1b, w2b, gsem, ssem, *, tm):
    i = pl.program_id(0)
    n_used = meta_ref[0]

    @pl.when(jnp.logical_and(i < n_used, meta_ref[1 + i] == 1))
    def _():
        w1b[...] = w1_ref[0].astype(BF16)
        w2b[...] = w2_ref[0].astype(BF16)
    xs, ys = (x0, x1), (y0, y1)

    def tile_at(ref, row):
        return ref.at[pl.ds(pl.multiple_of(row, TOKEN_ROWS), TOKEN_ROWS)]

    def for_tokens(fn):
        def body(b, carry):
            for j in range(DMA_UNROLL):
                fn(b * DMA_UNROLL + j, j % 2)
            return carry
        lax.fori_loop(0, tm // DMA_UNROLL, body, 0)

    def gather(tok_ref, xbuf, sem):
        for_tokens(lambda r, prio: pltpu.make_async_copy(
            tile_at(h_hbm, tok_ref[0, 0, r]), tile_at(xbuf, r * TOKEN_ROWS), sem).start(priority=prio))

    def scatter(dst_ref, ybuf, sem):
        for_tokens(lambda r, prio: pltpu.make_async_copy(
            tile_at(ybuf, r * TOKEN_ROWS), tile_at(o_hbm, dst_ref[0, 0, r]), sem).start(priority=prio))

    def wait_gather(xbuf, sem):
        pltpu.make_async_copy(h_hbm.at[pl.ds(0, tm * TOKEN_ROWS)], xbuf, sem).wait()

    def wait_scatter(ybuf, sem):
        pltpu.make_async_copy(ybuf, o_hbm.at[pl.ds(0, tm * TOKEN_ROWS)], sem).wait()

    @pl.when(i == 0)
    def _():
        gather(tok0_ref, x0, gsem.at[0])
        y1[...] = jnp.zeros_like(y1)

    def step(p):
        xb, xn, yb, yp = xs[p], xs[1 - p], ys[p], ys[1 - p]
        wait_gather(xb, gsem.at[p])

        @pl.when(i >= 1)
        def _():
            wait_scatter(yb, ssem.at[p])

        @pl.when(n_used > 0)
        def _():
            scatter(dstp_ref, yp, ssem.at[1 - p])
            gather(tokn_ref, xn, gsem.at[1 - p])

        x = _load_token_tiles(xb).astype(BF16)
        hid = _dot(x, w1b[...]) + b1_ref[0]
        f = hid.shape[1] // 2
        g = jnp.minimum(hid[:, :f], SWIGLU_LIMIT)
        lin = jnp.clip(hid[:, f:], -SWIGLU_LIMIT, SWIGLU_LIMIT)
        act = g * jax.nn.sigmoid(SWIGLU_ALPHA * g) * (lin + 1.0)
        _store_token_tiles(yb, _dot(act.astype(BF16), w2b[...]) + b2_ref[0])

        @pl.when(i == n_used - 1)
        def _():
            wait_scatter(yp, ssem.at[1 - p])
            scatter(dstc_ref, yb, ssem.at[p])
            wait_scatter(yb, ssem.at[p])
            wait_gather(xn, gsem.at[1 - p])

    for p in range(2):
        @pl.when(jnp.logical_and(i < n_used, i % 2 == p))
        def _(p=p):
            step(p)


def _moe(h2, row_tok, row_dst, blk_e, n_used, w1, b1, w2, b2, out_rows, tm):
    n_tiles = row_tok.shape[0]
    d = w1.shape[1]
    f2 = w1.shape[2]
    tile_rows = (tm * TOKEN_ROWS, LANES)
    smem = lambda fn: pl.BlockSpec((1, 1, tm), fn, memory_space=pltpu.SMEM)
    grid_spec = pltpu.PrefetchScalarGridSpec(
        num_scalar_prefetch=2,
        grid=(n_tiles,),
        in_specs=[
            smem(lambda i, be, nu: (0, 0, 0)),
            smem(lambda i, be, nu: (jnp.minimum(i + 1, nu[0] - 1), 0, 0)),
            smem(lambda i, be, nu: (i, 0, 0)),
            smem(lambda i, be, nu: (i + 1, 0, 0)),
            pl.BlockSpec(memory_space=pl.ANY),
            pl.BlockSpec((1, d, f2), lambda i, be, nu: (be[i], 0, 0)),
            pl.BlockSpec((1, 1, f2), lambda i, be, nu: (be[i], 0, 0)),
            pl.BlockSpec((1, f2 // 2, d), lambda i, be, nu: (be[i], 0, 0)),
            pl.BlockSpec((1, 1, d), lambda i, be, nu: (be[i], 0, 0)),
        ],
        out_specs=pl.BlockSpec(memory_space=pl.ANY),
        scratch_shapes=[pltpu.VMEM(tile_rows, U32), pltpu.VMEM(tile_rows, U32),
                        pltpu.VMEM(tile_rows, U32), pltpu.VMEM(tile_rows, U32),
                        pltpu.VMEM((d, f2), BF16), pltpu.VMEM((f2 // 2, d), BF16),
                        pltpu.SemaphoreType.DMA((2,)), pltpu.SemaphoreType.DMA((2,))],
    )
    return pl.pallas_call(
        functools.partial(_moe_kernel, tm=tm),
        out_shape=jax.ShapeDtypeStruct((out_rows * TOKEN_ROWS, LANES), U32),
        grid_spec=grid_spec,
        compiler_params=_cparams("arbitrary"),
        name="moe_experts",
    )(blk_e, n_used, row_tok, row_tok, row_dst, row_dst, h2, w1, b1, w2, b2)


def _combine_kernel(*refs, tq, final):
    ys = refs[:TOP_K]
    if final:
        gate_ref, x_ref, g2_ref, fg_ref, o_ref = refs[TOP_K:]
    else:
        gate_ref, x_ref, g2_ref, o_ref = refs[TOP_K:]
    f = None
    d = x_ref.shape[1]
    for k in range(TOP_K):
        gcol = jnp.broadcast_to(gate_ref[k:k + 1, :], (LANES, tq)).T
        t = _load_token_tiles(ys[k]) * jnp.tile(gcol, (1, d // LANES))
        f = t if f is None else f + t
    xn = x_ref[...] + g2_ref[0] * f
    if final:
        xn = _rms(xn, fg_ref[...])
    o_ref[...] = xn


def _combine(ys, t_tot, t_off, gate, x, mod, g2_j, group_rows, final_g, tq):
    m, d = x.shape
    per_group = group_rows // tq
    final = final_g is not None
    in_specs = [pl.BlockSpec((tq * TOKEN_ROWS, LANES),
                             functools.partial(lambda i, k: ((k * t_tot + t_off) // tq + i, 0), k=k))
                for k in range(TOP_K)]
    in_specs += [
        pl.BlockSpec((TOP_K, tq), lambda i: (0, i)),
        pl.BlockSpec((tq, d), lambda i: (i, 0)),
        pl.BlockSpec((1, 1, d), lambda i: ((i // per_group) * 6 + g2_j, 0, 0)),
    ]
    args = [ys] * TOP_K + [gate, x, mod]
    if final:
        in_specs.append(pl.BlockSpec((1, d), lambda i: (0, 0)))
        args.append(final_g)
    return pl.pallas_call(
        functools.partial(_combine_kernel, tq=tq, final=final),
        out_shape=jax.ShapeDtypeStruct((m, d), F32),
        grid=(m // tq,),
        in_specs=in_specs,
        out_specs=pl.BlockSpec((tq, d), lambda i: (i, 0)),
        compiler_params=_cparams("parallel"),
        name="moe_combine",
    )(*args)


def _rope_tables(seq, rot_dim):
    t = np.arange(seq)
    row = (t // GRID_W).astype(np.float32)
    col = (t % GRID_W).astype(np.float32)
    axis_dim = rot_dim // 2
    inv = jnp.asarray(ROPE_THETA, F32) ** (-jnp.arange(0, axis_dim, 2, dtype=F32) / axis_dim)
    ang_r = jnp.asarray(row)[:, None] * inv
    ang_c = jnp.asarray(col)[:, None] * inv
    zeros = jnp.zeros_like(ang_r)
    cos = jnp.concatenate([jnp.cos(ang_r)] * 2 + [jnp.cos(ang_c)] * 2, axis=1)
    s_a = jnp.concatenate([-jnp.sin(ang_r), zeros, -jnp.sin(ang_c), zeros], axis=1)
    s_b = jnp.concatenate([zeros, jnp.sin(ang_r), zeros, jnp.sin(ang_c)], axis=1)
    pad = LANES - rot_dim
    if pad:
        cos, s_a, s_b = (jnp.pad(a, ((0, 0), (0, pad))) for a in (cos, s_a, s_b))
    return cos, s_a, s_b


def _route(idx, counts, tm):
    t_tot = idx.shape[1]
    n_assign = t_tot * TOP_K
    n_rows = n_assign + N_EXPERTS * tm
    n_tiles = n_rows // tm
    counts = counts.astype(jnp.int32)
    pad_need = (-counts) % tm
    flat_e = idx.T.reshape(-1)
    slot = jnp.arange(tm, dtype=jnp.int32)[None, :]
    pad_keys = jnp.where(slot < pad_need[:, None], jnp.arange(N_EXPERTS, dtype=jnp.int32)[:, None], N_EXPERTS)
    keys = jnp.concatenate([flat_e, pad_keys.reshape(-1)])
    pos_bits = (n_rows - 1).bit_length()
    packed = lax.sort((keys << pos_bits) | jnp.arange(n_rows, dtype=jnp.int32))
    skeys = packed >> pos_bits
    svals = packed & ((1 << pos_bits) - 1)
    real = svals < n_assign
    row_tok = jnp.where(real, svals // TOP_K, 0)
    rows = jnp.arange(n_rows, dtype=jnp.int32)
    row_dst = jnp.where(real, (svals % TOP_K) * t_tot + svals // TOP_K, n_assign + rows % tm)
    tile_e = skeys[::tm]
    n_used = jnp.sum(tile_e < N_EXPERTS).astype(jnp.int32)
    last_e = tile_e[jnp.maximum(n_used - 1, 0)]
    blk_e = jnp.where(tile_e < N_EXPERTS, tile_e, last_e).astype(jnp.int32)
    row_dst = jnp.concatenate([n_assign + jnp.arange(tm, dtype=jnp.int32), row_dst])
    row_tok = (row_tok * TOKEN_ROWS).reshape(n_tiles, 1, tm)
    row_dst = (row_dst * TOKEN_ROWS).reshape(n_tiles + 1, 1, tm)
    first = jnp.concatenate([jnp.ones((1,), jnp.int32), (blk_e[1:] != blk_e[:-1]).astype(jnp.int32)])
    meta = jnp.concatenate([n_used.reshape(1), first])
    return row_tok, row_dst, blk_e, meta


def _pick(m, pref):
    t = pref
    while m % t:
        t //= 2
    return t


def kernel(x, c, ctx, c_ctx, ada_w, ada_b, norm_mix_g, norm_ffn_g, ab_w_in, sgu_norm_g, sgu_w, sgu_b,
           gqa_q_norm_g, gqa_k_norm_g, ab_w_out, mla_w_in, mla_q_norm_g, mla_kv_norm_g, mla_w_uq, mla_w_ukv,
           mla_w_out, router_w, router_b, moe_w1, moe_b1, moe_w2, moe_b2, final_norm_g):
    bn, s, d = x.shape
    l = ctx.shape[1]
    depth = ada_w.shape[0]
    m_lat, m_ctx = bn * s, bn * l
    tm_lat = _pick(s, PROJ_TM)
    tm_ctx = _pick(m_ctx, PROJ_TM)
    tm_moe = MOE_TM
    tq_comb = _pick(int(np.gcd(s, m_ctx)), MOE_TM)

    n_cond = (bn + 1 + 7) // 8 * 8
    cond = jnp.concatenate([c, c_ctx[None, :], jnp.zeros((n_cond - bn - 1, d), F32)], axis=0)
    mod_all = _ada(cond, ada_w, ada_b)

    rope_gqa = _rope_tables(s, HEAD_DIM)
    rope_mla = _rope_tables(s, MLA_ROPE)
    w1_all = moe_w1.reshape(depth * N_EXPERTS, d, -1)
    w2_all = moe_w2.reshape(depth * N_EXPERTS, -1, d)
    b1_all = moe_b1.reshape(depth * N_EXPERTS, 1, -1)
    b2_all = moe_b2.reshape(depth * N_EXPERTS, 1, d)

    xl = x.reshape(m_lat, d)
    xc = ctx.reshape(m_ctx, d)
    for i in range(depth):
        ctx_out = i < depth - 1
        j = i // 2
        mod_lat = mod_all[i, :bn].reshape(bn * 6, 1, d)
        mod_ctx = mod_all[i, bn].reshape(6, 1, d)
        ng = norm_mix_g[i].reshape(1, d)
        if i % 2 == 0:
            w_in = ab_w_in[j].astype(BF16)
            w_out = ab_w_out[j].astype(BF16)
            sgu_g = sgu_norm_g[j].reshape(1, SGU_WIDTH)
            sgu_wb = sgu_w[j].astype(BF16)
            sgu_bfull = jnp.broadcast_to(sgu_b[j][:, :, None], (SGU_GROUPS, CHUNK, SGU_GROUP_CH))
            q_g = gqa_q_norm_g[j].reshape(1, HEAD_DIM)
            k_g = gqa_k_norm_g[j].reshape(1, HEAD_DIM)
            oa, q, k, v = _proj_ab(xl, mod_lat, (1, 0), s, ng, w_in, sgu_g, sgu_wb, sgu_bfull, q_g, k_g,
                                   rope_gqa, tm_lat)
            oac, qc, kc, vc = _proj_ab(xc, mod_ctx, (1, 0), m_ctx, ng, w_in, sgu_g, sgu_wb, sgu_bfull, q_g, k_g,
                                       None, tm_ctx)
            ob = _attention(q, [(kc, vc), (k, v)], bn, GQA_HEADS, GQA_KV_HEADS, HEAD_DIM, HEAD_DIM, ATTN_TQ)
            ins = [(oa, w_out[:SGU_WIDTH]), (ob, w_out[SGU_WIDTH:])]
            if ctx_out:
                obc = _attention(qc, [(kc, vc)], bn, GQA_HEADS, GQA_KV_HEADS, HEAD_DIM, HEAD_DIM, ATTN_TQ)
                ins_c = [(oac, w_out[:SGU_WIDTH]), (obc, w_out[SGU_WIDTH:])]
        else:
            w_in_f = mla_w_in[j]
            w_in = jnp.pad(w_in_f, ((0, 0), (0, MLA_IN_PAD - w_in_f.shape[1]))).astype(BF16)
            w_in_kv = w_in[:, MLA_Q_LORA:]
            w_out = mla_w_out[j].astype(BF16)
            q_g = mla_q_norm_g[j].reshape(1, MLA_Q_LORA)
            kv_g = mla_kv_norm_g[j].reshape(1, MLA_KV_LORA)
            w_uq = mla_w_uq[j].reshape(MLA_Q_LORA, MLA_HEADS, MLA_QK)
            w_uq = jnp.pad(w_uq, ((0, 0), (0, 0), (0, MLA_QK_PAD - MLA_QK)))
            w_uq = w_uq.reshape(MLA_Q_LORA, MLA_HEADS * MLA_QK_PAD).astype(BF16)
            w_ukv = mla_w_ukv[j].reshape(MLA_KV_LORA, MLA_HEADS, MLA_NOPE + MLA_V)
            w_uk = w_ukv[:, :, :MLA_NOPE].reshape(MLA_KV_LORA, MLA_HEADS * MLA_NOPE).astype(BF16)
            w_uv = w_ukv[:, :, MLA_NOPE:].reshape(MLA_KV_LORA, MLA_HEADS * MLA_V).astype(BF16)
            q, k, v = _proj_mla(xl, mod_lat, (1, 0), s, ng, w_in, q_g, w_uq, kv_g, w_uk, w_uv, rope_mla, True,
                                tm_lat)
            if ctx_out:
                qc, kc, vc = _proj_mla(xc, mod_ctx, (1, 0), m_ctx, ng, w_in, q_g, w_uq, kv_g, w_uk, w_uv, None,
                                       True, tm_ctx)
            else:
                kc, vc = _proj_mla(xc, mod_ctx, (1, 0), m_ctx, ng, w_in_kv, None, None, kv_g, w_uk, w_uv, None,
                                   False, tm_ctx)
            ob = _attention(q, [(kc, vc), (k, v)], bn, MLA_HEADS, MLA_HEADS, MLA_QK_PAD, MLA_V, ATTN_TQ)
            ins = [(ob, w_out)]
            if ctx_out:
                obc = _attention(qc, [(kc, vc)], bn, MLA_HEADS, MLA_HEADS, MLA_QK_PAD, MLA_V, ATTN_TQ)
                ins_c = [(obc, w_out)]

        nfg = norm_ffn_g[i].reshape(1, d)
        rwt = router_w[i].T
        rb = router_b[i].reshape(N_EXPERTS, 1)
        cnt0 = jnp.zeros((N_EXPERTS, LANES), F32)
        xl, h2, idx, gate, cnt = _out_router(ins, xl, mod_lat, (2, 4, 3), s, nfg, rwt, rb, cnt0, tm_lat)
        if ctx_out:
            xc, h2c, idx_c, gate_c, cnt = _out_router(ins_c, xc, mod_ctx, (2, 4, 3), m_ctx, nfg, rwt, rb, cnt, tm_ctx)
            h2 = jnp.concatenate([h2, h2c], axis=0)
            idx = jnp.concatenate([idx, idx_c], axis=1)
        t_tot = h2.shape[0] // TOKEN_ROWS
        n_assign = t_tot * TOP_K
        row_tok, row_dst, blk_e, n_used = _route(idx, cnt[:, 0], tm_moe)
        ys = _moe(h2, row_tok, row_dst, blk_e + i * N_EXPERTS, n_used, w1_all, b1_all, w2_all, b2_all,
                  n_assign + tm_moe, tm_moe)
        last = i == depth - 1
        fg = final_norm_g.reshape(1, d) if last else None
        xl = _combine(ys, t_tot, 0, gate, xl, mod_lat, 5, s, fg, tq_comb)
        if ctx_out:
            xc = _combine(ys, t_tot, m_lat, gate_c, xc, mod_ctx, 5, m_ctx, None, tq_comb)
    return xl.reshape(bn, s, d)
```

```python
import functools

import jax
import jax.numpy as jnp
import numpy as np
from jax import lax
from jax.experimental import pallas as pl
from jax.experimental.pallas import tpu as pltpu

F32 = jnp.float32
BF16 = jnp.bfloat16
U32 = jnp.uint32

GRID_W = 64
EPS = 1e-6
ROPE_THETA = 10000.0
CHUNK = 128
SGU_GROUPS = 4
SGU_GROUP_CH = 128
SGU_WIDTH = SGU_GROUPS * SGU_GROUP_CH
GQA_HEADS = 4
GQA_KV_HEADS = 2
HEAD_DIM = 128
GQA_Q_W = GQA_HEADS * HEAD_DIM
GQA_KV_W = GQA_KV_HEADS * HEAD_DIM
A_END = 2 * SGU_WIDTH
K_OFF = A_END + GQA_Q_W
V_OFF = K_OFF + GQA_KV_W
MLA_HEADS = 8
MLA_Q_LORA = 512
MLA_KV_LORA = 256
MLA_NOPE = 128
MLA_ROPE = 64
MLA_V = 128
MLA_QK = MLA_NOPE + MLA_ROPE
MLA_QK_PAD = 256
MLA_IN_PAD = 896
N_EXPERTS = 32
TOP_K = 4
D_EXPERT = 1024
SWIGLU_LIMIT = 7.0
SWIGLU_ALPHA = 1.702

LANES = 128
TOKEN_ROWS = 4
DMA_UNROLL = 64
ROW_SUB = 4
ATTN_TQ = 256
PROJ_TM = 1024
MOE_TM = 512
ADA_TN = 1536
VMEM_LIMIT = 56 * 1024 * 1024


def _cparams(*sem):
    return pltpu.CompilerParams(dimension_semantics=sem, vmem_limit_bytes=VMEM_LIMIT)


def _dot(a, b):
    return jnp.dot(a, b, preferred_element_type=F32)


def _dot_nt(a, b):
    return lax.dot_general(a, b, (((1,), (1,)), ((), ())), preferred_element_type=F32)


def _split2(a):
    hi = a.astype(BF16)
    lo = (a - hi.astype(F32)).astype(BF16)
    return hi, lo


def _row_sub(tm):
    return ROW_SUB if tm % (ROW_SUB * CHUNK) == 0 else 1


def _rms(x, g):
    return x * lax.rsqrt(jnp.mean(x * x, axis=-1, keepdims=True) + EPS) * g


def _store_token_tiles(ref, val):
    n, d = val.shape
    h = d // 2
    lo = lax.bitcast_convert_type(val[:, :h].astype(BF16).astype(F32), U32) >> 16
    hi = lax.bitcast_convert_type(val[:, h:].astype(BF16).astype(F32), U32) & jnp.uint32(0xFFFF0000)
    packed = hi | lo
    for s in range(TOKEN_ROWS):
        ref[pl.ds(s, n, stride=TOKEN_ROWS), :] = packed[:, s * LANES:(s + 1) * LANES]


def _load_token_tiles(ref):
    n = ref.shape[0] // TOKEN_ROWS
    packed = jnp.concatenate([ref[pl.ds(s, n, stride=TOKEN_ROWS), :] for s in range(TOKEN_ROWS)], axis=1)
    lo = lax.bitcast_convert_type(packed << 16, F32)
    hi = lax.bitcast_convert_type(packed & jnp.uint32(0xFFFF0000), F32)
    return jnp.concatenate([lo, hi], axis=1)


def _rope(x, c, s_a, s_b, half):
    n = x.shape[-1]
    return x * c + pltpu.roll(x, n - half, 1) * s_a + pltpu.roll(x, half, 1) * s_b


def _ada_kernel(c_ref, w_ref, b_ref, o_ref):
    c = c_ref[...]
    sc = c * jax.nn.sigmoid(c)
    c_hi, c_lo = _split2(sc)
    w = w_ref[0]
    w_hi, w_lo = _split2(w)
    acc = _dot(c_hi, w_hi) + _dot(c_hi, w_lo) + _dot(c_lo, w_hi)
    o_ref[0] = acc + b_ref[0]


def _ada(cond, ada_w, ada_b):
    depth, d, n = ada_w.shape
    rows = cond.shape[0]
    tn = _pick(n, ADA_TN)
    return pl.pallas_call(
        _ada_kernel,
        out_shape=jax.ShapeDtypeStruct((depth, rows, n), F32),
        grid=(depth, n // tn),
        in_specs=[
            pl.BlockSpec((rows, d), lambda l, j: (0, 0)),
            pl.BlockSpec((1, d, tn), lambda l, j: (l, 0, j)),
            pl.BlockSpec((1, 1, tn), lambda l, j: (l, 0, j)),
        ],
        out_specs=pl.BlockSpec((1, rows, tn), lambda l, j: (l, 0, j)),
        compiler_params=_cparams("arbitrary", "arbitrary"),
        name="ada_mod",
    )(cond, ada_w, ada_b.reshape(depth, 1, n))


def _proj_ab_kernel(*refs, use_rope, tm, sub):
    if use_rope:
        (x_ref, sc_ref, sh_ref, g_ref, w_ref, sgug_ref, sguw_ref, sgub_ref, qg_ref, kg_ref,
         c_ref, sa_ref, sb_ref, oa_ref, q_ref, k_ref, v_ref) = refs
    else:
        (x_ref, sc_ref, sh_ref, g_ref, w_ref, sgug_ref, sguw_ref, sgub_ref, qg_ref, kg_ref,
         oa_ref, q_ref, k_ref, v_ref) = refs
    ts = tm // sub
    n_chunks = ts // CHUNK
    qg = qg_ref[...] * (HEAD_DIM ** -0.5)
    for b in range(sub):
        rows = slice(b * ts, (b + 1) * ts)
        x = x_ref[rows, :]
        h = _rms(x, g_ref[...]) * (1.0 + sc_ref[0]) + sh_ref[0]
        p = _dot(h.astype(BF16), w_ref[...])
        rope = (lambda t: _rope(t, c_ref[rows, :], sa_ref[rows, :], sb_ref[rows, :], HEAD_DIM // 4)) \
            if use_rope else (lambda t: t)

        for g in range(SGU_GROUPS):
            u = jax.nn.gelu(p[:, g * SGU_GROUP_CH:(g + 1) * SGU_GROUP_CH])
            v = jax.nn.gelu(p[:, SGU_WIDTH + g * SGU_GROUP_CH:SGU_WIDTH + (g + 1) * SGU_GROUP_CH])
            v = _rms(v, sgug_ref[:, g * SGU_GROUP_CH:(g + 1) * SGU_GROUP_CH]).astype(BF16)
            vcat = jnp.concatenate([v[c * CHUNK:(c + 1) * CHUNK, :] for c in range(n_chunks)], axis=1)
            mixed = _dot(sguw_ref[g], vcat)
            bias = sgub_ref[g]
            for c in range(n_chunks):
                gate = mixed[:, c * SGU_GROUP_CH:(c + 1) * SGU_GROUP_CH] + bias
                oa_ref[b * ts + c * CHUNK:b * ts + (c + 1) * CHUNK, g * SGU_GROUP_CH:(g + 1) * SGU_GROUP_CH] = (
                    u[c * CHUNK:(c + 1) * CHUNK, :] * gate).astype(oa_ref.dtype)

        for hd in range(GQA_HEADS):
            qh = rope(_rms(p[:, A_END + hd * HEAD_DIM:A_END + (hd + 1) * HEAD_DIM], qg))
            q_ref[rows, hd * HEAD_DIM:(hd + 1) * HEAD_DIM] = qh.astype(q_ref.dtype)
        for hd in range(GQA_KV_HEADS):
            kh = rope(_rms(p[:, K_OFF + hd * HEAD_DIM:K_OFF + (hd + 1) * HEAD_DIM], kg_ref[...]))
            k_ref[rows, hd * HEAD_DIM:(hd + 1) * HEAD_DIM] = kh.astype(k_ref.dtype)
        v_ref[rows, :] = p[:, V_OFF:].astype(v_ref.dtype)


def _proj_ab(x, mod, mod_rows, group_rows, norm_g, w_in, sgu_g, sgu_w, sgu_bfull, q_g, k_g, rope, tm):
    m, d = x.shape
    per_group = group_rows // tm
    sc_j, sh_j = mod_rows
    use_rope = rope is not None
    in_specs = [
        pl.BlockSpec((tm, d), lambda i: (i, 0)),
        pl.BlockSpec((1, 1, d), lambda i: ((i // per_group) * 6 + sc_j, 0, 0)),
        pl.BlockSpec((1, 1, d), lambda i: ((i // per_group) * 6 + sh_j, 0, 0)),
        pl.BlockSpec((1, d), lambda i: (0, 0)),
        pl.BlockSpec(w_in.shape, lambda i: (0, 0)),
        pl.BlockSpec((1, SGU_WIDTH), lambda i: (0, 0)),
        pl.BlockSpec(sgu_w.shape, lambda i: (0, 0, 0)),
        pl.BlockSpec(sgu_bfull.shape, lambda i: (0, 0, 0)),
        pl.BlockSpec((1, HEAD_DIM), lambda i: (0, 0)),
        pl.BlockSpec((1, HEAD_DIM), lambda i: (0, 0)),
    ]
    args = [x, mod, mod, norm_g, w_in, sgu_g, sgu_w, sgu_bfull, q_g, k_g]
    if use_rope:
        s_tiles = rope[0].shape[0] // tm
        for t in rope:
            in_specs.append(pl.BlockSpec((tm, HEAD_DIM), lambda i: (i % s_tiles, 0)))
            args.append(t)
    return pl.pallas_call(
        functools.partial(_proj_ab_kernel, use_rope=use_rope, tm=tm, sub=_row_sub(tm)),
        out_shape=(jax.ShapeDtypeStruct((m, SGU_WIDTH), BF16),
                   jax.ShapeDtypeStruct((m, GQA_Q_W), BF16),
                   jax.ShapeDtypeStruct((m, GQA_KV_W), BF16),
                   jax.ShapeDtypeStruct((m, GQA_KV_W), BF16)),
        grid=(m // tm,),
        in_specs=in_specs,
        out_specs=(pl.BlockSpec((tm, SGU_WIDTH), lambda i: (i, 0)),
                   pl.BlockSpec((tm, GQA_Q_W), lambda i: (i, 0)),
                   pl.BlockSpec((tm, GQA_KV_W), lambda i: (i, 0)),
                   pl.BlockSpec((tm, GQA_KV_W), lambda i: (i, 0))),
        compiler_params=_cparams("parallel"),
        name="proj_ab",
    )(*args)


def _proj_mla_kernel(*refs, with_q, use_rope, sub):
    it = iter(refs)
    x_ref, sc_ref, sh_ref, g_ref, w_ref = (next(it) for _ in range(5))
    if with_q:
        qg_ref, wuq_ref = next(it), next(it)
    kvg_ref, wuk_ref, wuv_ref = next(it), next(it), next(it)
    if use_rope:
        c_ref, sa_ref, sb_ref = next(it), next(it), next(it)
    if with_q:
        q_ref = next(it)
    k_ref, v_ref = next(it), next(it)

    ts = x_ref.shape[0] // sub
    off = MLA_Q_LORA if with_q else 0
    for b in range(sub):
        rows = slice(b * ts, (b + 1) * ts)
        rope = (lambda t: _rope(t, c_ref[rows, :], sa_ref[rows, :], sb_ref[rows, :], MLA_ROPE // 4)) \
            if use_rope else (lambda t: t)
        x = x_ref[rows, :]
        h = _rms(x, g_ref[...]) * (1.0 + sc_ref[0]) + sh_ref[0]
        p = _dot(h.astype(BF16), w_ref[...])
        ckv = _rms(p[:, off:off + MLA_KV_LORA], kvg_ref[...]).astype(BF16)
        kr = rope(p[:, off + MLA_KV_LORA:off + MLA_KV_LORA + LANES])
        kr = kr.astype(k_ref.dtype)
        k_nope = _dot(ckv, wuk_ref[...])
        v_ref[rows, :] = _dot(ckv, wuv_ref[...]).astype(v_ref.dtype)
        for hd in range(MLA_HEADS):
            k_ref[rows, hd * MLA_QK_PAD:hd * MLA_QK_PAD + MLA_NOPE] = (
                k_nope[:, hd * MLA_NOPE:(hd + 1) * MLA_NOPE].astype(k_ref.dtype))
            k_ref[rows, hd * MLA_QK_PAD + MLA_NOPE:(hd + 1) * MLA_QK_PAD] = kr
        if with_q:
            cq = _rms(p[:, :MLA_Q_LORA], qg_ref[...]).astype(BF16)
            q = _dot(cq, wuq_ref[...]) * (MLA_QK ** -0.5)
            for hd in range(MLA_HEADS):
                q_ref[rows, hd * MLA_QK_PAD:hd * MLA_QK_PAD + MLA_NOPE] = (
                    q[:, hd * MLA_QK_PAD:hd * MLA_QK_PAD + MLA_NOPE].astype(q_ref.dtype))
                qr = rope(q[:, hd * MLA_QK_PAD + MLA_NOPE:(hd + 1) * MLA_QK_PAD])
                q_ref[rows, hd * MLA_QK_PAD + MLA_NOPE:(hd + 1) * MLA_QK_PAD] = qr.astype(q_ref.dtype)


def _proj_mla(x, mod, mod_rows, group_rows, norm_g, w_in, q_g, w_uq, kv_g, w_uk, w_uv, rope, with_q, tm):
    m, d = x.shape
    per_group = group_rows // tm
    sc_j, sh_j = mod_rows
    use_rope = rope is not None
    const2 = lambda i: (0, 0)
    in_specs = [
        pl.BlockSpec((tm, d), lambda i: (i, 0)),
        pl.BlockSpec((1, 1, d), lambda i: ((i // per_group) * 6 + sc_j, 0, 0)),
        pl.BlockSpec((1, 1, d), lambda i: ((i // per_group) * 6 + sh_j, 0, 0)),
        pl.BlockSpec((1, d), const2),
        pl.BlockSpec(w_in.shape, const2),
    ]
    args = [x, mod, mod, norm_g, w_in]
    if with_q:
        in_specs += [pl.BlockSpec(q_g.shape, const2), pl.BlockSpec(w_uq.shape, const2)]
        args += [q_g, w_uq]
    in_specs += [pl.BlockSpec(kv_g.shape, const2), pl.BlockSpec(w_uk.shape, const2),
                 pl.BlockSpec(w_uv.shape, const2)]
    args += [kv_g, w_uk, w_uv]
    if use_rope:
        s_tiles = rope[0].shape[0] // tm
        for t in rope:
            in_specs.append(pl.BlockSpec((tm, LANES), lambda i: (i % s_tiles, 0)))
            args.append(t)
    kw = MLA_HEADS * MLA_QK_PAD
    vw = MLA_HEADS * MLA_V
    out_shape = [jax.ShapeDtypeStruct((m, kw), BF16), jax.ShapeDtypeStruct((m, vw), BF16)]
    out_specs = [pl.BlockSpec((tm, kw), lambda i: (i, 0)), pl.BlockSpec((tm, vw), lambda i: (i, 0))]
    if with_q:
        out_shape = [jax.ShapeDtypeStruct((m, kw), BF16)] + out_shape
        out_specs = [pl.BlockSpec((tm, kw), lambda i: (i, 0))] + out_specs
    return pl.pallas_call(
        functools.partial(_proj_mla_kernel, with_q=with_q, use_rope=use_rope, sub=1),
        out_shape=tuple(out_shape),
        grid=(m // tm,),
        in_specs=in_specs,
        out_specs=tuple(out_specs),
        compiler_params=_cparams("parallel"),
        name="proj_mla",
    )(*args)


def _attn_kernel(*refs, n_seg, tq, n_q, group, dk, dv):
    q_ref = refs[0]
    kv = refs[1:1 + 2 * n_seg]
    o_ref = refs[1 + 2 * n_seg]

    for c in range(n_q):
        rows = slice(c * tq, (c + 1) * tq)
        q = jnp.concatenate([q_ref[rows, g * dk:(g + 1) * dk] for g in range(group)], axis=0)
        ss = [_dot_nt(q, kv[2 * j][...]) for j in range(n_seg)]
        mx = ss[0].max(axis=-1, keepdims=True)
        for s in ss[1:]:
            mx = jnp.maximum(mx, s.max(axis=-1, keepdims=True))
        acc = None
        den = None
        for j, s in enumerate(ss):
            e = jnp.exp(s - mx)
            d = e.sum(axis=-1, keepdims=True)
            o = _dot(e.astype(BF16), kv[2 * j + 1][...])
            acc = o if acc is None else acc + o
            den = d if den is None else den + d
        o = (acc / den).astype(o_ref.dtype)
        for g in range(group):
            o_ref[rows, g * dv:(g + 1) * dv] = o[g * tq:(g + 1) * tq, :]


def _attention(q, segs, batch, heads, kv_heads, dk, dv, tq):
    t_q = q.shape[0] // batch
    group = heads // kv_heads
    in_specs = [pl.BlockSpec((t_q, group * dk), lambda b, h: (b, h))]
    args = [q]
    for k, v in segs:
        t_k = k.shape[0] // batch
        in_specs.append(pl.BlockSpec((t_k, dk), lambda b, h: (b, h)))
        in_specs.append(pl.BlockSpec((t_k, dv), lambda b, h: (b, h)))
        args += [k, v]
    tq = min(tq, t_q)
    return pl.pallas_call(
        functools.partial(_attn_kernel, n_seg=len(segs), tq=tq, n_q=t_q // tq, group=group, dk=dk, dv=dv),
        out_shape=jax.ShapeDtypeStruct((q.shape[0], heads * dv), BF16),
        grid=(batch, kv_heads),
        in_specs=in_specs,
        out_specs=pl.BlockSpec((t_q, group * dv), lambda b, h: (b, h)),
        compiler_params=_cparams("parallel", "parallel"),
        name="attention",
    )(*args)


def _out_router_kernel(*refs, n_in, tm, sub):
    ins = refs[:2 * n_in]
    (x_ref, g1_ref, sc_ref, sh_ref, ng_ref, rw_ref, rb_ref, cnt0_ref,
     xo_ref, h2_ref, idx_ref, gate_ref, cnt_ref) = refs[2 * n_in:]
    i = pl.program_id(0)

    @pl.when(i == 0)
    def _():
        cnt_ref[...] = cnt0_ref[...]

    ts = tm // sub
    w_hi, w_lo = _split2(rw_ref[...])
    for b in range(sub):
        rows = slice(b * ts, (b + 1) * ts)
        y = None
        for j in range(n_in):
            t = _dot(ins[2 * j][rows, :], ins[2 * j + 1][...])
            y = t if y is None else y + t
        xn = x_ref[rows, :] + g1_ref[0] * y
        xo_ref[rows, :] = xn
        h2 = _rms(xn, ng_ref[...]) * (1.0 + sc_ref[0]) + sh_ref[0]
        _store_token_tiles(h2_ref.at[pl.ds(b * ts * TOKEN_ROWS, ts * TOKEN_ROWS)], h2)

        h_hi, h_lo = _split2(h2)
        logits = _dot_nt(w_hi, h_hi) + _dot_nt(w_hi, h_lo) + _dot_nt(w_lo, h_hi) + rb_ref[...]

        e_iota = lax.broadcasted_iota(jnp.int32, logits.shape, 0)
        work = logits
        tops, idxs = [], []
        for _ in range(TOP_K):
            mx = work.max(axis=0, keepdims=True)
            ix = jnp.where(work == mx, e_iota, N_EXPERTS).min(axis=0, keepdims=True)
            tops.append(mx)
            idxs.append(ix)
            work = jnp.where(e_iota == ix, -jnp.inf, work)
        exps = [jnp.exp(t - tops[0]) for t in tops]
        den = exps[0] + exps[1] + exps[2] + exps[3]

        for k in range(TOP_K):
            idx_ref[k:k + 1, rows] = idxs[k]
            gate_ref[k:k + 1, rows] = exps[k] / den
        cnt = sum((e_iota == ix).astype(F32) for ix in idxs)
        cnt_ref[...] = cnt_ref[...] + cnt.sum(axis=1, keepdims=True)


def _out_router(ins, x, mod, mod_rows, group_rows, norm_g, router_wt, router_b, cnt0, tm):
    m, d = x.shape
    per_group = group_rows // tm
    g1_j, sc_j, sh_j = mod_rows
    in_specs, args = [], []
    for a, w in ins:
        in_specs += [pl.BlockSpec((tm, a.shape[1]), lambda i: (i, 0)), pl.BlockSpec(w.shape, lambda i: (0, 0))]
        args += [a, w]
    mod_spec = lambda j: pl.BlockSpec((1, 1, d), lambda i: ((i // per_group) * 6 + j, 0, 0))
    in_specs += [
        pl.BlockSpec((tm, d), lambda i: (i, 0)),
        mod_spec(g1_j), mod_spec(sc_j), mod_spec(sh_j),
        pl.BlockSpec((1, d), lambda i: (0, 0)),
        pl.BlockSpec(router_wt.shape, lambda i: (0, 0)),
        pl.BlockSpec(router_b.shape, lambda i: (0, 0)),
        pl.BlockSpec(cnt0.shape, lambda i: (0, 0)),
    ]
    args += [x, mod, mod, mod, norm_g, router_wt, router_b, cnt0]
    small = lambda dt: jax.ShapeDtypeStruct((TOP_K, m), dt)
    small_spec = pl.BlockSpec((TOP_K, tm), lambda i: (0, i))
    return pl.pallas_call(
        functools.partial(_out_router_kernel, n_in=len(ins), tm=tm, sub=1),
        out_shape=(jax.ShapeDtypeStruct((m, d), F32), jax.ShapeDtypeStruct((m * TOKEN_ROWS, LANES), U32),
                   small(jnp.int32), small(F32),
                   jax.ShapeDtypeStruct(cnt0.shape, F32)),
        grid=(m // tm,),
        in_specs=in_specs,
        out_specs=(pl.BlockSpec((tm, d), lambda i: (i, 0)),
                   pl.BlockSpec((tm * TOKEN_ROWS, LANES), lambda i: (i, 0)),
                   small_spec, small_spec,
                   pl.BlockSpec(cnt0.shape, lambda i: (0, 0))),
        compiler_params=_cparams("arbitrary"),
        name="out_router",
    )(*args)


def _moe_kernel(blk_e_ref, meta_ref, tok0_ref, tokn_ref, dstp_ref, dstc_ref, h_hbm, w1_ref, b1_ref, w2_ref, b2_ref,
                o_hbm, x0, x1, y0, y1, w1b, w2b, gsem, ssem, *, tm):
    i = pl.program_id(0)
    n_used = meta_ref[0]

    @pl.when(jnp.logical_and(i < n_used, meta_ref[1 + i] == 1))
    def _():
        w1b[...] = w1_ref[0].astype(BF16)
        w2b[...] = w2_ref[0].astype(BF16)
    xs, ys = (x0, x1), (y0, y1)

    def tile_at(ref, row):
        return ref.at[pl.ds(pl.multiple_of(row, TOKEN_ROWS), TOKEN_ROWS)]

    def for_tokens(fn):
        def body(b, carry):
            for j in range(DMA_UNROLL):
                fn(b * DMA_UNROLL + j, j % 2)
            return carry
        lax.fori_loop(0, tm // DMA_UNROLL, body, 0)

    def gather(tok_ref, xbuf, sem):
        for_tokens(lambda r, prio: pltpu.make_async_copy(
            tile_at(h_hbm, tok_ref[0, 0, r]), tile_at(xbuf, r * TOKEN_ROWS), sem).start(priority=prio))

    def scatter(dst_ref, ybuf, sem):
        for_tokens(lambda r, prio: pltpu.make_async_copy(
            tile_at(ybuf, r * TOKEN_ROWS), tile_at(o_hbm, dst_ref[0, 0, r]), sem).start(priority=prio))

    def wait_gather(xbuf, sem):
        pltpu.make_async_copy(h_hbm.at[pl.ds(0, tm * TOKEN_ROWS)], xbuf, sem).wait()

    def wait_scatter(ybuf, sem):
        pltpu.make_async_copy(ybuf, o_hbm.at[pl.ds(0, tm * TOKEN_ROWS)], sem).wait()

    @pl.when(i == 0)
    def _():
        gather(tok0_ref, x0, gsem.at[0])
        y1[...] = jnp.zeros_like(y1)

    def step(p):
        xb, xn, yb, yp = xs[p], xs[1 - p], ys[p], ys[1 - p]
        wait_gather(xb, gsem.at[p])

        @pl.when(i >= 1)
        def _():
            wait_scatter(yb, ssem.at[p])

        @pl.when(n_used > 0)
        def _():
            gather(tokn_ref, xn, gsem.at[1 - p])

        x = _load_token_tiles(xb).astype(BF16)
        hid = _dot(x, w1b[...]) + b1_ref[0]
        f = hid.shape[1] // 2
        g = jnp.minimum(hid[:, :f], SWIGLU_LIMIT)
        lin = jnp.clip(hid[:, f:], -SWIGLU_LIMIT, SWIGLU_LIMIT)
        act = (g * jax.nn.sigmoid(SWIGLU_ALPHA * g) * (lin + 1.0)).astype(BF16)

        @pl.when(n_used > 0)
        def _():
            scatter(dstp_ref, yp, ssem.at[1 - p])

        _store_token_tiles(yb, _dot(act, w2b[...]) + b2_ref[0])

        @pl.when(i == n_used - 1)
        def _():
            wait_scatter(yp, ssem.at[1 - p])
            scatter(dstc_ref, yb, ssem.at[p])
            wait_scatter(yb, ssem.at[p])
            wait_gather(xn, gsem.at[1 - p])

    for p in range(2):
        @pl.when(jnp.logical_and(i < n_used, i % 2 == p))
        def _(p=p):
            step(p)


def _moe(h2, row_tok, row_dst, blk_e, n_used, w1, b1, w2, b2, out_rows, tm):
    n_tiles = row_tok.shape[0]
    d = w1.shape[1]
    f2 = w1.shape[2]
    tile_rows = (tm * TOKEN_ROWS, LANES)
    smem = lambda fn: pl.BlockSpec((1, 1, tm), fn, memory_space=pltpu.SMEM)
    grid_spec = pltpu.PrefetchScalarGridSpec(
        num_scalar_prefetch=2,
        grid=(n_tiles,),
        in_specs=[
            smem(lambda i, be, nu: (0, 0, 0)),
            smem(lambda i, be, nu: (jnp.minimum(i + 1, nu[0] - 1), 0, 0)),
            smem(lambda i, be, nu: (i, 0, 0)),
            smem(lambda i, be, nu: (i + 1, 0, 0)),
            pl.BlockSpec(memory_space=pl.ANY),
            pl.BlockSpec((1, d, f2), lambda i, be, nu: (be[i], 0, 0)),
            pl.BlockSpec((1, 1, f2), lambda i, be, nu: (be[i], 0, 0)),
            pl.BlockSpec((1, f2 // 2, d), lambda i, be, nu: (be[i], 0, 0)),
            pl.BlockSpec((1, 1, d), lambda i, be, nu: (be[i], 0, 0)),
        ],
        out_specs=pl.BlockSpec(memory_space=pl.ANY),
        scratch_shapes=[pltpu.VMEM(tile_rows, U32), pltpu.VMEM(tile_rows, U32),
                        pltpu.VMEM(tile_rows, U32), pltpu.VMEM(tile_rows, U32),
                        pltpu.VMEM((d, f2), BF16), pltpu.VMEM((f2 // 2, d), BF16),
                        pltpu.SemaphoreType.DMA((2,)), pltpu.SemaphoreType.DMA((2,))],
    )
    return pl.pallas_call(
        functools.partial(_moe_kernel, tm=tm),
        out_shape=jax.ShapeDtypeStruct((out_rows * TOKEN_ROWS, LANES), U32),
        grid_spec=grid_spec,
        compiler_params=_cparams("arbitrary"),
        name="moe_experts",
    )(blk_e, n_used, row_tok, row_tok, row_dst, row_dst, h2, w1, b1, w2, b2)


def _combine_kernel(*refs, tq, final):
    ys = refs[:TOP_K]
    if final:
        gate_ref, x_ref, g2_ref, fg_ref, o_ref = refs[TOP_K:]
    else:
        gate_ref, x_ref, g2_ref, o_ref = refs[TOP_K:]
    f = None
    d = x_ref.shape[1]
    for k in range(TOP_K):
        gcol = jnp.broadcast_to(gate_ref[k:k + 1, :], (LANES, tq)).T
        t = _load_token_tiles(ys[k]) * jnp.tile(gcol, (1, d // LANES))
        f = t if f is None else f + t
    xn = x_ref[...] + g2_ref[0] * f
    if final:
        xn = _rms(xn, fg_ref[...])
    o_ref[...] = xn


def _combine(ys, t_tot, t_off, gate, x, mod, g2_j, group_rows, final_g, tq):
    m, d = x.shape
    per_group = group_rows // tq
    final = final_g is not None
    in_specs = [pl.BlockSpec((tq * TOKEN_ROWS, LANES),
                             functools.partial(lambda i, k: ((k * t_tot + t_off) // tq + i, 0), k=k))
                for k in range(TOP_K)]
    in_specs += [
        pl.BlockSpec((TOP_K, tq), lambda i: (0, i)),
        pl.BlockSpec((tq, d), lambda i: (i, 0)),
        pl.BlockSpec((1, 1, d), lambda i: ((i // per_group) * 6 + g2_j, 0, 0)),
    ]
    args = [ys] * TOP_K + [gate, x, mod]
    if final:
        in_specs.append(pl.BlockSpec((1, d), lambda i: (0, 0)))
        args.append(final_g)
    return pl.pallas_call(
        functools.partial(_combine_kernel, tq=tq, final=final),
        out_shape=jax.ShapeDtypeStruct((m, d), F32),
        grid=(m // tq,),
        in_specs=in_specs,
        out_specs=pl.BlockSpec((tq, d), lambda i: (i, 0)),
        compiler_params=_cparams("parallel"),
        name="moe_combine",
    )(*args)


def _rope_tables(seq, rot_dim):
    t = np.arange(seq)
    row = (t // GRID_W).astype(np.float32)
    col = (t % GRID_W).astype(np.float32)
    axis_dim = rot_dim // 2
    inv = jnp.asarray(ROPE_THETA, F32) ** (-jnp.arange(0, axis_dim, 2, dtype=F32) / axis_dim)
    ang_r = jnp.asarray(row)[:, None] * inv
    ang_c = jnp.asarray(col)[:, None] * inv
    zeros = jnp.zeros_like(ang_r)
    cos = jnp.concatenate([jnp.cos(ang_r)] * 2 + [jnp.cos(ang_c)] * 2, axis=1)
    s_a = jnp.concatenate([-jnp.sin(ang_r), zeros, -jnp.sin(ang_c), zeros], axis=1)
    s_b = jnp.concatenate([zeros, jnp.sin(ang_r), zeros, jnp.sin(ang_c)], axis=1)
    pad = LANES - rot_dim
    if pad:
        cos, s_a, s_b = (jnp.pad(a, ((0, 0), (0, pad))) for a in (cos, s_a, s_b))
    return cos, s_a, s_b


def _route(idx, counts, tm):
    t_tot = idx.shape[1]
    n_assign = t_tot * TOP_K
    n_rows = n_assign + N_EXPERTS * tm
    n_tiles = n_rows // tm
    counts = counts.astype(jnp.int32)
    pad_need = (-counts) % tm
    flat_e = idx.T.reshape(-1)
    slot = jnp.arange(tm, dtype=jnp.int32)[None, :]
    pad_keys = jnp.where(slot < pad_need[:, None], jnp.arange(N_EXPERTS, dtype=jnp.int32)[:, None], N_EXPERTS)
    keys = jnp.concatenate([flat_e, pad_keys.reshape(-1)])
    pos_bits = (n_rows - 1).bit_length()
    packed = lax.sort((keys << pos_bits) | jnp.arange(n_rows, dtype=jnp.int32))
    skeys = packed >> pos_bits
    svals = packed & ((1 << pos_bits) - 1)
    real = svals < n_assign
    row_tok = jnp.where(real, svals // TOP_K, 0)
    rows = jnp.arange(n_rows, dtype=jnp.int32)
    row_dst = jnp.where(real, (svals % TOP_K) * t_tot + svals // TOP_K, n_assign + rows % tm)
    tile_e = skeys[::tm]
    n_used = jnp.sum(tile_e < N_EXPERTS).astype(jnp.int32)
    last_e = tile_e[jnp.maximum(n_used - 1, 0)]
    blk_e = jnp.where(tile_e < N_EXPERTS, tile_e, last_e).astype(jnp.int32)
    row_dst = jnp.concatenate([n_assign + jnp.arange(tm, dtype=jnp.int32), row_dst])
    row_tok = (row_tok * TOKEN_ROWS).reshape(n_tiles, 1, tm)
    row_dst = (row_dst * TOKEN_ROWS).reshape(n_tiles + 1, 1, tm)
    first = jnp.concatenate([jnp.ones((1,), jnp.int32), (blk_e[1:] != blk_e[:-1]).astype(jnp.int32)])
    meta = jnp.concatenate([n_used.reshape(1), first])
    return row_tok, row_dst, blk_e, meta


def _pick(m, pref):
    t = pref
    while m % t:
        t //= 2
    return t


def kernel(x, c, ctx, c_ctx, ada_w, ada_b, norm_mix_g, norm_ffn_g, ab_w_in, sgu_norm_g, sgu_w, sgu_b,
           gqa_q_norm_g, gqa_k_norm_g, ab_w_out, mla_w_in, mla_q_norm_g, mla_kv_norm_g, mla_w_uq, mla_w_ukv,
           mla_w_out, router_w, router_b, moe_w1, moe_b1, moe_w2, moe_b2, final_norm_g):
    bn, s, d = x.shape
    l = ctx.shape[1]
    depth = ada_w.shape[0]
    m_lat, m_ctx = bn * s, bn * l
    tm_lat = _pick(s, PROJ_TM)
    tm_ctx = _pick(m_ctx, PROJ_TM)
    tm_moe = MOE_TM
    tq_comb = _pick(int(np.gcd(s, m_ctx)), MOE_TM)

    n_cond = (bn + 1 + 7) // 8 * 8
    cond = jnp.concatenate([c, c_ctx[None, :], jnp.zeros((n_cond - bn - 1, d), F32)], axis=0)
    mod_all = _ada(cond, ada_w, ada_b)

    rope_gqa = _rope_tables(s, HEAD_DIM)
    rope_mla = _rope_tables(s, MLA_ROPE)
    w1_all = moe_w1.reshape(depth * N_EXPERTS, d, -1)
    w2_all = moe_w2.reshape(depth * N_EXPERTS, -1, d)
    b1_all = moe_b1.reshape(depth * N_EXPERTS, 1, -1)
    b2_all = moe_b2.reshape(depth * N_EXPERTS, 1, d)

    xl = x.reshape(m_lat, d)
    xc = ctx.reshape(m_ctx, d)
    for i in range(depth):
        ctx_out = i < depth - 1
        j = i // 2
        mod_lat = mod_all[i, :bn].reshape(bn * 6, 1, d)
        mod_ctx = mod_all[i, bn].reshape(6, 1, d)
        ng = norm_mix_g[i].reshape(1, d)
        if i % 2 == 0:
            w_in = ab_w_in[j].astype(BF16)
            w_out = ab_w_out[j].astype(BF16)
            sgu_g = sgu_norm_g[j].reshape(1, SGU_WIDTH)
            sgu_wb = sgu_w[j].astype(BF16)
            sgu_bfull = jnp.broadcast_to(sgu_b[j][:, :, None], (SGU_GROUPS, CHUNK, SGU_GROUP_CH))
            q_g = gqa_q_norm_g[j].reshape(1, HEAD_DIM)
            k_g = gqa_k_norm_g[j].reshape(1, HEAD_DIM)
            oa, q, k, v = _proj_ab(xl, mod_lat, (1, 0), s, ng, w_in, sgu_g, sgu_wb, sgu_bfull, q_g, k_g,
                                   rope_gqa, tm_lat)
            oac, qc, kc, vc = _proj_ab(xc, mod_ctx, (1, 0), m_ctx, ng, w_in, sgu_g, sgu_wb, sgu_bfull, q_g, k_g,
                                       None, tm_ctx)
            ob = _attention(q, [(kc, vc), (k, v)], bn, GQA_HEADS, GQA_KV_HEADS, HEAD_DIM, HEAD_DIM, ATTN_TQ)
            ins = [(oa, w_out[:SGU_WIDTH]), (ob, w_out[SGU_WIDTH:])]
            if ctx_out:
                obc = _attention(qc, [(kc, vc)], bn, GQA_HEADS, GQA_KV_HEADS, HEAD_DIM, HEAD_DIM, ATTN_TQ)
                ins_c = [(oac, w_out[:SGU_WIDTH]), (obc, w_out[SGU_WIDTH:])]
        else:
            w_in_f = mla_w_in[j]
            w_in = jnp.pad(w_in_f, ((0, 0), (0, MLA_IN_PAD - w_in_f.shape[1]))).astype(BF16)
            w_in_kv = w_in[:, MLA_Q_LORA:]
            w_out = mla_w_out[j].astype(BF16)
            q_g = mla_q_norm_g[j].reshape(1, MLA_Q_LORA)
            kv_g = mla_kv_norm_g[j].reshape(1, MLA_KV_LORA)
            w_uq = mla_w_uq[j].reshape(MLA_Q_LORA, MLA_HEADS, MLA_QK)
            w_uq = jnp.pad(w_uq, ((0, 0), (0, 0), (0, MLA_QK_PAD - MLA_QK)))
            w_uq = w_uq.reshape(MLA_Q_LORA, MLA_HEADS * MLA_QK_PAD).astype(BF16)
            w_ukv = mla_w_ukv[j].reshape(MLA_KV_LORA, MLA_HEADS, MLA_NOPE + MLA_V)
            w_uk = w_ukv[:, :, :MLA_NOPE].reshape(MLA_KV_LORA, MLA_HEADS * MLA_NOPE).astype(BF16)
            w_uv = w_ukv[:, :, MLA_NOPE:].reshape(MLA_KV_LORA, MLA_HEADS * MLA_V).astype(BF16)
            q, k, v = _proj_mla(xl, mod_lat, (1, 0), s, ng, w_in, q_g, w_uq, kv_g, w_uk, w_uv, rope_mla, True,
                                tm_lat)
            if ctx_out:
                qc, kc, vc = _proj_mla(xc, mod_ctx, (1, 0), m_ctx, ng, w_in, q_g, w_uq, kv_g, w_uk, w_uv, None,
                                       True, tm_ctx)
            else:
                kc, vc = _proj_mla(xc, mod_ctx, (1, 0), m_ctx, ng, w_in_kv, None, None, kv_g, w_uk, w_uv, None,
                                   False, tm_ctx)
            ob = _attention(q, [(kc, vc), (k, v)], bn, MLA_HEADS, MLA_HEADS, MLA_QK_PAD, MLA_V, ATTN_TQ)
            ins = [(ob, w_out)]
            if ctx_out:
                obc = _attention(qc, [(kc, vc)], bn, MLA_HEADS, MLA_HEADS, MLA_QK_PAD, MLA_V, ATTN_TQ)
                ins_c = [(obc, w_out)]

        nfg = norm_ffn_g[i].reshape(1, d)
        rwt = router_w[i].T
        rb = router_b[i].reshape(N_EXPERTS, 1)
        cnt0 = jnp.zeros((N_EXPERTS, LANES), F32)
        xl, h2, idx, gate, cnt = _out_router(ins, xl, mod_lat, (2, 4, 3), s, nfg, rwt, rb, cnt0, tm_lat)
        if ctx_out:
            xc, h2c, idx_c, gate_c, cnt = _out_router(ins_c, xc, mod_ctx, (2, 4, 3), m_ctx, nfg, rwt, rb, cnt, tm_ctx)
            h2 = jnp.concatenate([h2, h2c], axis=0)
            idx = jnp.concatenate([idx, idx_c], axis=1)
        t_tot = h2.shape[0] // TOKEN_ROWS
        n_assign = t_tot * TOP_K
        row_tok, row_dst, blk_e, n_used = _route(idx, cnt[:, 0], tm_moe)
        ys = _moe(h2, row_tok, row_dst, blk_e + i * N_EXPERTS, n_used, w1_all, b1_all, w2_all, b2_all,
                  n_assign + tm_moe, tm_moe)
        last = i == depth - 1
        fg = final_norm_g.reshape(1, d) if last else None
        xl = _combine(ys, t_tot, 0, gate, xl, mod_lat, 5, s, fg, tq_comb)
        if ctx_out:
            xc = _combine(ys, t_tot, m_lat, gate_c, xc, mod_ctx, 5, m_ctx, None, tq_comb)
    return xl.reshape(bn, s, d)
```
